```python
import math, functools
import jax, jax.numpy as jnp
from jax import lax
import numpy as np

D_MODEL = 1024
BATCH = 2
SEQ = 8192
DEPTH = 1
DEC_BATCH = 128
DEC_SEQ = 1
PAST_LEN = 16384
PAGE_SIZE = 128

HEAD_DIM = 64
N_HEADS_MIX = D_MODEL // HEAD_DIM
H_A = N_HEADS_MIX // 2
H_B = N_HEADS_MIX - H_A
KV_B = 2
G_B = H_B // KV_B
DILATED_GROUPS = ((128, 1), (512, 4), (2048, 16))
WIN_A = max(w for w, _ in DILATED_GROUPS)
WIN_B = 128
QBLK = 128
N_BUCKETS = 32
MAX_DISTANCE = 2048
N_KEYS = 128
N_EXPERTS = N_KEYS * N_KEYS
PEER_HEADS = 8
PEER_TOPK = 16
D_KEY = 128
PEER_BLOCK = 128
RMS_EPS = 1e-6
SCALE = HEAD_DIM ** -0.5
W_A = H_A * HEAD_DIM
W_BQ = H_B * HEAD_DIM
W_BKV = KV_B * HEAD_DIM
IN_SPLITS = (W_A, 2 * W_A, 3 * W_A, 3 * W_A + W_BQ, 3 * W_A + W_BQ + W_BKV)
D_IN = 3 * W_A + W_BQ + 2 * W_BKV
D_MIX = W_A + W_BQ

kernel_name = "hybrid_dilated_swa_peer_decode_step"


def rmsnorm(x, g):
    xf = x.astype(jnp.float32)
    y = xf * lax.rsqrt(jnp.mean(xf * xf, axis=-1, keepdims=True) + RMS_EPS)
    return (y * g.astype(jnp.float32)).astype(x.dtype)


def t5_bucket(dist):
    exact = N_BUCKETS // 2
    d = jnp.maximum(dist, 0)
    logd = jnp.log(jnp.maximum(d, 1).astype(jnp.float32) / exact) / math.log(MAX_DISTANCE / exact)
    large = jnp.minimum(exact + (logd * (N_BUCKETS - exact)).astype(jnp.int32), N_BUCKETS - 1)
    return jnp.where(d < exact, d, large)


def qkv_split(xn, w_in):
    n, L, _ = xn.shape
    h = jnp.einsum('nld,de->nle', xn, w_in)
    qa, ka, va, qb, kb, vb = jnp.split(h, IN_SPLITS, axis=-1)
    return (qa.reshape(n, L, H_A, HEAD_DIM), ka.reshape(n, L, H_A, HEAD_DIM), va.reshape(n, L, H_A, HEAD_DIM),
            qb.reshape(n, L, KV_B, G_B, HEAD_DIM), kb.reshape(n, L, KV_B, HEAD_DIM), vb.reshape(n, L, KV_B, HEAD_DIM))


def dilated_mixture(q, k_ctx, v_ctx, q_off, n_invalid, rel_bias):
    nq = q.shape[1]
    qi = jnp.arange(nq, dtype=jnp.int32)[:, None]
    outs, lses = [], []
    for w, d in DILATED_GROUPS:
        m = jnp.arange(w // d + 1, dtype=jnp.int32)[None, :]
        idx = q_off + qi - m * d
        valid = idx >= n_invalid
        idx = jnp.maximum(idx, 0)
        kg = k_ctx[:, idx]
        vg = v_ctx[:, idx]
        bias = rel_bias[t5_bucket(m[0] * d), :H_A].T.astype(jnp.float32)
        logits = jnp.einsum('nqhd,nqshd->nhqs', q, kg, preferred_element_type=jnp.float32) * SCALE
        logits = jnp.where(valid[None, None], logits + bias[None, :, None, :], -jnp.inf)
        mx = jnp.max(logits, axis=-1, keepdims=True)
        e = jnp.exp(logits - mx)
        s = jnp.sum(e, axis=-1, keepdims=True)
        outs.append(jnp.einsum('nhqs,nqshd->nqhd', e / s, vg))
        lses.append((mx + jnp.log(s))[..., 0])
    wts = jax.nn.softmax(jnp.stack(lses), axis=0)
    wts = jnp.transpose(wts, (0, 1, 3, 2))[..., None]
    return jnp.sum(wts * jnp.stack(outs), axis=0)


def mixer_a_prompt(qa, ka, va, rel_bias):
    n, s = qa.shape[:2]
    nb = s // QBLK
    pad = ((0, 0), (WIN_A, 0), (0, 0), (0, 0))
    kp, vp = jnp.pad(ka, pad), jnp.pad(va, pad)
    qblocks = jnp.moveaxis(qa.reshape(n, nb, QBLK, H_A, HEAD_DIM), 1, 0)

    def one_block(args):
        b, qblk = args
        s0 = b * QBLK
        kc = lax.dynamic_slice_in_dim(kp, s0, QBLK + WIN_A, axis=1)
        vc = lax.dynamic_slice_in_dim(vp, s0, QBLK + WIN_A, axis=1)
        return dilated_mixture(qblk, kc, vc, WIN_A, WIN_A - s0, rel_bias)

    o = lax.map(one_block, (jnp.arange(nb, dtype=jnp.int32), qblocks))
    return jnp.moveaxis(o, 0, 1).reshape(n, s, H_A, HEAD_DIM)


def window_sink_attn(q, k, v, dist, key_ok, rel_bias, sinks):
    mask = ((dist >= 0) & (dist <= WIN_B))[None] & key_ok[:, None, :]
    nq, nc = dist.shape
    bias = rel_bias[t5_bucket(dist), H_A:].astype(jnp.float32).reshape(nq, nc, KV_B, G_B)
    bias = jnp.transpose(bias, (2, 3, 0, 1))
    logits = jnp.einsum('nqkgd,nckd->nkgqc', q, k, preferred_element_type=jnp.float32) * SCALE + bias[None]
    logits = jnp.where(mask[:, None, None], logits, -jnp.inf)
    sink = sinks.astype(jnp.float32).reshape(1, KV_B, G_B, 1, 1)
    mx = jnp.maximum(jnp.max(logits, axis=-1, keepdims=True), sink)
    e = jnp.exp(logits - mx)
    p = e / (jnp.sum(e, axis=-1, keepdims=True) + jnp.exp(sink - mx))
    return jnp.einsum('nkgqc,nckd->nqkgd', p, v)


def mixer_b_prompt(qb, kb, vb, rel_bias, sinks):
    n, s = qb.shape[:2]
    nb = s // QBLK

    def band(t):
        t = t.reshape(n, nb, QBLK, KV_B, HEAD_DIM)
        prev = jnp.concatenate([jnp.zeros_like(t[:, :1]), t[:, :-1]], axis=1)
        return jnp.concatenate([prev, t], axis=2).reshape(n * nb, 2 * QBLK, KV_B, HEAD_DIM)

    i = jnp.arange(QBLK, dtype=jnp.int32)[:, None]
    j = jnp.arange(2 * QBLK, dtype=jnp.int32)[None, :]
    dist = QBLK + i - j
    key_ok = (jnp.arange(nb, dtype=jnp.int32)[:, None] * QBLK - QBLK + j) >= 0
    key_ok = jnp.broadcast_to(key_ok[None], (n, nb, 2 * QBLK)).reshape(n * nb, 2 * QBLK)
    o = window_sink_attn(qb.reshape(n * nb, QBLK, KV_B, G_B, HEAD_DIM), band(kb), band(vb),
                         dist, key_ok, rel_bias, sinks)
    return o.reshape(n, s, KV_B, G_B, HEAD_DIM)


def peer_block(xb, w_pq, sub_keys, down, up):
    t = xb.shape[0]
    q = jnp.einsum('td,de->te', xb, w_pq).reshape(t, PEER_HEADS, 2, D_KEY // 2)
    s = jnp.einsum('thcd,ckd->thck', q, sub_keys, preferred_element_type=jnp.float32)
    sv, si = lax.top_k(s, PEER_TOPK)
    cand = (sv[:, :, 0, :, None] + sv[:, :, 1, None, :]).reshape(t, PEER_HEADS, PEER_TOPK * PEER_TOPK)
    cidx = (si[:, :, 0, :, None] * N_KEYS + si[:, :, 1, None, :]).reshape(t, PEER_HEADS, PEER_TOPK * PEER_TOPK)
    fv, fi = lax.top_k(cand, PEER_TOPK)
    eidx = jnp.take_along_axis(cidx, fi, axis=-1)
    g = jax.nn.softmax(fv, axis=-1)
    a = jax.nn.gelu(jnp.einsum('td,thkd->thk', xb, down[eidx], preferred_element_type=jnp.float32),
                    approximate=False)
    return jnp.einsum('thk,thkd->td', g * a, up[eidx]).astype(xb.dtype)


def peer_ffn(xt, w_pq, sub_keys, down, up):
    t = xt.shape[0]
    nb = -(-t // PEER_BLOCK)
    xp = jnp.pad(xt, ((0, nb * PEER_BLOCK - t), (0, 0))).reshape(nb, PEER_BLOCK, xt.shape[1])
    y = lax.map(functools.partial(peer_block, w_pq=w_pq, sub_keys=sub_keys, down=down, up=up), xp)
    return y.reshape(nb * PEER_BLOCK, xt.shape[1])[:t]


def block_tail(x, oa, ob, w_out, norm_ffn, w_pq, sub_keys, down, up):
    n, L, d = x.shape
    o = jnp.concatenate([oa.reshape(n, L, W_A), ob.reshape(n, L, W_BQ)], axis=-1).astype(x.dtype)
    x = x + jnp.einsum('nle,ed->nld', o, w_out)
    xn = rmsnorm(x, norm_ffn)
    return x + peer_ffn(xn.reshape(n * L, d), w_pq, sub_keys, down, up).reshape(n, L, d)


def setup_inputs(seed: int = 0) -> dict:
    key = jax.random.key(seed)
    ks = jax.random.split(key, 20)
    f32 = jnp.float32
    la = min(WIN_A, PAST_LEN)
    lb = min(WIN_B, PAST_LEN)

    def nrm(k, shape, s):
        return jax.random.normal(k, shape, f32) * s

    return {
        "x_prompt": nrm(ks[0], (BATCH, SEQ, D_MODEL), 1.0),
        "x_sample": nrm(ks[1], (DEC_BATCH, DEC_SEQ, D_MODEL), 1.0),
        "cache_a_k": nrm(ks[2], (DEPTH, DEC_BATCH, la, H_A, HEAD_DIM), 1.0),
        "cache_a_v": nrm(ks[3], (DEPTH, DEC_BATCH, la, H_A, HEAD_DIM), 1.0),
        "cache_b_k": nrm(ks[4], (DEPTH, DEC_BATCH, lb, KV_B, HEAD_DIM), 1.0),
        "cache_b_v": nrm(ks[5], (DEPTH, DEC_BATCH, lb, KV_B, HEAD_DIM), 1.0),
        "norm_attn": 1.0 + nrm(ks[6], (DEPTH, D_MODEL), 0.05),
        "w_in": nrm(ks[7], (DEPTH, D_MODEL, D_IN), D_MODEL ** -0.5),
        "rel_bias": nrm(ks[8], (N_BUCKETS, H_A + H_B), 0.5),
        "sinks": nrm(ks[9], (DEPTH, H_B), 0.5),
        "w_out": nrm(ks[10], (DEPTH, D_MIX, D_MODEL), D_MIX ** -0.5),
        "norm_ffn": 1.0 + nrm(ks[11], (DEPTH, D_MODEL), 0.05),
        "w_peer_q": nrm(ks[12], (DEPTH, D_MODEL, PEER_HEADS * D_KEY), D_MODEL ** -0.5),
        "peer_sub_keys": nrm(ks[13], (DEPTH, 2, N_KEYS, D_KEY // 2), (D_KEY // 2) ** -0.5),
        "peer_down": nrm(ks[14], (DEPTH, N_EXPERTS, D_MODEL), D_MODEL ** -0.5),
        "peer_up": nrm(ks[15], (DEPTH, N_EXPERTS, D_MODEL), PEER_HEADS ** -0.5),
        "norm_final": 1.0 + nrm(ks[16], (D_MODEL,), 0.05),
    }


def reference(x_prompt, x_sample, cache_a_k, cache_a_v, cache_b_k, cache_b_v, norm_attn, w_in, rel_bias,
              sinks, w_out, norm_ffn, w_peer_q, peer_sub_keys, peer_down, peer_up, norm_final):
    xp, xs = x_prompt, x_sample
    p_ak, p_av, p_bk, p_bv = [], [], [], []
    s_ak, s_av, s_bk, s_bv = [], [], [], []
    for l in range(DEPTH):
        qa, ka, va, qb, kb, vb = qkv_split(rmsnorm(xp, norm_attn[l]), w_in[l])
        oa = mixer_a_prompt(qa, ka, va, rel_bias)
        ob = mixer_b_prompt(qb, kb, vb, rel_bias, sinks[l])
        xp = block_tail(xp, oa, ob, w_out[l], norm_ffn[l], w_peer_q[l], peer_sub_keys[l], peer_down[l], peer_up[l])
        s_len = ka.shape[1]
        p_ak.append(ka[:, s_len - min(WIN_A, s_len):])
        p_av.append(va[:, s_len - min(WIN_A, s_len):])
        p_bk.append(kb[:, s_len - min(WIN_B, s_len):])
        p_bv.append(vb[:, s_len - min(WIN_B, s_len):])

        qa, ka, va, qb, kb, vb = qkv_split(rmsnorm(xs, norm_attn[l]), w_in[l])
        nq = xs.shape[1]
        la = cache_a_k.shape[2]
        lb = cache_b_k.shape[2]
        ka_ctx = jnp.concatenate([cache_a_k[l], ka.astype(cache_a_k.dtype)], axis=1)
        va_ctx = jnp.concatenate([cache_a_v[l], va.astype(cache_a_v.dtype)], axis=1)
        kb_ctx = jnp.concatenate([cache_b_k[l], kb.astype(cache_b_k.dtype)], axis=1)
        vb_ctx = jnp.concatenate([cache_b_v[l], vb.astype(cache_b_v.dtype)], axis=1)
        oa = dilated_mixture(qa, ka_ctx, va_ctx, la, 0, rel_bias)
        dist = lb + jnp.arange(nq, dtype=jnp.int32)[:, None] - jnp.arange(lb + nq, dtype=jnp.int32)[None, :]
        key_ok = jnp.ones((xs.shape[0], lb + nq), dtype=bool)
        ob = window_sink_attn(qb, kb_ctx, vb_ctx, dist, key_ok, rel_bias, sinks[l])
        xs = block_tail(xs, oa, ob, w_out[l], norm_ffn[l], w_peer_q[l], peer_sub_keys[l], peer_down[l], peer_up[l])
        s_ak.append(ka_ctx[:, la + nq - min(WIN_A, la + nq):])
        s_av.append(va_ctx[:, la + nq - min(WIN_A, la + nq):])
        s_bk.append(kb_ctx[:, lb + nq - min(WIN_B, lb + nq):])
        s_bv.append(vb_ctx[:, lb + nq - min(WIN_B, lb + nq):])

    y_prompt = rmsnorm(xp, norm_final)
    y_sample = rmsnorm(xs, norm_final)
    return (y_prompt, y_sample, jnp.stack(p_ak), jnp.stack(p_av), jnp.stack(p_bk), jnp.stack(p_bv),
            jnp.stack(s_ak), jnp.stack(s_av), jnp.stack(s_bk), jnp.stack(s_bv))
```

```python
import functools
import math

import numpy as np
import jax
import jax.numpy as jnp
from jax import lax
from jax.experimental import pallas as pl
from jax.experimental.pallas import tpu as pltpu

HEAD_DIM = 64
H_A = 8
H_B = 8
KV_B = 2
G_B = H_B // KV_B
DILATIONS = (1, 4, 16)
SLOTS = 128
N_BUCKETS = 32
MAX_DISTANCE = 2048
N_KEYS = 128
PEER_HEADS = 8
PEER_TOPK = 16
D_KEY = 128
RMS_EPS = 1e-6
SCALE = HEAD_DIM ** -0.5
W_A = H_A * HEAD_DIM
W_BQ = H_B * HEAD_DIM
W_BKV = KV_B * HEAD_DIM

LANES = 128
SUBLANES = 8
VMEM_LIMIT_BYTES = 56 * 1024 * 1024

MXU_DTYPE = jnp.bfloat16
F32 = jnp.float32
NEG_INF = float("-inf")


def _cparams(sem):
    return pltpu.CompilerParams(dimension_semantics=sem, vmem_limit_bytes=VMEM_LIMIT_BYTES)


def _t5_bucket_np(dist):
    exact = N_BUCKETS // 2
    d = np.maximum(dist, 0)
    logd = np.log(np.maximum(d, 1).astype(np.float32) / np.float32(exact)) / np.float32(math.log(MAX_DISTANCE / exact))
    large = np.minimum(exact + (logd * np.float32(N_BUCKETS - exact)).astype(np.int32), N_BUCKETS - 1)
    return np.where(d < exact, d, large).astype(np.int32)


def _window_bucket_matrix(dilation):
    i = np.arange(SLOTS)[:, None]
    j = np.arange(2 * SLOTS)[None, :]
    dist = i - j + SLOTS
    ok = (dist >= 0) & (dist <= SLOTS)
    return np.where(ok, _t5_bucket_np(dist * dilation), -1).astype(np.int32)


def _sample_tables(cache_len):
    delta = cache_len - np.arange(cache_len)
    mult = np.zeros(cache_len, np.float32)
    for d in DILATIONS:
        mult += ((delta % d == 0) & (delta <= SLOTS * d)).astype(np.float32)
    return _t5_bucket_np(delta)[None, :], mult[None, :]


def _residue_perm(tb, d):
    p = np.zeros((tb, tb), np.float32)
    rows = np.arange(tb)
    p[rows, (rows % (tb // d)) * d + rows // (tb // d)] = 1.0
    return p


def _rms(x, g):
    return x * lax.rsqrt(jnp.mean(x * x, axis=-1, keepdims=True) + RMS_EPS) * g


def _dot(a, b):
    return jnp.dot(a, b, preferred_element_type=F32)


def _dot_nt(a, b):
    return lax.dot_general(a, b, (((1,), (1,)), ((), ())), preferred_element_type=F32)


def _split2(x):
    hi = x.astype(MXU_DTYPE)
    lo = (x - hi.astype(F32)).astype(MXU_DTYPE)
    return hi, lo


def _split3(x):
    hi = x.astype(MXU_DTYPE)
    r1 = x - hi.astype(F32)
    mid = r1.astype(MXU_DTYPE)
    lo = (r1 - mid.astype(F32)).astype(MXU_DTYPE)
    return hi, mid, lo


def _bias_kernel(col0, nheads, bidx_ref, rb_ref, o_ref):
    b = bidx_ref[...]
    for h in range(nheads):
        val = jnp.full(b.shape, NEG_INF, F32)
        for k in range(N_BUCKETS):
            val = jnp.where(b == k, rb_ref[k, col0 + h], val)
        o_ref[h] = val


def _bias_table(bidx, rel_bias, col0, nheads):
    r, c = bidx.shape
    return pl.pallas_call(
        functools.partial(_bias_kernel, col0, nheads),
        out_shape=jax.ShapeDtypeStruct((nheads, r, c), F32),
        in_specs=[pl.BlockSpec((r, c), lambda: (0, 0)), pl.BlockSpec(memory_space=pltpu.SMEM)],
        out_specs=pl.BlockSpec((nheads, r, c), lambda: (0, 0, 0)),
        name="bias_table",
    )(jnp.asarray(bidx), rel_bias)


QKV_TB = 512


def _qkv_prompt_kernel(x_ref, g_ref, w_ref, p4_ref, p16_ref,
                       q1_ref, k1_ref, v1_ref, kf_ref, vf_ref,
                       q4_ref, k4_ref, v4_ref, q16_ref, k16_ref, v16_ref,
                       qb_ref, qbs_ref, kb_ref, vb_ref, vbs_ref, kbf_ref, vbf_ref):
    xn = _rms(x_ref[0], g_ref[...])
    h = _dot(xn.astype(MXU_DTYPE), w_ref[...])
    c0, c1, c2, c3 = 0, W_A, 2 * W_A, 3 * W_A
    c4, c5, c6 = c3 + W_BQ, c3 + W_BQ + W_BKV, c3 + W_BQ + 2 * W_BKV
    c7 = c6 + W_BQ
    ha = h[:, :c3].astype(MXU_DTYPE)
    q1_ref[0] = ha[:, c0:c1]
    k1_ref[0] = ha[:, c1:c2]
    v1_ref[0] = ha[:, c2:c3]
    kf_ref[0] = h[:, c1:c2]
    vf_ref[0] = h[:, c2:c3]
    qb_ref[0] = h[:, c3:c4].astype(MXU_DTYPE)
    kb_ref[0] = h[:, c4:c5].astype(MXU_DTYPE)
    vb_ref[0] = h[:, c5:c6].astype(MXU_DTYPE)
    kbf_ref[0] = h[:, c4:c5]
    vbf_ref[0] = h[:, c5:c6]
    qbs_ref[0] = h[:, c6:c7].astype(MXU_DTYPE)
    vbs_ref[0] = h[:, c7:].astype(MXU_DTYPE)
    tb = ha.shape[0]
    for d, p_ref, outs in ((4, p4_ref, (q4_ref, k4_ref, v4_ref)), (16, p16_ref, (q16_ref, k16_ref, v16_ref))):
        perm = _dot(p_ref[...], ha).astype(MXU_DTYPE)
        rows = tb // d
        for r in range(d):
            for t, o_ref in enumerate(outs):
                o_ref[0, r] = perm[r * rows:(r + 1) * rows, t * W_A:(t + 1) * W_A]


def _qkv_prompt(x, g, w_ext):
    n, s, dm = x.shape
    tb = QKV_TB
    nb = s // tb
    p4 = jnp.asarray(_residue_perm(tb, 4), MXU_DTYPE)
    p16 = jnp.asarray(_residue_perm(tb, 16), MXU_DTYPE)
    bf = MXU_DTYPE

    def nat(width, dt):
        return jax.ShapeDtypeStruct((n, s, width), dt), pl.BlockSpec((1, tb, width), lambda i, j: (i, j, 0))

    def res(d):
        return (jax.ShapeDtypeStruct((n, d, s // d, W_A), bf),
                pl.BlockSpec((1, d, tb // d, W_A), lambda i, j: (i, 0, j, 0)))

    outs = [nat(W_A, bf), nat(W_A, bf), nat(W_A, bf), nat(W_A, F32), nat(W_A, F32),
            res(4), res(4), res(4), res(16), res(16), res(16),
            nat(W_BQ, bf), nat(W_BQ, bf), nat(W_BKV, bf), nat(W_BKV, bf), nat(W_BKV, bf),
            nat(W_BKV, F32), nat(W_BKV, F32)]
    return pl.pallas_call(
        _qkv_prompt_kernel,
        grid=(n, nb),
        in_specs=[pl.BlockSpec((1, tb, dm), lambda i, j: (i, j, 0)),
                  pl.BlockSpec((1, dm), lambda i, j: (0, 0)),
                  pl.BlockSpec(w_ext.shape, lambda i, j: (0, 0)),
                  pl.BlockSpec((tb, tb), lambda i, j: (0, 0)),
                  pl.BlockSpec((tb, tb), lambda i, j: (0, 0))],
        out_shape=[o[0] for o in outs],
        out_specs=[o[1] for o in outs],
        compiler_params=_cparams(("parallel", "parallel")),
        name="qkv_prompt",
    )(x, g, w_ext, p4, p16)


def _qkv_sample_kernel(x_ref, g_ref, w_ref, o_ref):
    xn = _rms(x_ref[...], g_ref[...])
    o_ref[...] = _dot(xn.astype(MXU_DTYPE), w_ref[...])


def _qkv_sample(x, g, w):
    t = x.shape[0]
    return pl.pallas_call(
        _qkv_sample_kernel,
        out_shape=jax.ShapeDtypeStruct((t, w.shape[1]), F32),
        compiler_params=_cparams(None),
        name="qkv_sample",
    )(x, g, w)


def _swa_kernel(head_cfg, has_sink, want_lse, *refs):
    it = iter(refs)
    tab_ref = next(it)
    sink_ref = next(it) if has_sink else None
    q_refs = [next(it)]
    if any(c[0] == 1 for c in head_cfg):
        q_refs.append(next(it))
    kc_ref, kp_ref = next(it), next(it)
    v_refs = [(next(it), next(it))]
    if any(c[3] == 1 for c in head_cfg):
        v_refs.append((next(it), next(it)))
    o_ref = next(it)
    lse_ref = next(it) if want_lse else None

    first = pl.program_id(1) == 0
    col = lax.broadcasted_iota(jnp.int32, (SLOTS, 2 * SLOTS), 1)
    prev_pen = jnp.where((col < SLOTS) & first, NEG_INF, 0.0)
    lane = lax.broadcasted_iota(jnp.int32, (SLOTS, LANES), 1)
    low = lane < HEAD_DIM
    lse_acc = jnp.zeros((SLOTS, LANES), F32)
    for p in range(len(head_cfg) // 2):
        halves = []
        for hh in range(2):
            h = 2 * p + hh
            q_src, q_half, k_tile, v_src, v_tile = head_cfg[h]
            q = q_refs[q_src][0, :, p * LANES:(p + 1) * LANES]
            q = jnp.where(low if q_half == 0 else jnp.logical_not(low), q, jnp.zeros_like(q))
            ks = slice(k_tile * LANES, (k_tile + 1) * LANES)
            kcat = jnp.concatenate([kp_ref[0, :, ks], kc_ref[0, :, ks]], axis=0)
            s = _dot_nt(q, kcat) + tab_ref[h] + prev_pen
            m = jnp.max(s, axis=-1, keepdims=True)
            if has_sink:
                m = jnp.maximum(m, sink_ref[h])
            e = jnp.exp(s - m)
            l = jnp.sum(e, axis=-1, keepdims=True)
            if has_sink:
                l = l + jnp.exp(sink_ref[h] - m)
            vc_ref, vp_ref = v_refs[v_src]
            vs = slice(v_tile * LANES, (v_tile + 1) * LANES)
            vcat = jnp.concatenate([vp_ref[0, :, vs], vc_ref[0, :, vs]], axis=0)
            halves.append(_dot(e.astype(MXU_DTYPE), vcat) / l)
            if want_lse:
                lse_acc = jnp.where(lane == h, m + jnp.log(l), lse_acc)
        o_ref[0, :, p * LANES:(p + 1) * LANES] = jnp.where(low, halves[0], halves[1]).astype(o_ref.dtype)
    if want_lse:
        lse_ref[0] = lse_acc


def _swa(tab, q_list, k, v_list, head_cfg, sinks=None, want_lse=True):
    r, s, _ = q_list[0].shape
    ck = k.shape[-1]
    nb = s // SLOTS
    cur = lambda i, j: (i, j, 0)
    prev = lambda i, j: (i, jnp.maximum(j - 1, 0), 0)
    in_specs = [pl.BlockSpec(tab.shape, lambda i, j: (0, 0, 0))]
    args = [tab]
    if sinks is not None:
        in_specs.append(pl.BlockSpec(memory_space=pltpu.SMEM))
        args.append(sinks)
    for q in q_list:
        in_specs.append(pl.BlockSpec((1, SLOTS, q.shape[-1]), cur))
        args.append(q)
    in_specs += [pl.BlockSpec((1, SLOTS, ck), cur), pl.BlockSpec((1, SLOTS, ck), prev)]
    args += [k, k]
    for v in v_list:
        in_specs += [pl.BlockSpec((1, SLOTS, ck), cur), pl.BlockSpec((1, SLOTS, ck), prev)]
        args += [v, v]
    out_shape = [jax.ShapeDtypeStruct((r, s, W_A), MXU_DTYPE)]
    out_specs = [pl.BlockSpec((1, SLOTS, W_A), cur)]
    if want_lse:
        out_shape.append(jax.ShapeDtypeStruct((r, s, LANES), F32))
        out_specs.append(pl.BlockSpec((1, SLOTS, LANES), cur))
    out = pl.pallas_call(
        functools.partial(_swa_kernel, tuple(head_cfg), sinks is not None, want_lse),
        grid=(r, nb),
        in_specs=in_specs,
        out_shape=out_shape,
        out_specs=out_specs,
        compiler_params=_cparams(("parallel", "arbitrary")),
        name="swa",
    )(*args)
    return out if want_lse else out[0]


HEAD_CFG_A = tuple((0, h % 2, h // 2, 0, h // 2) for h in range(H_A))


def _head_cfg_b():
    cfg = []
    for h in range(H_B):
        c = h // G_B
        src = 0 if h % 2 == c else 1
        cfg.append((src, c, 0, src, 0))
    return tuple(cfg)


HEAD_CFG_B = _head_cfg_b()


def _shift_in(x, new_col):
    length = x.shape[-1]
    rolled = pltpu.roll(x, length - 1, axis=1)
    lane = lax.broadcasted_iota(jnp.int32, x.shape, 1)
    return jnp.where(lane == length - 1, new_col, rolled)


def _col_attention(q, kmat, vmat, k_new, v_new, bias_row, mult_row, bias_new, mult_new, sink):
    s = jnp.sum(q * kmat, axis=0, keepdims=True) + bias_row
    s_new = jnp.sum(q * k_new, axis=0, keepdims=True) + bias_new
    if mult_row is not None:
        s = jnp.where(mult_row > 0.0, s, NEG_INF)
    m = jnp.maximum(jnp.max(s, axis=-1, keepdims=True), s_new)
    if sink is not None:
        m = jnp.maximum(m, sink)
    e = jnp.exp(s - m)
    if mult_row is not None:
        e = e * mult_row
    e_new = mult_new * jnp.exp(s_new - m)
    l = jnp.sum(e, axis=-1, keepdims=True) + e_new
    if sink is not None:
        l = l + jnp.exp(sink - m)
    o = jnp.sum(vmat * e, axis=-1, keepdims=True) + v_new * e_new
    return o / l


def _sample_a_kernel(heads_per_step, bias0_ref, tab_ref, mult_ref, q_ref, kn_ref, vn_ref, k_ref, v_ref,
                     o_ref, ko_ref, vo_ref):
    hb = pl.program_id(1)
    mult = mult_ref[...]
    for hl in range(heads_per_step):
        rows = slice(hl * HEAD_DIM, (hl + 1) * HEAD_DIM)
        h = hb * heads_per_step + hl
        kmat, vmat = k_ref[0, hl], v_ref[0, hl]
        k_new, v_new = kn_ref[0, rows], vn_ref[0, rows]
        o_ref[0, rows] = _col_attention(q_ref[0, rows], kmat, vmat, k_new, v_new, tab_ref[hl], mult,
                                        bias0_ref[h], float(len(DILATIONS)), None)
        ko_ref[0, hl] = _shift_in(kmat, k_new)
        vo_ref[0, hl] = _shift_in(vmat, v_new)


def _sample_a(q_col, kn_col, vn_col, k_t, v_t, tab, mult, bias0):
    ns, nh, hd, length = k_t.shape
    hps = 4
    col_spec = pl.BlockSpec((1, hps * hd, 1), lambda i, j: (i, j, 0))
    cache_spec = pl.BlockSpec((1, hps, hd, length), lambda i, j: (i, j, 0, 0))
    return pl.pallas_call(
        functools.partial(_sample_a_kernel, hps),
        grid=(ns, nh // hps),
        in_specs=[pl.BlockSpec(memory_space=pltpu.SMEM),
                  pl.BlockSpec((hps, 1, length), lambda i, j: (j, 0, 0)),
                  pl.BlockSpec((1, length), lambda i, j: (0, 0)),
                  col_spec, col_spec, col_spec, cache_spec, cache_spec],
        out_shape=[jax.ShapeDtypeStruct((ns, nh * hd, 1), F32),
                   jax.ShapeDtypeStruct(k_t.shape, F32), jax.ShapeDtypeStruct(v_t.shape, F32)],
        out_specs=[col_spec, cache_spec, cache_spec],
        compiler_params=_cparams(("parallel", "arbitrary")),
        name="sample_attn_a",
    )(bias0, tab, mult, q_col, kn_col, vn_col, k_t, v_t)


def _sample_b_kernel(bias0_ref, sink_ref, tab_ref, q_ref, kn_ref, vn_ref, k_ref, v_ref, o_ref, ko_ref, vo_ref):
    for h in range(H_B):
        c = h // G_B
        rows = slice(h * HEAD_DIM, (h + 1) * HEAD_DIM)
        crow = slice(c * HEAD_DIM, (c + 1) * HEAD_DIM)
        o_ref[0, rows] = _col_attention(q_ref[0, rows], k_ref[0, c], v_ref[0, c], kn_ref[0, crow], vn_ref[0, crow],
                                        tab_ref[h], None, bias0_ref[h], 1.0, sink_ref[h])
    for c in range(KV_B):
        crow = slice(c * HEAD_DIM, (c + 1) * HEAD_DIM)
        ko_ref[0, c] = _shift_in(k_ref[0, c], kn_ref[0, crow])
        vo_ref[0, c] = _shift_in(v_ref[0, c], vn_ref[0, crow])


def _sample_b(q_col, kn_col, vn_col, k_t, v_t, tab, bias0, sinks):
    ns, nkv, hd, length = k_t.shape
    qspec = pl.BlockSpec((1, H_B * hd, 1), lambda i: (i, 0, 0))
    nspec = pl.BlockSpec((1, nkv * hd, 1), lambda i: (i, 0, 0))
    cspec = pl.BlockSpec((1, nkv, hd, length), lambda i: (i, 0, 0, 0))
    return pl.pallas_call(
        _sample_b_kernel,
        grid=(ns,),
        in_specs=[pl.BlockSpec(memory_space=pltpu.SMEM), pl.BlockSpec(memory_space=pltpu.SMEM),
                  pl.BlockSpec(tab.shape, lambda i: (0, 0, 0)), qspec, nspec, nspec, cspec, cspec],
        out_shape=[jax.ShapeDtypeStruct((ns, H_B * hd, 1), F32),
                   jax.ShapeDtypeStruct(k_t.shape, F32), jax.ShapeDtypeStruct(v_t.shape, F32)],
        out_specs=[qspec, cspec, cspec],
        compiler_params=_cparams(("parallel",)),
        name="sample_attn_b",
    )(bias0, sinks, tab, q_col, kn_col, vn_col, k_t, v_t)


TAIL_TB = 512


def _unpermute(pt_ref, blocks_ref, exact):
    d = blocks_ref.shape[1]
    x = jnp.concatenate([blocks_ref[0, r] for r in range(d)], axis=0)
    if not exact:
        return _dot(pt_ref[...], x)
    return sum(_dot(pt_ref[...], part) for part in _split3(x))


def _tail_prompt_kernel(x_ref, o1_ref, l1_ref, o4_ref, l4_ref, o16_ref, l16_ref, ob_ref, pt4_ref, pt16_ref,
                        w_ref, g_ref, h_ref, xn_ref):
    o_g = [o1_ref[0].astype(F32), _unpermute(pt4_ref, o4_ref, False), _unpermute(pt16_ref, o16_ref, False)]
    l_g = [l1_ref[0], _unpermute(pt4_ref, l4_ref, True), _unpermute(pt16_ref, l16_ref, True)]
    m = jnp.maximum(jnp.maximum(l_g[0], l_g[1]), l_g[2])
    e_g = [jnp.exp(l - m) for l in l_g]
    den = e_g[0] + e_g[1] + e_g[2]
    w_g = [e / den for e in e_g]
    lane = lax.broadcasted_iota(jnp.int32, (x_ref.shape[1], LANES), 1)
    low = lane < HEAD_DIM
    parts = []
    for p in range(H_A // 2):
        acc = None
        for w, o in zip(w_g, o_g):
            wp = jnp.where(low, w[:, 2 * p:2 * p + 1], w[:, 2 * p + 1:2 * p + 2])
            term = wp * o[:, p * LANES:(p + 1) * LANES]
            acc = term if acc is None else acc + term
        parts.append(acc.astype(MXU_DTYPE))
    o = jnp.concatenate(parts + [ob_ref[0]], axis=-1)
    h = x_ref[0] + _dot(o, w_ref[...])
    h_ref[0] = h
    xn_ref[0] = _rms(h, g_ref[...])


def _tail_prompt(x, o1, l1, o4, l4, o16, l16, ob, w_out, g):
    n, s, dm = x.shape
    tb = TAIL_TB
    pt4 = jnp.asarray(_residue_perm(tb, 4).T, MXU_DTYPE)
    pt16 = jnp.asarray(_residue_perm(tb, 16).T, MXU_DTYPE)
    nat = lambda w: pl.BlockSpec((1, tb, w), lambda i, j: (i, j, 0))
    res = lambda d, w: pl.BlockSpec((1, d, tb // d, w), lambda i, j: (i, 0, j, 0))
    const = lambda a: pl.BlockSpec(a.shape, lambda i, j: (0,) * a.ndim)
    return pl.pallas_call(
        _tail_prompt_kernel,
        grid=(n, s // tb),
        in_specs=[nat(dm), nat(W_A), nat(LANES), res(4, W_A), res(4, LANES), res(16, W_A), res(16, LANES),
                  nat(W_BQ), const(pt4), const(pt16), const(w_out), const(g)],
        out_shape=[jax.ShapeDtypeStruct((n, s, dm), F32), jax.ShapeDtypeStruct((n, s, dm), F32)],
        out_specs=[nat(dm), nat(dm)],
        compiler_params=_cparams(("parallel", "parallel")),
        name="tail_prompt",
    )(x, o1, l1, o4, l4, o16, l16, ob, pt4, pt16, w_out, g)


def _tail_sample_kernel(x_ref, o_ref, w_ref, g_ref, h_ref, xn_ref):
    h = x_ref[...] + _dot(o_ref[...].astype(MXU_DTYPE), w_ref[...])
    h_ref[...] = h
    xn_ref[...] = _rms(h, g_ref[...])


def _tail_sample(x, o, w_out, g):
    return pl.pallas_call(
        _tail_sample_kernel,
        out_shape=[jax.ShapeDtypeStruct(x.shape, F32), jax.ShapeDtypeStruct(x.shape, F32)],
        compiler_params=_cparams(None),
        name="tail_sample",
    )(x, o, w_out, g)


def _top_rows(s, count):
    rows = []
    for _ in range(count):
        m = jnp.max(s, axis=0, keepdims=True)
        rows.append(m)
        s = jnp.where(s == m, NEG_INF, s)
    return rows


def _route_kernel(x_ref, wqh_ref, wql_ref, skh_ref, skl_ref, s1_ref, s2_ref, tau_ref, mp_ref):
    xh, xl = _split2(x_ref[...])
    wqh = wqh_ref[...]
    q_t = _dot_nt(wqh, xh) + _dot_nt(wqh, xl) + _dot_nt(wql_ref[...], xh)
    half = D_KEY // 2
    taus, mps = [], []
    for h in range(PEER_HEADS):
        tops = []
        for c, s_ref in ((0, s1_ref), (1, s2_ref)):
            qh, ql = _split2(q_t[h * D_KEY + c * half:h * D_KEY + (c + 1) * half, :])
            s = _dot(skh_ref[c], qh) + _dot(skh_ref[c], ql) + _dot(skl_ref[c], qh)
            s_ref[h] = s
            tops.append(_top_rows(s, PEER_TOPK))
        v2 = jnp.concatenate(tops[1], axis=0)
        cand = jnp.concatenate([a + v2 for a in tops[0]], axis=0)
        best = _top_rows(cand, PEER_TOPK)
        z = sum(jnp.exp(b - best[0]) for b in best)
        taus.append(best[-1])
        mps.append(best[0] + jnp.log(z))
    tau_ref[...] = jnp.concatenate(taus, axis=0)
    mp_ref[...] = jnp.concatenate(mps, axis=0)


def _route(xn, wq_t_hi, wq_t_lo, sk_hi, sk_lo, tb):
    t, dm = xn.shape
    const = lambda a: pl.BlockSpec(a.shape, lambda i: (0,) * a.ndim)
    sspec = pl.BlockSpec((PEER_HEADS, N_KEYS, tb), lambda i: (0, 0, i))
    rspec = pl.BlockSpec((PEER_HEADS, tb), lambda i: (0, i))
    sshape = jax.ShapeDtypeStruct((PEER_HEADS, N_KEYS, t), F32)
    rshape = jax.ShapeDtypeStruct((PEER_HEADS, t), F32)
    return pl.pallas_call(
        _route_kernel,
        grid=(t // tb,),
        in_specs=[pl.BlockSpec((tb, dm), lambda i: (i, 0)), const(wq_t_hi), const(wq_t_lo), const(sk_hi), const(sk_lo)],
        out_shape=[sshape, sshape, rshape, rshape],
        out_specs=[sspec, sspec, rspec, rspec],
        compiler_params=_cparams(("parallel",)),
        name="peer_route",
    )(xn, wq_t_hi, wq_t_lo, sk_hi, sk_lo)


EXPERT_CHUNK = 2048
INV_SQRT2 = 0.7071067811865476


def _experts_kernel(xn_ref, res_ref, g_ref, down_ref, upt_ref, s1_ref, s2_ref, tau_ref, mp_ref, y_ref,
                    xb_ref, a_ref, p_ref, acc_ref):
    j = pl.program_id(1)
    tb = xn_ref.shape[0]

    @pl.when(j == 0)
    def _():
        xb_ref[...] = xn_ref[...].astype(MXU_DTYPE)
        acc_ref[...] = jnp.zeros_like(acc_ref)

    a_ref[...] = _dot_nt(down_ref[...], xb_ref[...])
    rows_per_chunk = EXPERT_CHUNK // N_KEYS

    def row_group_body(i8, carry):
        s1_base = pl.multiple_of(i8 * SUBLANES, SUBLANES)
        for g in range(tb // LANES):
            ls = slice(g * LANES, (g + 1) * LANES)
            s1_tiles = [s1_ref[h, pl.ds(s1_base, SUBLANES), ls] for h in range(PEER_HEADS)]
            for r in range(SUBLANES):
                r0 = pl.multiple_of((i8 * SUBLANES + r) * N_KEYS, N_KEYS)
                a = a_ref[pl.ds(r0, N_KEYS), ls]
                w = jnp.zeros((N_KEYS, LANES), F32)
                for h in range(PEER_HEADS):
                    tot = s1_tiles[h][r:r + 1, :] + s2_ref[h, :, ls]
                    gate = jnp.exp(tot - mp_ref[h:h + 1, ls])
                    w = w + jnp.where(tot >= tau_ref[h:h + 1, ls], gate, 0.0)
                act = 0.5 * a * (1.0 + lax.erf(a * INV_SQRT2))
                p_ref[pl.ds(r0, N_KEYS), ls] = (w * act).astype(MXU_DTYPE)
        return carry

    lax.fori_loop(0, rows_per_chunk // SUBLANES, row_group_body, 0)
    acc_ref[...] += _dot(upt_ref[...], p_ref[...])

    @pl.when(j == pl.num_programs(1) - 1)
    def _():
        y = res_ref[...] + acc_ref[...].T
        y_ref[...] = _rms(y, g_ref[...])


def _experts(xn, res, g, down, up_t, s1, s2, tau, mp, tb):
    t, dm = xn.shape
    ne = down.shape[0]
    rows_per_chunk = EXPERT_CHUNK // N_KEYS
    tok = pl.BlockSpec((tb, dm), lambda i, j: (i, 0))
    return pl.pallas_call(
        _experts_kernel,
        grid=(t // tb, ne // EXPERT_CHUNK),
        in_specs=[tok, tok, pl.BlockSpec((1, dm), lambda i, j: (0, 0)),
                  pl.BlockSpec((EXPERT_CHUNK, dm), lambda i, j: (j, 0)),
                  pl.BlockSpec((dm, EXPERT_CHUNK), lambda i, j: (0, j)),
                  pl.BlockSpec((PEER_HEADS, rows_per_chunk, tb), lambda i, j: (0, j, i)),
                  pl.BlockSpec((PEER_HEADS, N_KEYS, tb), lambda i, j: (0, 0, i)),
                  pl.BlockSpec((PEER_HEADS, tb), lambda i, j: (0, i)),
                  pl.BlockSpec((PEER_HEADS, tb), lambda i, j: (0, i))],
        out_shape=jax.ShapeDtypeStruct((t, dm), F32),
        out_specs=tok,
        scratch_shapes=[pltpu.VMEM((tb, dm), MXU_DTYPE), pltpu.VMEM((EXPERT_CHUNK, tb), F32),
                        pltpu.VMEM((EXPERT_CHUNK, tb), MXU_DTYPE), pltpu.VMEM((dm, tb), F32)],
        compiler_params=_cparams(("parallel", "arbitrary")),
        name="peer_experts",
    )(xn, res, g, down, up_t, s1, s2, tau, mp)


PEER_TB = 512


def _peer_and_final(xn, res, g_final, peer_w, tb):
    wq_t_hi, wq_t_lo, sk_hi, sk_lo, down, up_t = peer_w
    s1, s2, tau, mp = _route(xn, wq_t_hi, wq_t_lo, sk_hi, sk_lo, tb)
    return _experts(xn, res, g_final, down, up_t, s1, s2, tau, mp, tb)


def _cache_to_feature_major(c):
    return jnp.transpose(c, (0, 2, 3, 1))


def _cache_from_feature_major(c):
    return jnp.transpose(c, (0, 3, 1, 2))


def _layer(xp, xs, cache_a_k, cache_a_v, cache_b_k, cache_b_v, norm_attn, w_in, rel_bias, sinks, w_out, norm_ffn,
           w_peer_q, peer_sub_keys, peer_down, peer_up):
    n, s, dm = xp.shape
    ns = xs.shape[0]
    la, lb = cache_a_k.shape[1], cache_b_k.shape[1]

    c3 = 3 * W_A
    w_q_scaled = jnp.concatenate([w_in[:, :W_A] * SCALE, w_in[:, W_A:c3], w_in[:, c3:c3 + W_BQ] * SCALE,
                                  w_in[:, c3 + W_BQ:]], axis=1)
    qb_cols = w_q_scaled[:, c3:c3 + W_BQ].reshape(dm, H_B // 2, 2, HEAD_DIM)[:, :, ::-1].reshape(dm, W_BQ)
    vb_cols = w_in[:, c3 + W_BQ + W_BKV:].reshape(dm, KV_B, HEAD_DIM)[:, ::-1].reshape(dm, W_BKV)
    w_ext = jnp.concatenate([w_q_scaled, qb_cols, vb_cols], axis=1).astype(MXU_DTYPE)
    w_nat = w_q_scaled.astype(MXU_DTYPE)
    w_out_b = w_out.astype(MXU_DTYPE)
    g_attn, g_ffn = norm_attn[None, :], norm_ffn[None, :]
    wq_t = w_peer_q.T
    wq_t_hi = wq_t.astype(MXU_DTYPE)
    wq_t_lo = (wq_t - wq_t_hi.astype(F32)).astype(MXU_DTYPE)
    sk_hi = peer_sub_keys.astype(MXU_DTYPE)
    sk_lo = (peer_sub_keys - sk_hi.astype(F32)).astype(MXU_DTYPE)
    peer_w = (wq_t_hi, wq_t_lo, sk_hi, sk_lo, peer_down.astype(MXU_DTYPE), peer_up.T.astype(MXU_DTYPE))

    tabs_a = [_bias_table(_window_bucket_matrix(d), rel_bias, 0, H_A) for d in DILATIONS]
    tab_b = _bias_table(_window_bucket_matrix(1), rel_bias, H_A, H_B)
    bidx_sa, mult_sa = _sample_tables(la)
    bidx_sb, _ = _sample_tables(lb)
    tab_sa = _bias_table(bidx_sa, rel_bias, 0, H_A)
    tab_sb = _bias_table(bidx_sb, rel_bias, H_A, H_B)
    bias0_a, bias0_b = rel_bias[0, :H_A], rel_bias[0, H_A:]

    (q1, k1, v1, kf, vf, q4, k4, v4, q16, k16, v16, qb, qbs, kb, vb, vbs, kbf, vbf) = _qkv_prompt(xp, g_attn, w_ext)
    o1, l1 = _swa(tabs_a[0], [q1], k1, [v1], HEAD_CFG_A)
    flat = lambda a: a.reshape((a.shape[0] * a.shape[1],) + a.shape[2:])
    o4, l4 = _swa(tabs_a[1], [flat(q4)], flat(k4), [flat(v4)], HEAD_CFG_A)
    o16, l16 = _swa(tabs_a[2], [flat(q16)], flat(k16), [flat(v16)], HEAD_CFG_A)
    ob = _swa(tab_b, [qb, qbs], kb, [vb, vbs], HEAD_CFG_B, sinks=sinks, want_lse=False)
    unflat = lambda a, d: a.reshape((n, d) + a.shape[1:])
    hp, xnp = _tail_prompt(xp, o1, l1, unflat(o4, 4), unflat(l4, 4), unflat(o16, 16), unflat(l16, 16), ob,
                           w_out_b, g_ffn)
    prompt_caches = (kf[:, s - min(la, s):].reshape(n, -1, H_A, HEAD_DIM),
                     vf[:, s - min(la, s):].reshape(n, -1, H_A, HEAD_DIM),
                     kbf[:, s - min(lb, s):].reshape(n, -1, KV_B, HEAD_DIM),
                     vbf[:, s - min(lb, s):].reshape(n, -1, KV_B, HEAD_DIM))

    hs = _qkv_sample(xs, g_attn, w_nat)
    c4, c5 = c3 + W_BQ, c3 + W_BQ + W_BKV
    col = lambda a: a[:, :, None]
    oa_col, ka_out, va_out = _sample_a(col(hs[:, :W_A]), col(hs[:, W_A:2 * W_A]), col(hs[:, 2 * W_A:c3]),
                                       _cache_to_feature_major(cache_a_k), _cache_to_feature_major(cache_a_v),
                                       tab_sa, jnp.asarray(mult_sa), bias0_a)
    ob_col, kb_out, vb_out = _sample_b(col(hs[:, c3:c4]), col(hs[:, c4:c5]), col(hs[:, c5:]),
                                       _cache_to_feature_major(cache_b_k), _cache_to_feature_major(cache_b_v),
                                       tab_sb, bias0_b, sinks)
    o_s = jnp.concatenate([oa_col[:, :, 0], ob_col[:, :, 0]], axis=-1)
    h_s, xn_s = _tail_sample(xs, o_s, w_out_b, g_ffn)
    sample_caches = tuple(_cache_from_feature_major(c) for c in (ka_out, va_out, kb_out, vb_out))
    return (hp.reshape(n * s, dm), xnp.reshape(n * s, dm), h_s, xn_s, peer_w, prompt_caches, sample_caches)


def kernel(x_prompt, x_sample, cache_a_k, cache_a_v, cache_b_k, cache_b_v, norm_attn, w_in, rel_bias, sinks, w_out,
           norm_ffn, w_peer_q, peer_sub_keys, peer_down, peer_up, norm_final):
    depth = w_in.shape[0]
    assert depth == 1, "single-layer trunk"
    n, s, dm = x_prompt.shape
    ns = x_sample.shape[0]
    assert x_sample.shape[1] == 1 and s % QKV_TB == 0 and (n * s) % PEER_TB == 0 and ns % LANES == 0
    l = 0
    hp, xnp, h_s, xn_s, peer_w, prompt_caches, sample_caches = _layer(
        x_prompt, x_sample[:, 0], cache_a_k[l], cache_a_v[l], cache_b_k[l], cache_b_v[l], norm_attn[l], w_in[l],
        rel_bias, sinks[l], w_out[l], norm_ffn[l], w_peer_q[l], peer_sub_keys[l], peer_down[l], peer_up[l])
    g_final = norm_final[None, :]
    y_prompt = _peer_and_final(xnp, hp, g_final, peer_w, PEER_TB).reshape(n, s, dm)
    y_sample = _peer_and_final(xn_s, h_s, g_final, peer_w, LANES).reshape(ns, 1, dm)
    return (y_prompt, y_sample) + tuple(c[None] for c in prompt_caches) + tuple(c[None] for c in sample_caches)
```

```python
import functools
import math

import numpy as np
import jax
import jax.numpy as jnp
from jax import lax
from jax.experimental import pallas as pl
from jax.experimental.pallas import tpu as pltpu

HEAD_DIM = 64
H_A = 8
H_B = 8
KV_B = 2
G_B = H_B // KV_B
DILATIONS = (1, 4, 16)
SLOTS = 128
N_BUCKETS = 32
MAX_DISTANCE = 2048
N_KEYS = 128
PEER_HEADS = 8
PEER_TOPK = 16
D_KEY = 128
RMS_EPS = 1e-6
SCALE = HEAD_DIM ** -0.5
W_A = H_A * HEAD_DIM
W_BQ = H_B * HEAD_DIM
W_BKV = KV_B * HEAD_DIM

LANES = 128
SUBLANES = 8
VMEM_LIMIT_BYTES = 56 * 1024 * 1024

MXU_DTYPE = jnp.bfloat16
F32 = jnp.float32
NEG_INF = float("-inf")


def _cparams(sem, flags=None):
    return pltpu.CompilerParams(dimension_semantics=sem, vmem_limit_bytes=VMEM_LIMIT_BYTES, flags=flags)


def _t5_bucket_np(dist):
    exact = N_BUCKETS // 2
    d = np.maximum(dist, 0)
    logd = np.log(np.maximum(d, 1).astype(np.float32) / np.float32(exact)) / np.float32(math.log(MAX_DISTANCE / exact))
    large = np.minimum(exact + (logd * np.float32(N_BUCKETS - exact)).astype(np.int32), N_BUCKETS - 1)
    return np.where(d < exact, d, large).astype(np.int32)


def _window_bucket_matrix(dilation):
    i = np.arange(SLOTS)[:, None]
    j = np.arange(2 * SLOTS)[None, :]
    dist = i - j + SLOTS
    ok = (dist >= 0) & (dist <= SLOTS)
    return np.where(ok, _t5_bucket_np(dist * dilation), -1).astype(np.int32)


def _sample_tables(cache_len):
    delta = cache_len - np.arange(cache_len)
    mult = np.zeros(cache_len, np.float32)
    for d in DILATIONS:
        mult += ((delta % d == 0) & (delta <= SLOTS * d)).astype(np.float32)
    return _t5_bucket_np(delta)[None, :], mult[None, :]


def _residue_perm(tb, d):
    p = np.zeros((tb, tb), np.float32)
    rows = np.arange(tb)
    p[rows, (rows % (tb // d)) * d + rows // (tb // d)] = 1.0
    return p


def _rms(x, g):
    return x * lax.rsqrt(jnp.mean(x * x, axis=-1, keepdims=True) + RMS_EPS) * g


def _dot(a, b):
    return jnp.dot(a, b, preferred_element_type=F32)


def _dot_nt(a, b):
    return lax.dot_general(a, b, (((1,), (1,)), ((), ())), preferred_element_type=F32)


def _split2(x):
    hi = x.astype(MXU_DTYPE)
    lo = (x - hi.astype(F32)).astype(MXU_DTYPE)
    return hi, lo


def _split3(x):
    hi = x.astype(MXU_DTYPE)
    r1 = x - hi.astype(F32)
    mid = r1.astype(MXU_DTYPE)
    lo = (r1 - mid.astype(F32)).astype(MXU_DTYPE)
    return hi, mid, lo


def _bias_kernel(col0, nheads, bidx_ref, rb_ref, o_ref):
    b = bidx_ref[...]
    for h in range(nheads):
        val = jnp.full(b.shape, NEG_INF, F32)
        for k in range(N_BUCKETS):
            val = jnp.where(b == k, rb_ref[k, col0 + h], val)
        o_ref[h] = val


def _bias_table(bidx, rel_bias, col0, nheads):
    r, c = bidx.shape
    return pl.pallas_call(
        functools.partial(_bias_kernel, col0, nheads),
        out_shape=jax.ShapeDtypeStruct((nheads, r, c), F32),
        in_specs=[pl.BlockSpec((r, c), lambda: (0, 0)), pl.BlockSpec(memory_space=pltpu.SMEM)],
        out_specs=pl.BlockSpec((nheads, r, c), lambda: (0, 0, 0)),
        name="bias_table",
    )(jnp.asarray(bidx), rel_bias)


QKV_TB = 512


def _qkv_prompt_kernel(x_ref, g_ref, w_ref, p4_ref, p16_ref,
                       q1_ref, k1_ref, v1_ref, kf_ref, vf_ref,
                       q4_ref, k4_ref, v4_ref, q16_ref, k16_ref, v16_ref,
                       qb_ref, qbs_ref, kb_ref, vb_ref, vbs_ref, kbf_ref, vbf_ref):
    xn = _rms(x_ref[0], g_ref[...])
    h = _dot(xn.astype(MXU_DTYPE), w_ref[...])
    c0, c1, c2, c3 = 0, W_A, 2 * W_A, 3 * W_A
    c4, c5, c6 = c3 + W_BQ, c3 + W_BQ + W_BKV, c3 + W_BQ + 2 * W_BKV
    c7 = c6 + W_BQ
    ha = h[:, :c3].astype(MXU_DTYPE)
    q1_ref[0] = ha[:, c0:c1]
    k1_ref[0] = ha[:, c1:c2]
    v1_ref[0] = ha[:, c2:c3]
    kf_ref[0] = h[:, c1:c2]
    vf_ref[0] = h[:, c2:c3]
    qb_ref[0] = h[:, c3:c4].astype(MXU_DTYPE)
    kb_ref[0] = h[:, c4:c5].astype(MXU_DTYPE)
    vb_ref[0] = h[:, c5:c6].astype(MXU_DTYPE)
    kbf_ref[0] = h[:, c4:c5]
    vbf_ref[0] = h[:, c5:c6]
    qbs_ref[0] = h[:, c6:c7].astype(MXU_DTYPE)
    vbs_ref[0] = h[:, c7:].astype(MXU_DTYPE)
    tb = ha.shape[0]
    for d, p_ref, outs in ((4, p4_ref, (q4_ref, k4_ref, v4_ref)), (16, p16_ref, (q16_ref, k16_ref, v16_ref))):
        perm = _dot(p_ref[...], ha).astype(MXU_DTYPE)
        rows = tb // d
        for r in range(d):
            for t, o_ref in enumerate(outs):
                o_ref[0, r] = perm[r * rows:(r + 1) * rows, t * W_A:(t + 1) * W_A]


def _qkv_prompt(x, g, w_ext):
    n, s, dm = x.shape
    tb = QKV_TB
    nb = s // tb
    p4 = jnp.asarray(_residue_perm(tb, 4), MXU_DTYPE)
    p16 = jnp.asarray(_residue_perm(tb, 16), MXU_DTYPE)
    bf = MXU_DTYPE

    def nat(width, dt):
        return jax.ShapeDtypeStruct((n, s, width), dt), pl.BlockSpec((1, tb, width), lambda i, j: (i, j, 0))

    def res(d):
        return (jax.ShapeDtypeStruct((n, d, s // d, W_A), bf),
                pl.BlockSpec((1, d, tb // d, W_A), lambda i, j: (i, 0, j, 0)))

    outs = [nat(W_A, bf), nat(W_A, bf), nat(W_A, bf), nat(W_A, F32), nat(W_A, F32),
            res(4), res(4), res(4), res(16), res(16), res(16),
            nat(W_BQ, bf), nat(W_BQ, bf), nat(W_BKV, bf), nat(W_BKV, bf), nat(W_BKV, bf),
            nat(W_BKV, F32), nat(W_BKV, F32)]
    return pl.pallas_call(
        _qkv_prompt_kernel,
        grid=(n, nb),
        in_specs=[pl.BlockSpec((1, tb, dm), lambda i, j: (i, j, 0)),
                  pl.BlockSpec((1, dm), lambda i, j: (0, 0)),
                  pl.BlockSpec(w_ext.shape, lambda i, j: (0, 0)),
                  pl.BlockSpec((tb, tb), lambda i, j: (0, 0)),
                  pl.BlockSpec((tb, tb), lambda i, j: (0, 0))],
        out_shape=[o[0] for o in outs],
        out_specs=[o[1] for o in outs],
        compiler_params=_cparams(("parallel", "parallel")),
        name="qkv_prompt",
    )(x, g, w_ext, p4, p16)


def _qkv_sample_kernel(x_ref, g_ref, w_ref, o_ref):
    xn = _rms(x_ref[...], g_ref[...])
    o_ref[...] = _dot(xn.astype(MXU_DTYPE), w_ref[...])


def _qkv_sample(x, g, w):
    t = x.shape[0]
    return pl.pallas_call(
        _qkv_sample_kernel,
        out_shape=jax.ShapeDtypeStruct((t, w.shape[1]), F32),
        compiler_params=_cparams(None),
        name="qkv_sample",
    )(x, g, w)


def _swa_kernel(head_cfg, has_sink, want_lse, *refs):
    it = iter(refs)
    tab_ref = next(it)
    sink_ref = next(it) if has_sink else None
    q_refs = [next(it)]
    if any(c[0] == 1 for c in head_cfg):
        q_refs.append(next(it))
    kc_ref, kp_ref = next(it), next(it)
    v_refs = [(next(it), next(it))]
    if any(c[3] == 1 for c in head_cfg):
        v_refs.append((next(it), next(it)))
    o_ref = next(it)
    lse_ref = next(it) if want_lse else None

    first = pl.program_id(1) == 0
    col = lax.broadcasted_iota(jnp.int32, (SLOTS, 2 * SLOTS), 1)
    prev_pen = jnp.where((col < SLOTS) & first, NEG_INF, 0.0)
    lane = lax.broadcasted_iota(jnp.int32, (SLOTS, LANES), 1)
    low = lane < HEAD_DIM
    lse_acc = jnp.zeros((SLOTS, LANES), F32)
    for p in range(len(head_cfg) // 2):
        halves = []
        for hh in range(2):
            h = 2 * p + hh
            q_src, q_half, k_tile, v_src, v_tile = head_cfg[h]
            q = q_refs[q_src][0, :, p * LANES:(p + 1) * LANES]
            q = jnp.where(low if q_half == 0 else jnp.logical_not(low), q, jnp.zeros_like(q))
            ks = slice(k_tile * LANES, (k_tile + 1) * LANES)
            kcat = jnp.concatenate([kp_ref[0, :, ks], kc_ref[0, :, ks]], axis=0)
            s = _dot_nt(q, kcat) + tab_ref[h] + prev_pen
            m = jnp.max(s, axis=-1, keepdims=True)
            if has_sink:
                m = jnp.maximum(m, sink_ref[h])
            e = jnp.exp(s - m)
            l = jnp.sum(e, axis=-1, keepdims=True)
            if has_sink:
                l = l + jnp.exp(sink_ref[h] - m)
            vc_ref, vp_ref = v_refs[v_src]
            vs = slice(v_tile * LANES, (v_tile + 1) * LANES)
            vcat = jnp.concatenate([vp_ref[0, :, vs], vc_ref[0, :, vs]], axis=0)
            halves.append(_dot(e.astype(MXU_DTYPE), vcat) / l)
            if want_lse:
                lse_acc = jnp.where(lane == h, m + jnp.log(l), lse_acc)
        o_ref[0, :, p * LANES:(p + 1) * LANES] = jnp.where(low, halves[0], halves[1]).astype(o_ref.dtype)
    if want_lse:
        lse_ref[0] = lse_acc


def _swa(tab, q_list, k, v_list, head_cfg, sinks=None, want_lse=True):
    r, s, _ = q_list[0].shape
    ck = k.shape[-1]
    nb = s // SLOTS
    cur = lambda i, j: (i, j, 0)
    prev = lambda i, j: (i, jnp.maximum(j - 1, 0), 0)
    in_specs = [pl.BlockSpec(tab.shape, lambda i, j: (0, 0, 0))]
    args = [tab]
    if sinks is not None:
        in_specs.append(pl.BlockSpec(memory_space=pltpu.SMEM))
        args.append(sinks)
    for q in q_list:
        in_specs.append(pl.BlockSpec((1, SLOTS, q.shape[-1]), cur))
        args.append(q)
    in_specs += [pl.BlockSpec((1, SLOTS, ck), cur), pl.BlockSpec((1, SLOTS, ck), prev)]
    args += [k, k]
    for v in v_list:
        in_specs += [pl.BlockSpec((1, SLOTS, ck), cur), pl.BlockSpec((1, SLOTS, ck), prev)]
        args += [v, v]
    out_shape = [jax.ShapeDtypeStruct((r, s, W_A), MXU_DTYPE)]
    out_specs = [pl.BlockSpec((1, SLOTS, W_A), cur)]
    if want_lse:
        out_shape.append(jax.ShapeDtypeStruct((r, s, LANES), F32))
        out_specs.append(pl.BlockSpec((1, SLOTS, LANES), cur))
    out = pl.pallas_call(
        functools.partial(_swa_kernel, tuple(head_cfg), sinks is not None, want_lse),
        grid=(r, nb),
        in_specs=in_specs,
        out_shape=out_shape,
        out_specs=out_specs,
        compiler_params=_cparams(("parallel", "arbitrary")),
        name="swa",
    )(*args)
    return out if want_lse else out[0]


HEAD_CFG_A = tuple((0, h % 2, h // 2, 0, h // 2) for h in range(H_A))


def _head_cfg_b():
    cfg = []
    for h in range(H_B):
        c = h // G_B
        src = 0 if h % 2 == c else 1
        cfg.append((src, c, 0, src, 0))
    return tuple(cfg)


HEAD_CFG_B = _head_cfg_b()


def _shift_in(x, new_col):
    length = x.shape[-1]
    rolled = pltpu.roll(x, length - 1, axis=1)
    lane = lax.broadcasted_iota(jnp.int32, x.shape, 1)
    return jnp.where(lane == length - 1, new_col, rolled)


def _col_attention(q, kmat, vmat, k_new, v_new, bias_row, mult_row, bias_new, mult_new, sink):
    s = jnp.sum(q * kmat, axis=0, keepdims=True) + bias_row
    s_new = jnp.sum(q * k_new, axis=0, keepdims=True) + bias_new
    if mult_row is not None:
        s = jnp.where(mult_row > 0.0, s, NEG_INF)
    m = jnp.maximum(jnp.max(s, axis=-1, keepdims=True), s_new)
    if sink is not None:
        m = jnp.maximum(m, sink)
    e = jnp.exp(s - m)
    if mult_row is not None:
        e = e * mult_row
    e_new = mult_new * jnp.exp(s_new - m)
    l = jnp.sum(e, axis=-1, keepdims=True) + e_new
    if sink is not None:
        l = l + jnp.exp(sink - m)
    o = jnp.sum(vmat * e, axis=-1, keepdims=True) + v_new * e_new
    return o / l


def _sample_a_kernel(heads_per_step, bias0_ref, tab_ref, mult_ref, q_ref, kn_ref, vn_ref, k_ref, v_ref,
                     o_ref, ko_ref, vo_ref):
    hb = pl.program_id(1)
    mult = mult_ref[...]
    for hl in range(heads_per_step):
        rows = slice(hl * HEAD_DIM, (hl + 1) * HEAD_DIM)
        h = hb * heads_per_step + hl
        kmat, vmat = k_ref[0, hl], v_ref[0, hl]
        k_new, v_new = kn_ref[0, rows], vn_ref[0, rows]
        o_ref[0, rows] = _col_attention(q_ref[0, rows], kmat, vmat, k_new, v_new, tab_ref[hl], mult,
                                        bias0_ref[h], float(len(DILATIONS)), None)
        ko_ref[0, hl] = _shift_in(kmat, k_new)
        vo_ref[0, hl] = _shift_in(vmat, v_new)


def _sample_a(h_col, col_q, col_k, col_v, k_t, v_t, tab, mult, bias0):
    ns, nh, hd, length = k_t.shape
    hps = 4
    rows = hps * hd
    assert col_q % rows == 0 and col_k % rows == 0 and col_v % rows == 0
    col_at = lambda col: pl.BlockSpec((1, rows, 1), lambda i, j: (i, col // rows + j, 0))
    cache_spec = pl.BlockSpec((1, hps, hd, length), lambda i, j: (i, j, 0, 0))
    return pl.pallas_call(
        functools.partial(_sample_a_kernel, hps),
        grid=(ns, nh // hps),
        in_specs=[pl.BlockSpec(memory_space=pltpu.SMEM),
                  pl.BlockSpec((hps, 1, length), lambda i, j: (j, 0, 0)),
                  pl.BlockSpec((1, length), lambda i, j: (0, 0)),
                  col_at(col_q), col_at(col_k), col_at(col_v), cache_spec, cache_spec],
        out_shape=[jax.ShapeDtypeStruct((ns, nh * hd, 1), F32),
                   jax.ShapeDtypeStruct(k_t.shape, F32), jax.ShapeDtypeStruct(v_t.shape, F32)],
        out_specs=[col_at(0), cache_spec, cache_spec],
        compiler_params=_cparams(("parallel", "arbitrary")),
        name="sample_attn_a",
    )(bias0, tab, mult, h_col, h_col, h_col, k_t, v_t)


SAMPLE_B_BATCH = 8


def _sample_b_kernel(bias0_ref, sink_ref, tab_ref, q_ref, kn_ref, vn_ref, k_ref, v_ref, o_ref, ko_ref, vo_ref):
    for s in range(q_ref.shape[0]):
        for h in range(H_B):
            c = h // G_B
            rows = slice(h * HEAD_DIM, (h + 1) * HEAD_DIM)
            crow = slice(c * HEAD_DIM, (c + 1) * HEAD_DIM)
            o_ref[s, rows] = _col_attention(q_ref[s, rows], k_ref[s, c], v_ref[s, c], kn_ref[s, crow],
                                            vn_ref[s, crow], tab_ref[h], None, bias0_ref[h], 1.0, sink_ref[h])
        for c in range(KV_B):
            crow = slice(c * HEAD_DIM, (c + 1) * HEAD_DIM)
            ko_ref[s, c] = _shift_in(k_ref[s, c], kn_ref[s, crow])
            vo_ref[s, c] = _shift_in(v_ref[s, c], vn_ref[s, crow])


def _sample_b(h_col, col_q, col_k, col_v, k_t, v_t, tab, bias0, sinks):
    ns, nkv, hd, length = k_t.shape
    sb = SAMPLE_B_BATCH
    qrows, krows = H_B * hd, nkv * hd
    assert ns % sb == 0 and col_q % qrows == 0 and col_k % krows == 0 and col_v % krows == 0
    qspec = pl.BlockSpec((sb, qrows, 1), lambda i: (i, col_q // qrows, 0))
    kspec = pl.BlockSpec((sb, krows, 1), lambda i: (i, col_k // krows, 0))
    vspec = pl.BlockSpec((sb, krows, 1), lambda i: (i, col_v // krows, 0))
    ospec = pl.BlockSpec((sb, qrows, 1), lambda i: (i, 0, 0))
    cspec = pl.BlockSpec((sb, nkv, hd, length), lambda i: (i, 0, 0, 0))
    return pl.pallas_call(
        _sample_b_kernel,
        grid=(ns // sb,),
        in_specs=[pl.BlockSpec(memory_space=pltpu.SMEM), pl.BlockSpec(memory_space=pltpu.SMEM),
                  pl.BlockSpec(tab.shape, lambda i: (0, 0, 0)), qspec, kspec, vspec, cspec, cspec],
        out_shape=[jax.ShapeDtypeStruct((ns, H_B * hd, 1), F32),
                   jax.ShapeDtypeStruct(k_t.shape, F32), jax.ShapeDtypeStruct(v_t.shape, F32)],
        out_specs=[ospec, cspec, cspec],
        compiler_params=_cparams(("parallel",)),
        name="sample_attn_b",
    )(bias0, sinks, tab, h_col, h_col, h_col, k_t, v_t)


TAIL_TB = 512


def _unpermute(pt_ref, blocks_ref, exact):
    d = blocks_ref.shape[1]
    x = jnp.concatenate([blocks_ref[0, r] for r in range(d)], axis=0)
    if not exact:
        return _dot(pt_ref[...], x)
    return sum(_dot(pt_ref[...], part) for part in _split3(x))


def _tail_prompt_kernel(x_ref, o1_ref, l1_ref, o4_ref, l4_ref, o16_ref, l16_ref, ob_ref, pt4_ref, pt16_ref,
                        w_ref, g_ref, h_ref, xn_ref):
    o_g = [o1_ref[0].astype(F32), _unpermute(pt4_ref, o4_ref, False), _unpermute(pt16_ref, o16_ref, False)]
    l_g = [l1_ref[0], _unpermute(pt4_ref, l4_ref, True), _unpermute(pt16_ref, l16_ref, True)]
    m = jnp.maximum(jnp.maximum(l_g[0], l_g[1]), l_g[2])
    e_g = [jnp.exp(l - m) for l in l_g]
    den = e_g[0] + e_g[1] + e_g[2]
    w_g = [e / den for e in e_g]
    lane = lax.broadcasted_iota(jnp.int32, (x_ref.shape[1], LANES), 1)
    low = lane < HEAD_DIM
    parts = []
    for p in range(H_A // 2):
        acc = None
        for w, o in zip(w_g, o_g):
            wp = jnp.where(low, w[:, 2 * p:2 * p + 1], w[:, 2 * p + 1:2 * p + 2])
            term = wp * o[:, p * LANES:(p + 1) * LANES]
            acc = term if acc is None else acc + term
        parts.append(acc.astype(MXU_DTYPE))
    o = jnp.concatenate(parts + [ob_ref[0]], axis=-1)
    h = x_ref[0] + _dot(o, w_ref[...])
    h_ref[0] = h
    xn_ref[0] = _rms(h, g_ref[...])


def _tail_prompt(x, o1, l1, o4, l4, o16, l16, ob, w_out, g):
    n, s, dm = x.shape
    tb = TAIL_TB
    pt4 = jnp.asarray(_residue_perm(tb, 4).T, MXU_DTYPE)
    pt16 = jnp.asarray(_residue_perm(tb, 16).T, MXU_DTYPE)
    nat = lambda w: pl.BlockSpec((1, tb, w), lambda i, j: (i, j, 0))
    res = lambda d, w: pl.BlockSpec((1, d, tb // d, w), lambda i, j: (i, 0, j, 0))
    const = lambda a: pl.BlockSpec(a.shape, lambda i, j: (0,) * a.ndim)
    return pl.pallas_call(
        _tail_prompt_kernel,
        grid=(n, s // tb),
        in_specs=[nat(dm), nat(W_A), nat(LANES), res(4, W_A), res(4, LANES), res(16, W_A), res(16, LANES),
                  nat(W_BQ), const(pt4), const(pt16), const(w_out), const(g)],
        out_shape=[jax.ShapeDtypeStruct((n, s, dm), F32), jax.ShapeDtypeStruct((n, s, dm), F32)],
        out_specs=[nat(dm), nat(dm)],
        compiler_params=_cparams(("parallel", "parallel")),
        name="tail_prompt",
    )(x, o1, l1, o4, l4, o16, l16, ob, pt4, pt16, w_out, g)


def _tail_sample_kernel(x_ref, o_ref, w_ref, g_ref, h_ref, xn_ref):
    h = x_ref[...] + _dot(o_ref[...].astype(MXU_DTYPE), w_ref[...])
    h_ref[...] = h
    xn_ref[...] = _rms(h, g_ref[...])


def _tail_sample(x, o, w_out, g):
    return pl.pallas_call(
        _tail_sample_kernel,
        out_shape=[jax.ShapeDtypeStruct(x.shape, F32), jax.ShapeDtypeStruct(x.shape, F32)],
        compiler_params=_cparams(None),
        name="tail_sample",
    )(x, o, w_out, g)


def _top_rows(s, count):
    rows = []
    for _ in range(count):
        m = jnp.max(s, axis=0, keepdims=True)
        rows.append(m)
        s = jnp.where(s == m, NEG_INF, s)
    return rows


def _pad_rows(rows, count):
    pad = [jnp.full_like(rows[0], NEG_INF)] * (count - len(rows))
    return jnp.concatenate(list(rows) + pad, axis=0)


def _candidate_sums(top1, top2):
    k = PEER_TOPK + 1
    wide = -(-k // SUBLANES) * SUBLANES
    narrow = -(-(k // 2) // SUBLANES) * SUBLANES
    assert k // (narrow + 1) <= 1
    v2_wide = _pad_rows(top2, wide)
    v2_narrow = v2_wide[:narrow]
    parts = [top1[0] + v2_wide] + [top1[a] + v2_narrow for a in range(1, narrow)]
    parts.append(_pad_rows(top1[narrow:], -(-(k - narrow) // SUBLANES) * SUBLANES) + top2[0])
    return jnp.concatenate(parts, axis=0)


def _route_kernel(x_ref, wqh_ref, wql_ref, skh_ref, skl_ref, thr_ref, p1_ref, p2_ref):
    xh, xl = _split2(x_ref[...])
    wqh = wqh_ref[...]
    q_t = _dot_nt(wqh, xh) + _dot_nt(wqh, xl) + _dot_nt(wql_ref[...], xh)
    half = D_KEY // 2
    for h in range(PEER_HEADS):
        scores, tops = [], []
        for c in range(2):
            qh, ql = _split2(q_t[h * D_KEY + c * half:h * D_KEY + (c + 1) * half, :])
            s = _dot(skh_ref[c], qh) + _dot(skh_ref[c], ql) + _dot(skl_ref[c], qh)
            scores.append(s)
            tops.append(_top_rows(s, PEER_TOPK + 1))
        best = _top_rows(_candidate_sums(tops[0], tops[1]), PEER_TOPK + 1)
        tau = 0.5 * (best[PEER_TOPK - 1] + best[PEER_TOPK])
        z = sum(jnp.exp(b - best[0]) for b in best[:PEER_TOPK])
        log_norm = best[0] + jnp.log(z)
        m2 = tops[1][0]
        thr = jnp.exp((tau - m2) - scores[0])
        p1 = jnp.exp(scores[0] + (m2 - log_norm))
        p2 = jnp.exp(scores[1] - m2)
        for g in range(thr.shape[1] // LANES):
            for o_ref, val in ((thr_ref, thr), (p1_ref, p1), (p2_ref, p2)):
                o_ref[h, g] = val[:, g * LANES:(g + 1) * LANES]


def _route(xn, wq_t_hi, wq_t_lo, sk_hi, sk_lo, tb):
    t, dm = xn.shape
    const = lambda a: pl.BlockSpec(a.shape, lambda i: (0,) * a.ndim)
    gspec = pl.BlockSpec((PEER_HEADS, tb // LANES, N_KEYS, LANES), lambda i: (0, i, 0, 0))
    gshape = jax.ShapeDtypeStruct((PEER_HEADS, t // LANES, N_KEYS, LANES), F32)
    return pl.pallas_call(
        _route_kernel,
        grid=(t // tb,),
        in_specs=[pl.BlockSpec((tb, dm), lambda i: (i, 0)), const(wq_t_hi), const(wq_t_lo), const(sk_hi), const(sk_lo)],
        out_shape=[gshape] * 3,
        out_specs=[gspec] * 3,
        compiler_params=_cparams(("parallel",)),
        name="peer_route",
    )(xn, wq_t_hi, wq_t_lo, sk_hi, sk_lo)


EXPERT_CHUNK = SUBLANES * N_KEYS
GATE_JBLOCK = 4 * SUBLANES
INV_SQRT2 = 0.7071067811865476


def _gate_and_activate(g, a_ref, p_ref, thr_ref, p1_ref, p2_ref):
    nv = GATE_JBLOCK // SUBLANES
    for jb in range(N_KEYS // GATE_JBLOCK):
        w = [[None] * nv for _ in range(SUBLANES)]
        for h in range(PEER_HEADS):
            p2 = [p2_ref[h, g, pl.ds(jb * GATE_JBLOCK + v * SUBLANES, SUBLANES), :] for v in range(nv)]
            for r in range(SUBLANES):
                thr = jnp.broadcast_to(thr_ref[h, g, pl.ds(r, 1), :], (SUBLANES, LANES))
                p1 = jnp.broadcast_to(p1_ref[h, g, pl.ds(r, 1), :], (SUBLANES, LANES))
                for v in range(nv):
                    term = jnp.where(p2[v] >= thr, p1 * p2[v], 0.0)
                    w[r][v] = term if w[r][v] is None else w[r][v] + term
        for r in range(SUBLANES):
            row0 = r * N_KEYS + jb * GATE_JBLOCK
            a = a_ref[g, pl.ds(row0, GATE_JBLOCK), :]
            half = 0.5 * a
            act = half + half * lax.erf(a * INV_SQRT2)
            p_ref[g, pl.ds(row0, GATE_JBLOCK), :] = (jnp.concatenate(w[r], axis=0) * act).astype(p_ref.dtype)


def _experts_kernel(xn_ref, res_ref, g_ref, down_ref, upt_ref, thr_a_ref, p1_a_ref, thr_b_ref, p1_b_ref,
                    p2_ref, y_ref, xb_ref, a0_ref, a1_ref, pb0_ref, pb1_ref, acc_ref):
    i, j = pl.program_id(0), pl.program_id(1)
    ch = EXPERT_CHUNK
    ng, dm = a0_ref.shape[0], acc_ref.shape[0]
    slab_a, slab_u = ch // ng, dm // ng

    @pl.when((i == 0) & (j == 0))
    def _():
        for ref in (a0_ref, a1_ref, pb0_ref, pb1_ref):
            ref[...] = jnp.zeros_like(ref)

    @pl.when(j == 0)
    def _():
        xb_ref[...] = xn_ref[...].astype(MXU_DTYPE)

    keep = j >= 1

    def tick(half, a_in_ref, a_out_ref, p_out_ref, p_in_ref, thr_ref, p1_ref):
        for g in range(ng):
            _gate_and_activate(g, a_in_ref, p_out_ref, thr_ref, p1_ref, p2_ref)
            ra = g * slab_a
            nxt = _dot_nt(down_ref[pl.ds(half * ch + ra, slab_a), :], xb_ref[...])
            for gg in range(ng):
                a_out_ref[gg, pl.ds(ra, slab_a), :] = nxt[:, gg * LANES:(gg + 1) * LANES]
            ru = g * slab_u
            p_in = jnp.concatenate([p_in_ref[gg] for gg in range(ng)], axis=1)
            upd = _dot(upt_ref[pl.ds(ru, slab_u), half * ch:(half + 1) * ch], p_in)
            acc_ref[pl.ds(ru, slab_u), :] = jnp.where(keep, acc_ref[pl.ds(ru, slab_u), :] + upd, 0.0)

    tick(0, a1_ref, a0_ref, pb1_ref, pb0_ref, thr_a_ref, p1_a_ref)
    tick(1, a0_ref, a1_ref, pb0_ref, pb1_ref, thr_b_ref, p1_b_ref)

    @pl.when(j == pl.num_programs(1) - 1)
    def _():
        y = res_ref[...] + acc_ref[...].T
        y_ref[...] = _rms(y, g_ref[...])


def _experts(xn, res, g, down, up_t, thr, p1, p2, tb):
    t, dm = xn.shape
    ne = down.shape[0]
    step = 2 * EXPERT_CHUNK
    nj = ne // step
    ng = tb // LANES
    tok = pl.BlockSpec((tb, dm), lambda i, j: (i, 0))
    rshape = (PEER_HEADS, ng, SUBLANES, LANES)
    row_a = pl.BlockSpec(rshape, lambda i, j: (0, i, jnp.maximum(2 * j - 1, 0), 0))
    row_b = pl.BlockSpec(rshape, lambda i, j: (0, i, jnp.minimum(2 * j, 2 * nj - 1), 0))
    full = pl.BlockSpec((PEER_HEADS, ng, N_KEYS, LANES), lambda i, j: (0, i, 0, 0))
    return pl.pallas_call(
        _experts_kernel,
        grid=(t // tb, nj + 1),
        in_specs=[tok, tok, pl.BlockSpec((1, dm), lambda i, j: (0, 0)),
                  pl.BlockSpec((step, dm), lambda i, j: (jnp.minimum(j, nj - 1), 0)),
                  pl.BlockSpec((dm, step), lambda i, j: (0, jnp.maximum(j - 1, 0))),
                  row_a, row_a, row_b, row_b, full],
        out_shape=jax.ShapeDtypeStruct((t, dm), F32),
        out_specs=tok,
        scratch_shapes=[pltpu.VMEM((tb, dm), MXU_DTYPE),
                        pltpu.VMEM((ng, EXPERT_CHUNK, LANES), F32), pltpu.VMEM((ng, EXPERT_CHUNK, LANES), F32),
                        pltpu.VMEM((ng, EXPERT_CHUNK, LANES), MXU_DTYPE),
                        pltpu.VMEM((ng, EXPERT_CHUNK, LANES), MXU_DTYPE),
                        pltpu.VMEM((dm, tb), F32)],
        compiler_params=_cparams(("arbitrary", "arbitrary")),
        name="peer_experts",
    )(xn, res, g, down, up_t, thr, p1, thr, p1, p2)


PEER_TB = 512


def _peer_and_final(xn, res, g_final, peer_w, tb):
    wq_t_hi, wq_t_lo, sk_hi, sk_lo, down, up_t = peer_w
    thr, p1, p2 = _route(xn, wq_t_hi, wq_t_lo, sk_hi, sk_lo, tb)
    return _experts(xn, res, g_final, down, up_t, thr, p1, p2, tb)


def _cache_to_feature_major(c):
    return jnp.transpose(c, (0, 2, 3, 1))


def _cache_from_feature_major(c):
    return jnp.transpose(c, (0, 3, 1, 2))


def _layer(xp, xs, cache_a_k, cache_a_v, cache_b_k, cache_b_v, norm_attn, w_in, rel_bias, sinks, w_out, norm_ffn,
           w_peer_q, peer_sub_keys, peer_down, peer_up):
    n, s, dm = xp.shape
    ns = xs.shape[0]
    la, lb = cache_a_k.shape[1], cache_b_k.shape[1]

    c3 = 3 * W_A
    w_q_scaled = jnp.concatenate([w_in[:, :W_A] * SCALE, w_in[:, W_A:c3], w_in[:, c3:c3 + W_BQ] * SCALE,
                                  w_in[:, c3 + W_BQ:]], axis=1)
    qb_cols = w_q_scaled[:, c3:c3 + W_BQ].reshape(dm, H_B // 2, 2, HEAD_DIM)[:, :, ::-1].reshape(dm, W_BQ)
    vb_cols = w_in[:, c3 + W_BQ + W_BKV:].reshape(dm, KV_B, HEAD_DIM)[:, ::-1].reshape(dm, W_BKV)
    w_ext = jnp.concatenate([w_q_scaled, qb_cols, vb_cols], axis=1).astype(MXU_DTYPE)
    w_nat = w_q_scaled.astype(MXU_DTYPE)
    w_out_b = w_out.astype(MXU_DTYPE)
    g_attn, g_ffn = norm_attn[None, :], norm_ffn[None, :]
    wq_t = w_peer_q.T
    wq_t_hi = wq_t.astype(MXU_DTYPE)
    wq_t_lo = (wq_t - wq_t_hi.astype(F32)).astype(MXU_DTYPE)
    sk_hi = peer_sub_keys.astype(MXU_DTYPE)
    sk_lo = (peer_sub_keys - sk_hi.astype(F32)).astype(MXU_DTYPE)
    peer_w = (wq_t_hi, wq_t_lo, sk_hi, sk_lo, peer_down.astype(MXU_DTYPE), peer_up.T.astype(MXU_DTYPE))

    tabs_a = [_bias_table(_window_bucket_matrix(d), rel_bias, 0, H_A) for d in DILATIONS]
    tab_b = _bias_table(_window_bucket_matrix(1), rel_bias, H_A, H_B)
    bidx_sa, mult_sa = _sample_tables(la)
    bidx_sb, _ = _sample_tables(lb)
    tab_sa = _bias_table(bidx_sa, rel_bias, 0, H_A)
    tab_sb = _bias_table(bidx_sb, rel_bias, H_A, H_B)
    bias0_a, bias0_b = rel_bias[0, :H_A], rel_bias[0, H_A:]

    (q1, k1, v1, kf, vf, q4, k4, v4, q16, k16, v16, qb, qbs, kb, vb, vbs, kbf, vbf) = _qkv_prompt(xp, g_attn, w_ext)
    o1, l1 = _swa(tabs_a[0], [q1], k1, [v1], HEAD_CFG_A)
    flat = lambda a: a.reshape((a.shape[0] * a.shape[1],) + a.shape[2:])
    o4, l4 = _swa(tabs_a[1], [flat(q4)], flat(k4), [flat(v4)], HEAD_CFG_A)
    o16, l16 = _swa(tabs_a[2], [flat(q16)], flat(k16), [flat(v16)], HEAD_CFG_A)
    ob = _swa(tab_b, [qb, qbs], kb, [vb, vbs], HEAD_CFG_B, sinks=sinks, want_lse=False)
    unflat = lambda a, d: a.reshape((n, d) + a.shape[1:])
    hp, xnp = _tail_prompt(xp, o1, l1, unflat(o4, 4), unflat(l4, 4), unflat(o16, 16), unflat(l16, 16), ob,
                           w_out_b, g_ffn)
    prompt_caches = (kf[:, s - min(la, s):].reshape(n, -1, H_A, HEAD_DIM),
                     vf[:, s - min(la, s):].reshape(n, -1, H_A, HEAD_DIM),
                     kbf[:, s - min(lb, s):].reshape(n, -1, KV_B, HEAD_DIM),
                     vbf[:, s - min(lb, s):].reshape(n, -1, KV_B, HEAD_DIM))

    hs = _qkv_sample(xs, g_attn, w_nat)
    c4, c5 = c3 + W_BQ, c3 + W_BQ + W_BKV
    hs_col = hs[:, :, None]
    oa_col, ka_out, va_out = _sample_a(hs_col, 0, W_A, 2 * W_A,
                                       _cache_to_feature_major(cache_a_k), _cache_to_feature_major(cache_a_v),
                                       tab_sa, jnp.asarray(mult_sa), bias0_a)
    ob_col, kb_out, vb_out = _sample_b(hs_col, c3, c4, c5,
                                       _cache_to_feature_major(cache_b_k), _cache_to_feature_major(cache_b_v),
                                       tab_sb, bias0_b, sinks)
    o_s = jnp.concatenate([oa_col[:, :, 0], ob_col[:, :, 0]], axis=-1)
    h_s, xn_s = _tail_sample(xs, o_s, w_out_b, g_ffn)
    sample_caches = tuple(_cache_from_feature_major(c) for c in (ka_out, va_out, kb_out, vb_out))
    return (hp.reshape(n * s, dm), xnp.reshape(n * s, dm), h_s, xn_s, peer_w, prompt_caches, sample_caches)


def kernel(x_prompt, x_sample, cache_a_k, cache_a_v, cache_b_k, cache_b_v, norm_attn, w_in, rel_bias, sinks, w_out,
           norm_ffn, w_peer_q, peer_sub_keys, peer_down, peer_up, norm_final):
    depth = w_in.shape[0]
    assert depth == 1, "single-layer trunk"
    n, s, dm = x_prompt.shape
    ns = x_sample.shape[0]
    assert x_sample.shape[1] == 1 and s % QKV_TB == 0 and (n * s) % PEER_TB == 0 and ns % LANES == 0
    l = 0
    hp, xnp, h_s, xn_s, peer_w, prompt_caches, sample_caches = _layer(
        x_prompt, x_sample[:, 0], cache_a_k[l], cache_a_v[l], cache_b_k[l], cache_b_v[l], norm_attn[l], w_in[l],
        rel_bias, sinks[l], w_out[l], norm_ffn[l], w_peer_q[l], peer_sub_keys[l], peer_down[l], peer_up[l])
    g_final = norm_final[None, :]
    y_prompt = _peer_and_final(xnp, hp, g_final, peer_w, PEER_TB).reshape(n, s, dm)
    y_sample = _peer_and_final(xn_s, h_s, g_final, peer_w, LANES).reshape(ns, 1, dm)
    return (y_prompt, y_sample) + tuple(c[None] for c in prompt_caches) + tuple(c[None] for c in sample_caches)
```

```python
import functools
import math

import numpy as np
import jax
import jax.numpy as jnp
from jax import lax
from jax.experimental import pallas as pl
from jax.experimental.pallas import tpu as pltpu

HEAD_DIM = 64
H_A = 8
H_B = 8
KV_B = 2
G_B = H_B // KV_B
DILATIONS = (1, 4, 16)
SLOTS = 128
N_BUCKETS = 32
MAX_DISTANCE = 2048
N_KEYS = 128
PEER_HEADS = 8
PEER_TOPK = 16
D_KEY = 128
RMS_EPS = 1e-6
SCALE = HEAD_DIM ** -0.5
W_A = H_A * HEAD_DIM
W_BQ = H_B * HEAD_DIM
W_BKV = KV_B * HEAD_DIM

LANES = 128
SUBLANES = 8
VMEM_LIMIT_BYTES = 56 * 1024 * 1024

MXU_DTYPE = jnp.bfloat16
F32 = jnp.float32
NEG_INF = float("-inf")


def _cparams(sem, flags=None):
    return pltpu.CompilerParams(dimension_semantics=sem, vmem_limit_bytes=VMEM_LIMIT_BYTES, flags=flags)


def _t5_bucket_np(dist):
    exact = N_BUCKETS // 2
    d = np.maximum(dist, 0)
    logd = np.log(np.maximum(d, 1).astype(np.float32) / np.float32(exact)) / np.float32(math.log(MAX_DISTANCE / exact))
    large = np.minimum(exact + (logd * np.float32(N_BUCKETS - exact)).astype(np.int32), N_BUCKETS - 1)
    return np.where(d < exact, d, large).astype(np.int32)


def _window_bucket_matrix(dilation):
    i = np.arange(SLOTS)[:, None]
    j = np.arange(2 * SLOTS)[None, :]
    dist = i - j + SLOTS
    ok = (dist >= 0) & (dist <= SLOTS)
    return np.where(ok, _t5_bucket_np(dist * dilation), -1).astype(np.int32)


def _sample_tables(cache_len):
    delta = cache_len - np.arange(cache_len)
    mult = np.zeros(cache_len, np.float32)
    for d in DILATIONS:
        mult += ((delta % d == 0) & (delta <= SLOTS * d)).astype(np.float32)
    return _t5_bucket_np(delta)[None, :], mult[None, :]


def _residue_perm(tb, d):
    p = np.zeros((tb, tb), np.float32)
    rows = np.arange(tb)
    p[rows, (rows % (tb // d)) * d + rows // (tb // d)] = 1.0
    return p


def _rms(x, g):
    return x * lax.rsqrt(jnp.mean(x * x, axis=-1, keepdims=True) + RMS_EPS) * g


def _dot(a, b):
    return jnp.dot(a, b, preferred_element_type=F32)


def _dot_nt(a, b):
    return lax.dot_general(a, b, (((1,), (1,)), ((), ())), preferred_element_type=F32)


def _split2(x):
    hi = x.astype(MXU_DTYPE)
    lo = (x - hi.astype(F32)).astype(MXU_DTYPE)
    return hi, lo


def _split3(x):
    hi = x.astype(MXU_DTYPE)
    r1 = x - hi.astype(F32)
    mid = r1.astype(MXU_DTYPE)
    lo = (r1 - mid.astype(F32)).astype(MXU_DTYPE)
    return hi, mid, lo


def _bias_kernel(col0, nheads, bidx_ref, rb_ref, o_ref):
    b = bidx_ref[...]
    for h in range(nheads):
        val = jnp.full(b.shape, NEG_INF, F32)
        for k in range(N_BUCKETS):
            val = jnp.where(b == k, rb_ref[k, col0 + h], val)
        o_ref[h] = val


def _bias_table(bidx, rel_bias, col0, nheads):
    r, c = bidx.shape
    return pl.pallas_call(
        functools.partial(_bias_kernel, col0, nheads),
        out_shape=jax.ShapeDtypeStruct((nheads, r, c), F32),
        in_specs=[pl.BlockSpec((r, c), lambda: (0, 0)), pl.BlockSpec(memory_space=pltpu.SMEM)],
        out_specs=pl.BlockSpec((nheads, r, c), lambda: (0, 0, 0)),
        name="bias_table",
    )(jnp.asarray(bidx), rel_bias)


QKV_TB = 512


def _qkv_prompt_kernel(x_ref, g_ref, w_ref, p4_ref, p16_ref,
                       q1_ref, k1_ref, v1_ref, kf_ref, vf_ref,
                       q4_ref, k4_ref, v4_ref, q16_ref, k16_ref, v16_ref,
                       qb_ref, qbs_ref, kb_ref, vb_ref, vbs_ref, kbf_ref, vbf_ref):
    xn = _rms(x_ref[0], g_ref[...])
    h = _dot(xn.astype(MXU_DTYPE), w_ref[...])
    c0, c1, c2, c3 = 0, W_A, 2 * W_A, 3 * W_A
    c4, c5, c6 = c3 + W_BQ, c3 + W_BQ + W_BKV, c3 + W_BQ + 2 * W_BKV
    c7 = c6 + W_BQ
    ha = h[:, :c3].astype(MXU_DTYPE)
    q1_ref[0] = ha[:, c0:c1]
    k1_ref[0] = ha[:, c1:c2]
    v1_ref[0] = ha[:, c2:c3]
    kf_ref[0] = h[:, c1:c2]
    vf_ref[0] = h[:, c2:c3]
    qb_ref[0] = h[:, c3:c4].astype(MXU_DTYPE)
    kb_ref[0] = h[:, c4:c5].astype(MXU_DTYPE)
    vb_ref[0] = h[:, c5:c6].astype(MXU_DTYPE)
    kbf_ref[0] = h[:, c4:c5]
    vbf_ref[0] = h[:, c5:c6]
    qbs_ref[0] = h[:, c6:c7].astype(MXU_DTYPE)
    vbs_ref[0] = h[:, c7:].astype(MXU_DTYPE)
    tb = ha.shape[0]
    for d, p_ref, outs in ((4, p4_ref, (q4_ref, k4_ref, v4_ref)), (16, p16_ref, (q16_ref, k16_ref, v16_ref))):
        perm = _dot(p_ref[...], ha).astype(MXU_DTYPE)
        rows = tb // d
        for r in range(d):
            for t, o_ref in enumerate(outs):
                o_ref[0, r] = perm[r * rows:(r + 1) * rows, t * W_A:(t + 1) * W_A]


def _qkv_prompt(x, g, w_ext):
    n, s, dm = x.shape
    tb = QKV_TB
    nb = s // tb
    p4 = jnp.asarray(_residue_perm(tb, 4), MXU_DTYPE)
    p16 = jnp.asarray(_residue_perm(tb, 16), MXU_DTYPE)
    bf = MXU_DTYPE

    def nat(width, dt):
        return jax.ShapeDtypeStruct((n, s, width), dt), pl.BlockSpec((1, tb, width), lambda i, j: (i, j, 0))

    def res(d):
        return (jax.ShapeDtypeStruct((n, d, s // d, W_A), bf),
                pl.BlockSpec((1, d, tb // d, W_A), lambda i, j: (i, 0, j, 0)))

    outs = [nat(W_A, bf), nat(W_A, bf), nat(W_A, bf), nat(W_A, F32), nat(W_A, F32),
            res(4), res(4), res(4), res(16), res(16), res(16),
            nat(W_BQ, bf), nat(W_BQ, bf), nat(W_BKV, bf), nat(W_BKV, bf), nat(W_BKV, bf),
            nat(W_BKV, F32), nat(W_BKV, F32)]
    return pl.pallas_call(
        _qkv_prompt_kernel,
        grid=(n, nb),
        in_specs=[pl.BlockSpec((1, tb, dm), lambda i, j: (i, j, 0)),
                  pl.BlockSpec((1, dm), lambda i, j: (0, 0)),
                  pl.BlockSpec(w_ext.shape, lambda i, j: (0, 0)),
                  pl.BlockSpec((tb, tb), lambda i, j: (0, 0)),
                  pl.BlockSpec((tb, tb), lambda i, j: (0, 0))],
        out_shape=[o[0] for o in outs],
        out_specs=[o[1] for o in outs],
        compiler_params=_cparams(("parallel", "parallel")),
        name="qkv_prompt",
    )(x, g, w_ext, p4, p16)


def _qkv_sample_kernel(x_ref, g_ref, wt_ref, o_ref):
    xn = _rms(x_ref[...], g_ref[...])
    o_ref[...] = _dot_nt(wt_ref[...], xn.astype(MXU_DTYPE))


def _qkv_sample(x, g, w_t):
    return pl.pallas_call(
        _qkv_sample_kernel,
        out_shape=jax.ShapeDtypeStruct((w_t.shape[0], x.shape[0]), F32),
        compiler_params=_cparams(None),
        name="qkv_sample",
    )(x, g, w_t)


SWA_TILES = 4


def _swa_kernel(head_cfg, has_sink, want_lse, *refs):
    it = iter(refs)
    tab_ref = next(it)
    sink_ref = next(it) if has_sink else None
    q_refs = [next(it)]
    if any(c[0] == 1 for c in head_cfg):
        q_refs.append(next(it))
    kc_ref, kp_ref = next(it), next(it)
    v_refs = [(next(it), next(it))]
    if any(c[3] == 1 for c in head_cfg):
        v_refs.append((next(it), next(it)))
    o_ref = next(it)
    lse_ref = next(it) if want_lse else None

    first = pl.program_id(1) == 0
    col = lax.broadcasted_iota(jnp.int32, (SLOTS, 2 * SLOTS), 1)
    prev_pen = jnp.where((col < SLOTS) & first, NEG_INF, 0.0)
    lane = lax.broadcasted_iota(jnp.int32, (SLOTS, LANES), 1)
    low = lane < HEAD_DIM
    for t in range(kc_ref.shape[1] // SLOTS):
        rows = slice(t * SLOTS, (t + 1) * SLOTS)
        before = slice((t - 1) * SLOTS, t * SLOTS)

        def keys(cur_ref, prev_ref, lanes):
            prev = prev_ref[0, :, lanes] if t == 0 else cur_ref[0, before, lanes]
            return jnp.concatenate([prev, cur_ref[0, rows, lanes]], axis=0)

        lse_acc = jnp.zeros((SLOTS, LANES), F32)
        for p in range(len(head_cfg) // 2):
            halves = []
            for hh in range(2):
                h = 2 * p + hh
                q_src, q_half, k_tile, v_src, v_tile = head_cfg[h]
                q = q_refs[q_src][0, rows, p * LANES:(p + 1) * LANES]
                q = jnp.where(low if q_half == 0 else jnp.logical_not(low), q, jnp.zeros_like(q))
                kcat = keys(kc_ref, kp_ref, slice(k_tile * LANES, (k_tile + 1) * LANES))
                s = _dot_nt(q, kcat) + tab_ref[h]
                if t == 0:
                    s = s + prev_pen
                m = jnp.max(s, axis=-1, keepdims=True)
                if has_sink:
                    m = jnp.maximum(m, sink_ref[h])
                e = jnp.exp(s - m)
                l = jnp.sum(e, axis=-1, keepdims=True)
                if has_sink:
                    l = l + jnp.exp(sink_ref[h] - m)
                vc_ref, vp_ref = v_refs[v_src]
                vcat = keys(vc_ref, vp_ref, slice(v_tile * LANES, (v_tile + 1) * LANES))
                halves.append(_dot(e.astype(MXU_DTYPE), vcat) / l)
                if want_lse:
                    lse_acc = jnp.where(lane == h, m + jnp.log(l), lse_acc)
            o_ref[0, rows, p * LANES:(p + 1) * LANES] = jnp.where(low, halves[0], halves[1]).astype(o_ref.dtype)
        if want_lse:
            lse_ref[0, rows] = lse_acc


def _swa(tab, q_list, k, v_list, head_cfg, sinks=None, want_lse=True):
    r, s, _ = q_list[0].shape
    ck = k.shape[-1]
    rows = SWA_TILES * SLOTS
    assert s % rows == 0
    nb = s // rows
    cur = lambda i, j: (i, j, 0)
    prev = lambda i, j: (i, jnp.maximum(SWA_TILES * j - 1, 0), 0)
    in_specs = [pl.BlockSpec(tab.shape, lambda i, j: (0, 0, 0))]
    args = [tab]
    if sinks is not None:
        in_specs.append(pl.BlockSpec(memory_space=pltpu.SMEM))
        args.append(sinks)
    for q in q_list:
        in_specs.append(pl.BlockSpec((1, rows, q.shape[-1]), cur))
        args.append(q)
    in_specs += [pl.BlockSpec((1, rows, ck), cur), pl.BlockSpec((1, SLOTS, ck), prev)]
    args += [k, k]
    for v in v_list:
        in_specs += [pl.BlockSpec((1, rows, ck), cur), pl.BlockSpec((1, SLOTS, ck), prev)]
        args += [v, v]
    out_shape = [jax.ShapeDtypeStruct((r, s, W_A), MXU_DTYPE)]
    out_specs = [pl.BlockSpec((1, rows, W_A), cur)]
    if want_lse:
        out_shape.append(jax.ShapeDtypeStruct((r, s, LANES), F32))
        out_specs.append(pl.BlockSpec((1, rows, LANES), cur))
    out = pl.pallas_call(
        functools.partial(_swa_kernel, tuple(head_cfg), sinks is not None, want_lse),
        grid=(r, nb),
        in_specs=in_specs,
        out_shape=out_shape,
        out_specs=out_specs,
        compiler_params=_cparams(("parallel", "arbitrary")),
        name="swa",
    )(*args)
    return out if want_lse else out[0]


HEAD_CFG_A = tuple((0, h % 2, h // 2, 0, h // 2) for h in range(H_A))


def _head_cfg_b():
    cfg = []
    for h in range(H_B):
        c = h // G_B
        src = 0 if h % 2 == c else 1
        cfg.append((src, c, 0, src, 0))
    return tuple(cfg)


HEAD_CFG_B = _head_cfg_b()


def _shift_in(x, new_col):
    length = x.shape[-1]
    rolled = pltpu.roll(x, length - 1, axis=1)
    lane = lax.broadcasted_iota(jnp.int32, x.shape, 1)
    return jnp.where(lane == length - 1, new_col, rolled)


def _col_attention(q, kmat, vmat, k_new, v_new, bias_row, mult_row, bias_new, mult_new, sink):
    s = jnp.sum(q * kmat, axis=0, keepdims=True) + bias_row
    s_new = jnp.sum(q * k_new, axis=0, keepdims=True) + bias_new
    if mult_row is not None:
        s = jnp.where(mult_row > 0.0, s, NEG_INF)
    m = jnp.maximum(jnp.max(s, axis=-1, keepdims=True), s_new)
    if sink is not None:
        m = jnp.maximum(m, sink)
    e = jnp.exp(s - m)
    if mult_row is not None:
        e = e * mult_row
    e_new = mult_new * jnp.exp(s_new - m)
    l = jnp.sum(e, axis=-1, keepdims=True) + e_new
    if sink is not None:
        l = l + jnp.exp(sink - m)
    o = jnp.sum(vmat * e, axis=-1, keepdims=True) + v_new * e_new
    return o / l


SAMPLE_HEADS = H_A // KV_B
assert SAMPLE_HEADS == G_B


def _column(ht_ref, row0, nrows, pick):
    return jnp.sum(jnp.where(pick, ht_ref[pl.ds(row0, nrows), :], 0.0), axis=1, keepdims=True)


def _sample_kernel(bias0a_ref, bias0b_ref, sink_ref, taba_ref, mult_ref, tabb_ref, ht_ref,
                   ka_ref, va_ref, kb_ref, vb_ref, ot_ref, kao_ref, vao_ref, kbo_ref, vbo_ref):
    n, hb = pl.program_id(0), pl.program_id(1)
    lane = n % LANES
    pick = lax.broadcasted_iota(jnp.int32, (1, LANES), 1) == lane
    rows = SAMPLE_HEADS * HEAD_DIM
    blk = pl.multiple_of(hb * rows, rows)
    kvr = pl.multiple_of(hb * HEAD_DIM, HEAD_DIM)

    @pl.when((lane == 0) & (hb == 0))
    def _():
        ot_ref[...] = jnp.zeros_like(ot_ref)

    def put(row0, col):
        ot_ref[pl.ds(row0, rows), :] = jnp.where(pick, col, ot_ref[pl.ds(row0, rows), :])

    q, k_new, v_new = (_column(ht_ref, c0 + blk, rows, pick) for c0 in (0, W_A, 2 * W_A))
    mult = mult_ref[...]
    outs = []
    for hl in range(SAMPLE_HEADS):
        r = slice(hl * HEAD_DIM, (hl + 1) * HEAD_DIM)
        kmat, vmat = ka_ref[0, hl], va_ref[0, hl]
        outs.append(_col_attention(q[r], kmat, vmat, k_new[r], v_new[r], taba_ref[hl], mult,
                                   bias0a_ref[hb * SAMPLE_HEADS + hl], float(len(DILATIONS)), None))
        kao_ref[0, hl] = _shift_in(kmat, k_new[r])
        vao_ref[0, hl] = _shift_in(vmat, v_new[r])
    put(blk, jnp.concatenate(outs, axis=0))

    c3 = 3 * W_A
    q = _column(ht_ref, c3 + blk, rows, pick)
    k_new = _column(ht_ref, c3 + W_BQ + kvr, HEAD_DIM, pick)
    v_new = _column(ht_ref, c3 + W_BQ + W_BKV + kvr, HEAD_DIM, pick)
    kmat, vmat = kb_ref[0, 0], vb_ref[0, 0]
    outs = []
    for g in range(G_B):
        h = hb * G_B + g
        outs.append(_col_attention(q[g * HEAD_DIM:(g + 1) * HEAD_DIM], kmat, vmat, k_new, v_new, tabb_ref[g], None,
                                   bias0b_ref[h], 1.0, sink_ref[h]))
    put(W_A + blk, jnp.concatenate(outs, axis=0))
    kbo_ref[0, 0] = _shift_in(kmat, k_new)
    vbo_ref[0, 0] = _shift_in(vmat, v_new)


def _sample_attention(h_t, ka_t, va_t, kb_t, vb_t, tab_a, mult_a, tab_b, bias0_a, bias0_b, sinks):
    ns, nh, hd, la = ka_t.shape
    _, nkv, _, lb = kb_t.shape
    assert nh // SAMPLE_HEADS == nkv and ns % LANES == 0
    smem = pl.BlockSpec(memory_space=pltpu.SMEM)
    lanes_of = lambda rows: pl.BlockSpec((rows, LANES), lambda i, j: (0, i // LANES))
    a_spec = pl.BlockSpec((1, SAMPLE_HEADS, hd, la), lambda i, j: (i, j, 0, 0))
    b_spec = pl.BlockSpec((1, 1, hd, lb), lambda i, j: (i, j, 0, 0))
    shapes = [jax.ShapeDtypeStruct((W_A + W_BQ, ns), F32)] + [jax.ShapeDtypeStruct(c.shape, F32)
                                                              for c in (ka_t, va_t, kb_t, vb_t)]
    return pl.pallas_call(
        _sample_kernel,
        grid=(ns, nkv),
        in_specs=[smem, smem, smem,
                  pl.BlockSpec((SAMPLE_HEADS, 1, la), lambda i, j: (j, 0, 0)),
                  pl.BlockSpec((1, la), lambda i, j: (0, 0)),
                  pl.BlockSpec((G_B, 1, lb), lambda i, j: (j, 0, 0)),
                  lanes_of(h_t.shape[0]), a_spec, a_spec, b_spec, b_spec],
        out_shape=shapes,
        out_specs=[lanes_of(W_A + W_BQ), a_spec, a_spec, b_spec, b_spec],
        compiler_params=_cparams(("arbitrary", "arbitrary")),
        name="sample_attn",
    )(bias0_a, bias0_b, sinks, tab_a, mult_a, tab_b, h_t, ka_t, va_t, kb_t, vb_t)


TAIL_TB = 512


def _unpermute(pt_ref, blocks_ref, exact):
    d = blocks_ref.shape[1]
    x = jnp.concatenate([blocks_ref[0, r] for r in range(d)], axis=0)
    if not exact:
        return _dot(pt_ref[...], x)
    return sum(_dot(pt_ref[...], part) for part in _split3(x))


def _tail_prompt_kernel(x_ref, o1_ref, l1_ref, o4_ref, l4_ref, o16_ref, l16_ref, ob_ref, pt4_ref, pt16_ref,
                        w_ref, g_ref, h_ref, xn_ref):
    o_g = [o1_ref[0].astype(F32), _unpermute(pt4_ref, o4_ref, False), _unpermute(pt16_ref, o16_ref, False)]
    l_g = [l1_ref[0], _unpermute(pt4_ref, l4_ref, True), _unpermute(pt16_ref, l16_ref, True)]
    m = jnp.maximum(jnp.maximum(l_g[0], l_g[1]), l_g[2])
    e_g = [jnp.exp(l - m) for l in l_g]
    den = e_g[0] + e_g[1] + e_g[2]
    w_g = [e / den for e in e_g]
    lane = lax.broadcasted_iota(jnp.int32, (x_ref.shape[1], LANES), 1)
    low = lane < HEAD_DIM
    parts = []
    for p in range(H_A // 2):
        acc = None
        for w, o in zip(w_g, o_g):
            wp = jnp.where(low, w[:, 2 * p:2 * p + 1], w[:, 2 * p + 1:2 * p + 2])
            term = wp * o[:, p * LANES:(p + 1) * LANES]
            acc = term if acc is None else acc + term
        parts.append(acc.astype(MXU_DTYPE))
    o = jnp.concatenate(parts + [ob_ref[0]], axis=-1)
    h = x_ref[0] + _dot(o, w_ref[...])
    h_ref[0] = h
    xn_ref[0] = _rms(h, g_ref[...])


def _tail_prompt(x, o1, l1, o4, l4, o16, l16, ob, w_out, g):
    n, s, dm = x.shape
    tb = TAIL_TB
    pt4 = jnp.asarray(_residue_perm(tb, 4).T, MXU_DTYPE)
    pt16 = jnp.asarray(_residue_perm(tb, 16).T, MXU_DTYPE)
    nat = lambda w: pl.BlockSpec((1, tb, w), lambda i, j: (i, j, 0))
    res = lambda d, w: pl.BlockSpec((1, d, tb // d, w), lambda i, j: (i, 0, j, 0))
    const = lambda a: pl.BlockSpec(a.shape, lambda i, j: (0,) * a.ndim)
    return pl.pallas_call(
        _tail_prompt_kernel,
        grid=(n, s // tb),
        in_specs=[nat(dm), nat(W_A), nat(LANES), res(4, W_A), res(4, LANES), res(16, W_A), res(16, LANES),
                  nat(W_BQ), const(pt4), const(pt16), const(w_out), const(g)],
        out_shape=[jax.ShapeDtypeStruct((n, s, dm), F32), jax.ShapeDtypeStruct((n, s, dm), F32)],
        out_specs=[nat(dm), nat(dm)],
        compiler_params=_cparams(("parallel", "parallel")),
        name="tail_prompt",
    )(x, o1, l1, o4, l4, o16, l16, ob, pt4, pt16, w_out, g)


def _tail_sample_kernel(x_ref, ot_ref, w_ref, g_ref, h_ref, xn_ref):
    h = x_ref[...] + _dot(ot_ref[...].T.astype(MXU_DTYPE), w_ref[...])
    h_ref[...] = h
    xn_ref[...] = _rms(h, g_ref[...])


def _tail_sample(x, o_t, w_out, g):
    return pl.pallas_call(
        _tail_sample_kernel,
        out_shape=[jax.ShapeDtypeStruct(x.shape, F32), jax.ShapeDtypeStruct(x.shape, F32)],
        compiler_params=_cparams(None),
        name="tail_sample",
    )(x, o_t, w_out, g)


def _sorting_network(n):
    size = 1 << (n - 1).bit_length()

    def merge(lo, hi, r):
        step = r * 2
        if step < hi - lo:
            yield from merge(lo, hi, step)
            yield from merge(lo + r, hi, step)
            yield from ((i, i + r) for i in range(lo + r, hi - r, step))
        else:
            yield (lo, lo + r)

    def sort(lo, hi):
        if hi - lo >= 1:
            mid = lo + (hi - lo) // 2
            yield from sort(lo, mid)
            yield from sort(mid + 1, hi)
            yield from merge(lo, hi, 1)

    return [(i, j) for i, j in sort(0, size - 1) if j < n]


def _top_rows(s, count):
    tiles = [s[v * SUBLANES:(v + 1) * SUBLANES] for v in range(s.shape[0] // SUBLANES)]
    for i, j in _sorting_network(len(tiles)):
        tiles[i], tiles[j] = jnp.maximum(tiles[i], tiles[j]), jnp.minimum(tiles[i], tiles[j])
    rows = []
    for t in range(count):
        m = jnp.max(tiles[0], axis=0, keepdims=True)
        rows.append(m)
        if t + 1 < count:
            popped = tiles[0] == m
            depth = min(len(tiles), count - t - 1)
            for k in range(depth):
                below = tiles[k + 1] if k + 1 < len(tiles) else NEG_INF
                tiles[k] = jnp.where(popped, below, tiles[k])
    return rows


def _pad_rows(rows, count):
    pad = [jnp.full_like(rows[0], NEG_INF)] * (count - len(rows))
    return jnp.concatenate(list(rows) + pad, axis=0)


def _candidate_sums(top1, top2):
    k = PEER_TOPK + 1
    wide = -(-k // SUBLANES) * SUBLANES
    narrow = -(-(k // 2) // SUBLANES) * SUBLANES
    assert k // (narrow + 1) <= 1
    v2_wide = _pad_rows(top2, wide)
    v2_narrow = v2_wide[:narrow]
    parts = [top1[0] + v2_wide] + [top1[a] + v2_narrow for a in range(1, narrow)]
    parts.append(_pad_rows(top1[narrow:], -(-(k - narrow) // SUBLANES) * SUBLANES) + top2[0])
    return jnp.concatenate(parts, axis=0)


def _route_kernel(x_ref, wqh_ref, wql_ref, skh_ref, skl_ref, thr_ref, p1_ref, p2_ref):
    xh, xl = _split2(x_ref[...])
    wqh = wqh_ref[...]
    q_t = _dot_nt(wqh, xh) + _dot_nt(wqh, xl) + _dot_nt(wql_ref[...], xh)
    half = D_KEY // 2
    for h in range(PEER_HEADS):
        scores = []
        for c in range(2):
            qh, ql = _split2(q_t[h * D_KEY + c * half:h * D_KEY + (c + 1) * half, :])
            scores.append(_dot(skh_ref[c], qh) + _dot(skh_ref[c], ql) + _dot(skl_ref[c], qh))
        for g in range(q_t.shape[1] // LANES):
            s1, s2 = (s[:, g * LANES:(g + 1) * LANES] for s in scores)
            top1, top2 = _top_rows(s1, PEER_TOPK + 1), _top_rows(s2, PEER_TOPK + 1)
            best = _top_rows(_candidate_sums(top1, top2), PEER_TOPK + 1)
            tau = 0.5 * (best[PEER_TOPK - 1] + best[PEER_TOPK])
            z = sum(jnp.exp(b - best[0]) for b in best[:PEER_TOPK])
            log_norm = best[0] + jnp.log(z)
            m2 = top2[0]
            thr_ref[h, g] = jnp.exp((tau - m2) - s1)
            p1_ref[h, g] = jnp.exp(s1 + (m2 - log_norm))
            p2_ref[h, g] = jnp.exp(s2 - m2)


def _route(xn, wq_t_hi, wq_t_lo, sk_hi, sk_lo, tb):
    t, dm = xn.shape
    const = lambda a: pl.BlockSpec(a.shape, lambda i: (0,) * a.ndim)
    gspec = pl.BlockSpec((PEER_HEADS, tb // LANES, N_KEYS, LANES), lambda i: (0, i, 0, 0))
    gshape = jax.ShapeDtypeStruct((PEER_HEADS, t // LANES, N_KEYS, LANES), F32)
    return pl.pallas_call(
        _route_kernel,
        grid=(t // tb,),
        in_specs=[pl.BlockSpec((tb, dm), lambda i: (i, 0)), const(wq_t_hi), const(wq_t_lo), const(sk_hi), const(sk_lo)],
        out_shape=[gshape] * 3,
        out_specs=[gspec] * 3,
        compiler_params=_cparams(("parallel",)),
        name="peer_route",
    )(xn, wq_t_hi, wq_t_lo, sk_hi, sk_lo)


EXPERT_CHUNK = SUBLANES * N_KEYS
GATE_JBLOCK = 2 * SUBLANES
INV_SQRT2 = 0.7071067811865476


def _gate_and_activate(g, a_ref, p_ref, thr_ref, p1_ref, p2_ref):
    nv = GATE_JBLOCK // SUBLANES
    for jb in range(N_KEYS // GATE_JBLOCK):
        w = [[None] * nv for _ in range(SUBLANES)]
        for h in range(PEER_HEADS):
            p2 = [p2_ref[h, g, pl.ds(jb * GATE_JBLOCK + v * SUBLANES, SUBLANES), :] for v in range(nv)]
            for r in range(SUBLANES):
                thr = jnp.broadcast_to(thr_ref[h, g, pl.ds(r, 1), :], (SUBLANES, LANES))
                p1 = jnp.broadcast_to(p1_ref[h, g, pl.ds(r, 1), :], (SUBLANES, LANES))
                for v in range(nv):
                    term = jnp.where(p2[v] >= thr, p1 * p2[v], 0.0)
                    w[r][v] = term if w[r][v] is None else w[r][v] + term
        for r in range(SUBLANES):
            row0 = r * N_KEYS + jb * GATE_JBLOCK
            a = a_ref[g, pl.ds(row0, GATE_JBLOCK), :]
            half = 0.5 * a
            act = half + half * lax.erf(a * INV_SQRT2)
            p_ref[g, pl.ds(row0, GATE_JBLOCK), :] = (jnp.concatenate(w[r], axis=0) * act).astype(p_ref.dtype)


def _experts_kernel(xn_ref, res_ref, g_ref, down_ref, upt_ref, thr_a_ref, p1_a_ref, thr_b_ref, p1_b_ref,
                    p2_ref, y_ref, xb_ref, a0_ref, a1_ref, pb0_ref, pb1_ref, acc_ref):
    i, j = pl.program_id(0), pl.program_id(1)
    ch = EXPERT_CHUNK
    ng, dm = a0_ref.shape[0], acc_ref.shape[0]
    slab_a, slab_u = ch // ng, dm // ng

    @pl.when((i == 0) & (j == 0))
    def _():
        for ref in (a0_ref, a1_ref, pb0_ref, pb1_ref):
            ref[...] = jnp.zeros_like(ref)

    @pl.when(j == 0)
    def _():
        xb_ref[...] = xn_ref[...].astype(MXU_DTYPE)

    keep = j >= 1

    def tick(half, a_in_ref, a_out_ref, p_out_ref, p_in_ref, thr_ref, p1_ref):
        for g in range(ng):
            ra = g * slab_a
            nxt = _dot_nt(down_ref[pl.ds(half * ch + ra, slab_a), :], xb_ref[...])
            for gg in range(ng):
                a_out_ref[gg, pl.ds(ra, slab_a), :] = nxt[:, gg * LANES:(gg + 1) * LANES]
            ru = g * slab_u
            p_in = jnp.concatenate([p_in_ref[gg] for gg in range(ng)], axis=1)
            upd = _dot(upt_ref[pl.ds(ru, slab_u), half * ch:(half + 1) * ch], p_in)
            acc_ref[pl.ds(ru, slab_u), :] = jnp.where(keep, acc_ref[pl.ds(ru, slab_u), :] + upd, 0.0)
            _gate_and_activate(g, a_in_ref, p_out_ref, thr_ref, p1_ref, p2_ref)

    tick(0, a1_ref, a0_ref, pb1_ref, pb0_ref, thr_a_ref, p1_a_ref)
    tick(1, a0_ref, a1_ref, pb0_ref, pb1_ref, thr_b_ref, p1_b_ref)

    @pl.when(j == pl.num_programs(1) - 1)
    def _():
        y = res_ref[...] + acc_ref[...].T
        y_ref[...] = _rms(y, g_ref[...])


def _experts(xn, res, g, down, up_t, thr, p1, p2, tb):
    t, dm = xn.shape
    ne = down.shape[0]
    step = 2 * EXPERT_CHUNK
    nj = ne // step
    ng = tb // LANES
    tok = pl.BlockSpec((tb, dm), lambda i, j: (i, 0))
    rshape = (PEER_HEADS, ng, SUBLANES, LANES)
    row_a = pl.BlockSpec(rshape, lambda i, j: (0, i, jnp.maximum(2 * j - 1, 0), 0))
    row_b = pl.BlockSpec(rshape, lambda i, j: (0, i, jnp.minimum(2 * j, 2 * nj - 1), 0))
    full = pl.BlockSpec((PEER_HEADS, ng, N_KEYS, LANES), lambda i, j: (0, i, 0, 0))
    return pl.pallas_call(
        _experts_kernel,
        grid=(t // tb, nj + 1),
        in_specs=[tok, tok, pl.BlockSpec((1, dm), lambda i, j: (0, 0)),
                  pl.BlockSpec((step, dm), lambda i, j: (jnp.minimum(j, nj - 1), 0)),
                  pl.BlockSpec((dm, step), lambda i, j: (0, jnp.maximum(j - 1, 0))),
                  row_a, row_a, row_b, row_b, full],
        out_shape=jax.ShapeDtypeStruct((t, dm), F32),
        out_specs=tok,
        scratch_shapes=[pltpu.VMEM((tb, dm), MXU_DTYPE),
                        pltpu.VMEM((ng, EXPERT_CHUNK, LANES), F32), pltpu.VMEM((ng, EXPERT_CHUNK, LANES), F32),
                        pltpu.VMEM((ng, EXPERT_CHUNK, LANES), MXU_DTYPE),
                        pltpu.VMEM((ng, EXPERT_CHUNK, LANES), MXU_DTYPE),
                        pltpu.VMEM((dm, tb), F32)],
        compiler_params=_cparams(("arbitrary", "arbitrary")),
        name="peer_experts",
    )(xn, res, g, down, up_t, thr, p1, thr, p1, p2)


PEER_TB = 512


def _peer_and_final(xn, res, g_final, peer_w, tb):
    wq_t_hi, wq_t_lo, sk_hi, sk_lo, down, up_t = peer_w
    thr, p1, p2 = _route(xn, wq_t_hi, wq_t_lo, sk_hi, sk_lo, tb)
    return _experts(xn, res, g_final, down, up_t, thr, p1, p2, tb)


def _cache_to_feature_major(c):
    return jnp.transpose(c, (0, 2, 3, 1))


def _cache_from_feature_major(c):
    return jnp.transpose(c, (0, 3, 1, 2))


def _layer(xp, xs, cache_a_k, cache_a_v, cache_b_k, cache_b_v, norm_attn, w_in, rel_bias, sinks, w_out, norm_ffn,
           w_peer_q, peer_sub_keys, peer_down, peer_up):
    n, s, dm = xp.shape
    ns = xs.shape[0]
    la, lb = cache_a_k.shape[1], cache_b_k.shape[1]

    c3 = 3 * W_A
    w_q_scaled = jnp.concatenate([w_in[:, :W_A] * SCALE, w_in[:, W_A:c3], w_in[:, c3:c3 + W_BQ] * SCALE,
                                  w_in[:, c3 + W_BQ:]], axis=1)
    qb_cols = w_q_scaled[:, c3:c3 + W_BQ].reshape(dm, H_B // 2, 2, HEAD_DIM)[:, :, ::-1].reshape(dm, W_BQ)
    vb_cols = w_in[:, c3 + W_BQ + W_BKV:].reshape(dm, KV_B, HEAD_DIM)[:, ::-1].reshape(dm, W_BKV)
    w_ext = jnp.concatenate([w_q_scaled, qb_cols, vb_cols], axis=1).astype(MXU_DTYPE)
    w_nat_t = w_q_scaled.T.astype(MXU_DTYPE)
    w_out_b = w_out.astype(MXU_DTYPE)
    g_attn, g_ffn = norm_attn[None, :], norm_ffn[None, :]
    wq_t = w_peer_q.T
    wq_t_hi = wq_t.astype(MXU_DTYPE)
    wq_t_lo = (wq_t - wq_t_hi.astype(F32)).astype(MXU_DTYPE)
    sk_hi = peer_sub_keys.astype(MXU_DTYPE)
    sk_lo = (peer_sub_keys - sk_hi.astype(F32)).astype(MXU_DTYPE)
    peer_w = (wq_t_hi, wq_t_lo, sk_hi, sk_lo, peer_down.astype(MXU_DTYPE), peer_up.T.astype(MXU_DTYPE))

    tabs_a = [_bias_table(_window_bucket_matrix(d), rel_bias, 0, H_A) for d in DILATIONS]
    tab_b = _bias_table(_window_bucket_matrix(1), rel_bias, H_A, H_B)
    bidx_sa, mult_sa = _sample_tables(la)
    bidx_sb, _ = _sample_tables(lb)
    tab_sa = _bias_table(bidx_sa, rel_bias, 0, H_A)
    tab_sb = _bias_table(bidx_sb, rel_bias, H_A, H_B)
    bias0_a, bias0_b = rel_bias[0, :H_A], rel_bias[0, H_A:]

    (q1, k1, v1, kf, vf, q4, k4, v4, q16, k16, v16, qb, qbs, kb, vb, vbs, kbf, vbf) = _qkv_prompt(xp, g_attn, w_ext)
    o1, l1 = _swa(tabs_a[0], [q1], k1, [v1], HEAD_CFG_A)
    flat = lambda a: a.reshape((a.shape[0] * a.shape[1],) + a.shape[2:])
    o4, l4 = _swa(tabs_a[1], [flat(q4)], flat(k4), [flat(v4)], HEAD_CFG_A)
    o16, l16 = _swa(tabs_a[2], [flat(q16)], flat(k16), [flat(v16)], HEAD_CFG_A)
    ob = _swa(tab_b, [qb, qbs], kb, [vb, vbs], HEAD_CFG_B, sinks=sinks, want_lse=False)
    unflat = lambda a, d: a.reshape((n, d) + a.shape[1:])
    hp, xnp = _tail_prompt(xp, o1, l1, unflat(o4, 4), unflat(l4, 4), unflat(o16, 16), unflat(l16, 16), ob,
                           w_out_b, g_ffn)
    prompt_caches = (kf[:, s - min(la, s):].reshape(n, -1, H_A, HEAD_DIM),
                     vf[:, s - min(la, s):].reshape(n, -1, H_A, HEAD_DIM),
                     kbf[:, s - min(lb, s):].reshape(n, -1, KV_B, HEAD_DIM),
                     vbf[:, s - min(lb, s):].reshape(n, -1, KV_B, HEAD_DIM))

    hs_t = _qkv_sample(xs, g_attn, w_nat_t)
    o_t, *shifted = _sample_attention(
        hs_t, _cache_to_feature_major(cache_a_k), _cache_to_feature_major(cache_a_v),
        _cache_to_feature_major(cache_b_k), _cache_to_feature_major(cache_b_v),
        tab_sa, jnp.asarray(mult_sa), tab_sb, bias0_a, bias0_b, sinks)
    h_s, xn_s = _tail_sample(xs, o_t, w_out_b, g_ffn)
    sample_caches = tuple(_cache_from_feature_major(c) for c in shifted)
    return (hp.reshape(n * s, dm), xnp.reshape(n * s, dm), h_s, xn_s, peer_w, prompt_caches, sample_caches)


def kernel(x_prompt, x_sample, cache_a_k, cache_a_v, cache_b_k, cache_b_v, norm_attn, w_in, rel_bias, sinks, w_out,
           norm_ffn, w_peer_q, peer_sub_keys, peer_down, peer_up, norm_final):
    depth = w_in.shape[0]
    assert depth == 1, "single-layer trunk"
    n, s, dm = x_prompt.shape
    ns = x_sample.shape[0]
    assert x_sample.shape[1] == 1 and s % QKV_TB == 0 and (n * s) % PEER_TB == 0 and ns % LANES == 0
    l = 0
    hp, xnp, h_s, xn_s, peer_w, prompt_caches, sample_caches = _layer(
        x_prompt, x_sample[:, 0], cache_a_k[l], cache_a_v[l], cache_b_k[l], cache_b_v[l], norm_attn[l], w_in[l],
        rel_bias, sinks[l], w_out[l], norm_ffn[l], w_peer_q[l], peer_sub_keys[l], peer_down[l], peer_up[l])
    g_final = norm_final[None, :]
    y_prompt = _peer_and_final(xnp, hp, g_final, peer_w, PEER_TB).reshape(n, s, dm)
    y_sample = _peer_and_final(xn_s, h_s, g_final, peer_w, LANES).reshape(ns, 1, dm)
    return (y_prompt, y_sample) + tuple(c[None] for c in prompt_caches) + tuple(c[None] for c in sample_caches)
```

```python
import functools
import math

import numpy as np
import jax
import jax.numpy as jnp
from jax import lax
from jax.experimental import pallas as pl
from jax.experimental.pallas import tpu as pltpu

HEAD_DIM = 64
H_A = 8
H_B = 8
KV_B = 2
G_B = H_B // KV_B
DILATIONS = (1, 4, 16)
SLOTS = 128
N_BUCKETS = 32
MAX_DISTANCE = 2048
N_KEYS = 128
PEER_HEADS = 8
PEER_TOPK = 16
D_KEY = 128
RMS_EPS = 1e-6
SCALE = HEAD_DIM ** -0.5
W_A = H_A * HEAD_DIM
W_BQ = H_B * HEAD_DIM
W_BKV = KV_B * HEAD_DIM

LANES = 128
SUBLANES = 8
VMEM_LIMIT_BYTES = 56 * 1024 * 1024

MXU_DTYPE = jnp.bfloat16
F32 = jnp.float32
NEG_INF = float("-inf")


def _cparams(sem, flags=None):
    return pltpu.CompilerParams(dimension_semantics=sem, vmem_limit_bytes=VMEM_LIMIT_BYTES, flags=flags)


def _t5_bucket_np(dist):
    exact = N_BUCKETS // 2
    d = np.maximum(dist, 0)
    logd = np.log(np.maximum(d, 1).astype(np.float32) / np.float32(exact)) / np.float32(math.log(MAX_DISTANCE / exact))
    large = np.minimum(exact + (logd * np.float32(N_BUCKETS - exact)).astype(np.int32), N_BUCKETS - 1)
    return np.where(d < exact, d, large).astype(np.int32)


def _window_bucket_matrix(dilation):
    i = np.arange(SLOTS)[:, None]
    j = np.arange(2 * SLOTS)[None, :]
    dist = i - j + SLOTS
    ok = (dist >= 0) & (dist <= SLOTS)
    return np.where(ok, _t5_bucket_np(dist * dilation), -1).astype(np.int32)


def _sample_tables(cache_len):
    delta = cache_len - np.arange(cache_len)
    mult = np.zeros(cache_len, np.float32)
    for d in DILATIONS:
        mult += ((delta % d == 0) & (delta <= SLOTS * d)).astype(np.float32)
    return _t5_bucket_np(delta)[None, :], mult[None, :]


def _residue_perm(tb, d):
    p = np.zeros((tb, tb), np.float32)
    rows = np.arange(tb)
    p[rows, (rows % (tb // d)) * d + rows // (tb // d)] = 1.0
    return p


def _rms(x, g):
    return x * lax.rsqrt(jnp.mean(x * x, axis=-1, keepdims=True) + RMS_EPS) * g


def _dot(a, b):
    return jnp.dot(a, b, preferred_element_type=F32)


def _dot_nt(a, b):
    return lax.dot_general(a, b, (((1,), (1,)), ((), ())), preferred_element_type=F32)


def _split2(x):
    hi = x.astype(MXU_DTYPE)
    lo = (x - hi.astype(F32)).astype(MXU_DTYPE)
    return hi, lo


def _split3(x):
    hi = x.astype(MXU_DTYPE)
    r1 = x - hi.astype(F32)
    mid = r1.astype(MXU_DTYPE)
    lo = (r1 - mid.astype(F32)).astype(MXU_DTYPE)
    return hi, mid, lo


def _bias_kernel(col0, nheads, bidx_ref, rb_ref, o_ref):
    b = bidx_ref[...]
    for h in range(nheads):
        val = jnp.full(b.shape, NEG_INF, F32)
        for k in range(N_BUCKETS):
            val = jnp.where(b == k, rb_ref[k, col0 + h], val)
        o_ref[h] = val


def _bias_table(bidx, rel_bias, col0, nheads):
    r, c = bidx.shape
    return pl.pallas_call(
        functools.partial(_bias_kernel, col0, nheads),
        out_shape=jax.ShapeDtypeStruct((nheads, r, c), F32),
        in_specs=[pl.BlockSpec((r, c), lambda: (0, 0)), pl.BlockSpec(memory_space=pltpu.SMEM)],
        out_specs=pl.BlockSpec((nheads, r, c), lambda: (0, 0, 0)),
        name="bias_table",
    )(jnp.asarray(bidx), rel_bias)


QKV_TB = 512


def _qkv_prompt_kernel(x_ref, g_ref, w_ref, p4_ref, p16_ref,
                       q1_ref, k1_ref, v1_ref, kf_ref, vf_ref,
                       q4_ref, k4_ref, v4_ref, q16_ref, k16_ref, v16_ref,
                       qb_ref, qbs_ref, kb_ref, vb_ref, vbs_ref, kbf_ref, vbf_ref):
    xn = _rms(x_ref[0], g_ref[...])
    h = _dot(xn.astype(MXU_DTYPE), w_ref[...])
    c0, c1, c2, c3 = 0, W_A, 2 * W_A, 3 * W_A
    c4, c5, c6 = c3 + W_BQ, c3 + W_BQ + W_BKV, c3 + W_BQ + 2 * W_BKV
    c7 = c6 + W_BQ
    ha = h[:, :c3].astype(MXU_DTYPE)
    q1_ref[0] = ha[:, c0:c1]
    k1_ref[0] = ha[:, c1:c2]
    v1_ref[0] = ha[:, c2:c3]
    kf_ref[0] = h[:, c1:c2]
    vf_ref[0] = h[:, c2:c3]
    qb_ref[0] = h[:, c3:c4].astype(MXU_DTYPE)
    kb_ref[0] = h[:, c4:c5].astype(MXU_DTYPE)
    vb_ref[0] = h[:, c5:c6].astype(MXU_DTYPE)
    kbf_ref[0] = h[:, c4:c5]
    vbf_ref[0] = h[:, c5:c6]
    qbs_ref[0] = h[:, c6:c7].astype(MXU_DTYPE)
    vbs_ref[0] = h[:, c7:].astype(MXU_DTYPE)
    tb = ha.shape[0]
    for d, p_ref, outs in ((4, p4_ref, (q4_ref, k4_ref, v4_ref)), (16, p16_ref, (q16_ref, k16_ref, v16_ref))):
        perm = _dot(p_ref[...], ha).astype(MXU_DTYPE)
        rows = tb // d
        for r in range(d):
            for t, o_ref in enumerate(outs):
                o_ref[0, r] = perm[r * rows:(r + 1) * rows, t * W_A:(t + 1) * W_A]


def _qkv_prompt(x, g, w_ext):
    n, s, dm = x.shape
    tb = QKV_TB
    nb = s // tb
    p4 = jnp.asarray(_residue_perm(tb, 4), MXU_DTYPE)
    p16 = jnp.asarray(_residue_perm(tb, 16), MXU_DTYPE)
    bf = MXU_DTYPE

    def nat(width, dt):
        return jax.ShapeDtypeStruct((n, s, width), dt), pl.BlockSpec((1, tb, width), lambda i, j: (i, j, 0))

    def res(d):
        return (jax.ShapeDtypeStruct((n, d, s // d, W_A), bf),
                pl.BlockSpec((1, d, tb // d, W_A), lambda i, j: (i, 0, j, 0)))

    outs = [nat(W_A, bf), nat(W_A, bf), nat(W_A, bf), nat(W_A, F32), nat(W_A, F32),
            res(4), res(4), res(4), res(16), res(16), res(16),
            nat(W_BQ, bf), nat(W_BQ, bf), nat(W_BKV, bf), nat(W_BKV, bf), nat(W_BKV, bf),
            nat(W_BKV, F32), nat(W_BKV, F32)]
    return pl.pallas_call(
        _qkv_prompt_kernel,
        grid=(n, nb),
        in_specs=[pl.BlockSpec((1, tb, dm), lambda i, j: (i, j, 0)),
                  pl.BlockSpec((1, dm), lambda i, j: (0, 0)),
                  pl.BlockSpec(w_ext.shape, lambda i, j: (0, 0)),
                  pl.BlockSpec((tb, tb), lambda i, j: (0, 0)),
                  pl.BlockSpec((tb, tb), lambda i, j: (0, 0))],
        out_shape=[o[0] for o in outs],
        out_specs=[o[1] for o in outs],
        compiler_params=_cparams(("parallel", "parallel")),
        name="qkv_prompt",
    )(x, g, w_ext, p4, p16)


def _qkv_sample_kernel(x_ref, g_ref, wt_ref, o_ref):
    xn = _rms(x_ref[...], g_ref[...])
    o_ref[...] = _dot_nt(wt_ref[...], xn.astype(MXU_DTYPE))


def _qkv_sample(x, g, w_t):
    return pl.pallas_call(
        _qkv_sample_kernel,
        out_shape=jax.ShapeDtypeStruct((w_t.shape[0], x.shape[0]), F32),
        compiler_params=_cparams(None),
        name="qkv_sample",
    )(x, g, w_t)


SWA_TILES = 4


def _swa_kernel(head_cfg, has_sink, want_lse, *refs):
    it = iter(refs)
    tab_ref = next(it)
    sink_ref = next(it) if has_sink else None
    q_refs = [next(it)]
    if any(c[0] == 1 for c in head_cfg):
        q_refs.append(next(it))
    kc_ref, kp_ref = next(it), next(it)
    v_refs = [(next(it), next(it))]
    if any(c[3] == 1 for c in head_cfg):
        v_refs.append((next(it), next(it)))
    o_ref = next(it)
    lse_ref = next(it) if want_lse else None

    first = pl.program_id(1) == 0
    col = lax.broadcasted_iota(jnp.int32, (SLOTS, 2 * SLOTS), 1)
    prev_pen = jnp.where((col < SLOTS) & first, NEG_INF, 0.0)
    lane = lax.broadcasted_iota(jnp.int32, (SLOTS, LANES), 1)
    low = lane < HEAD_DIM
    for t in range(kc_ref.shape[1] // SLOTS):
        rows = slice(t * SLOTS, (t + 1) * SLOTS)
        before = slice((t - 1) * SLOTS, t * SLOTS)

        def keys(cur_ref, prev_ref, lanes):
            prev = prev_ref[0, :, lanes] if t == 0 else cur_ref[0, before, lanes]
            return jnp.concatenate([prev, cur_ref[0, rows, lanes]], axis=0)

        lse_acc = jnp.zeros((SLOTS, LANES), F32)
        for p in range(len(head_cfg) // 2):
            halves = []
            for hh in range(2):
                h = 2 * p + hh
                q_src, q_half, k_tile, v_src, v_tile = head_cfg[h]
                q = q_refs[q_src][0, rows, p * LANES:(p + 1) * LANES]
                q = jnp.where(low if q_half == 0 else jnp.logical_not(low), q, jnp.zeros_like(q))
                kcat = keys(kc_ref, kp_ref, slice(k_tile * LANES, (k_tile + 1) * LANES))
                s = _dot_nt(q, kcat) + tab_ref[h]
                if t == 0:
                    s = s + prev_pen
                m = jnp.max(s, axis=-1, keepdims=True)
                if has_sink:
                    m = jnp.maximum(m, sink_ref[h])
                e = jnp.exp(s - m)
                l = jnp.sum(e, axis=-1, keepdims=True)
                if has_sink:
                    l = l + jnp.exp(sink_ref[h] - m)
                vc_ref, vp_ref = v_refs[v_src]
                vcat = keys(vc_ref, vp_ref, slice(v_tile * LANES, (v_tile + 1) * LANES))
                halves.append(_dot(e.astype(MXU_DTYPE), vcat) / l)
                if want_lse:
                    lse_acc = jnp.where(lane == h, m + jnp.log(l), lse_acc)
            o_ref[0, rows, p * LANES:(p + 1) * LANES] = jnp.where(low, halves[0], halves[1]).astype(o_ref.dtype)
        if want_lse:
            lse_ref[0, rows] = lse_acc


def _swa(tab, q_list, k, v_list, head_cfg, sinks=None, want_lse=True):
    r, s, _ = q_list[0].shape
    ck = k.shape[-1]
    tiles = math.gcd(SWA_TILES, s // SLOTS)
    rows = tiles * SLOTS
    assert s % rows == 0
    nb = s // rows
    cur = lambda i, j: (i, j, 0)
    prev = lambda i, j: (i, jnp.maximum(tiles * j - 1, 0), 0)
    in_specs = [pl.BlockSpec(tab.shape, lambda i, j: (0, 0, 0))]
    args = [tab]
    if sinks is not None:
        in_specs.append(pl.BlockSpec(memory_space=pltpu.SMEM))
        args.append(sinks)
    for q in q_list:
        in_specs.append(pl.BlockSpec((1, rows, q.shape[-1]), cur))
        args.append(q)
    in_specs += [pl.BlockSpec((1, rows, ck), cur), pl.BlockSpec((1, SLOTS, ck), prev)]
    args += [k, k]
    for v in v_list:
        in_specs += [pl.BlockSpec((1, rows, ck), cur), pl.BlockSpec((1, SLOTS, ck), prev)]
        args += [v, v]
    out_shape = [jax.ShapeDtypeStruct((r, s, W_A), MXU_DTYPE)]
    out_specs = [pl.BlockSpec((1, rows, W_A), cur)]
    if want_lse:
        out_shape.append(jax.ShapeDtypeStruct((r, s, LANES), F32))
        out_specs.append(pl.BlockSpec((1, rows, LANES), cur))
    out = pl.pallas_call(
        functools.partial(_swa_kernel, tuple(head_cfg), sinks is not None, want_lse),
        grid=(r, nb),
        in_specs=in_specs,
        out_shape=out_shape,
        out_specs=out_specs,
        compiler_params=_cparams(("parallel", "arbitrary")),
        name="swa",
    )(*args)
    return out if want_lse else out[0]


HEAD_CFG_A = tuple((0, h % 2, h // 2, 0, h // 2) for h in range(H_A))


def _head_cfg_b():
    cfg = []
    for h in range(H_B):
        c = h // G_B
        src = 0 if h % 2 == c else 1
        cfg.append((src, c, 0, src, 0))
    return tuple(cfg)


HEAD_CFG_B = _head_cfg_b()


def _shift_in(x, new_col):
    length = x.shape[-1]
    rolled = pltpu.roll(x, length - 1, axis=1)
    lane = lax.broadcasted_iota(jnp.int32, x.shape, 1)
    return jnp.where(lane == length - 1, new_col, rolled)


def _col_attention(q, kmat, vmat, k_new, v_new, bias_row, mult_row, bias_new, mult_new, sink):
    s = jnp.sum(q * kmat, axis=0, keepdims=True) + bias_row
    s_new = jnp.sum(q * k_new, axis=0, keepdims=True) + bias_new
    if mult_row is not None:
        s = jnp.where(mult_row > 0.0, s, NEG_INF)
    m = jnp.maximum(jnp.max(s, axis=-1, keepdims=True), s_new)
    if sink is not None:
        m = jnp.maximum(m, sink)
    e = jnp.exp(s - m)
    if mult_row is not None:
        e = e * mult_row
    e_new = mult_new * jnp.exp(s_new - m)
    l = jnp.sum(e, axis=-1, keepdims=True) + e_new
    if sink is not None:
        l = l + jnp.exp(sink - m)
    o = jnp.sum(vmat * e, axis=-1, keepdims=True) + v_new * e_new
    return o / l


SAMPLE_HEADS = H_A // KV_B
assert SAMPLE_HEADS == G_B


def _column(ht_ref, row0, nrows, pick):
    return jnp.sum(jnp.where(pick, ht_ref[pl.ds(row0, nrows), :], 0.0), axis=1, keepdims=True)


def _sample_kernel(bias0a_ref, bias0b_ref, sink_ref, taba_ref, mult_ref, tabb_ref, ht_ref,
                   ka_ref, va_ref, kb_ref, vb_ref, ot_ref, kao_ref, vao_ref, kbo_ref, vbo_ref):
    n, hb = pl.program_id(0), pl.program_id(1)
    lane = n % LANES
    pick = lax.broadcasted_iota(jnp.int32, (1, LANES), 1) == lane
    rows = SAMPLE_HEADS * HEAD_DIM
    blk = pl.multiple_of(hb * rows, rows)
    kvr = pl.multiple_of(hb * HEAD_DIM, HEAD_DIM)

    @pl.when((lane == 0) & (hb == 0))
    def _():
        ot_ref[...] = jnp.zeros_like(ot_ref)

    def put(row0, col):
        ot_ref[pl.ds(row0, rows), :] = jnp.where(pick, col, ot_ref[pl.ds(row0, rows), :])

    q, k_new, v_new = (_column(ht_ref, c0 + blk, rows, pick) for c0 in (0, W_A, 2 * W_A))
    mult = mult_ref[...]
    outs = []
    for hl in range(SAMPLE_HEADS):
        r = slice(hl * HEAD_DIM, (hl + 1) * HEAD_DIM)
        kmat, vmat = ka_ref[0, hl], va_ref[0, hl]
        outs.append(_col_attention(q[r], kmat, vmat, k_new[r], v_new[r], taba_ref[hl], mult,
                                   bias0a_ref[hb * SAMPLE_HEADS + hl], float(len(DILATIONS)), None))
        kao_ref[0, hl] = _shift_in(kmat, k_new[r])
        vao_ref[0, hl] = _shift_in(vmat, v_new[r])
    put(blk, jnp.concatenate(outs, axis=0))

    c3 = 3 * W_A
    q = _column(ht_ref, c3 + blk, rows, pick)
    k_new = _column(ht_ref, c3 + W_BQ + kvr, HEAD_DIM, pick)
    v_new = _column(ht_ref, c3 + W_BQ + W_BKV + kvr, HEAD_DIM, pick)
    kmat, vmat = kb_ref[0, 0], vb_ref[0, 0]
    outs = []
    for g in range(G_B):
        h = hb * G_B + g
        outs.append(_col_attention(q[g * HEAD_DIM:(g + 1) * HEAD_DIM], kmat, vmat, k_new, v_new, tabb_ref[g], None,
                                   bias0b_ref[h], 1.0, sink_ref[h]))
    put(W_A + blk, jnp.concatenate(outs, axis=0))
    kbo_ref[0, 0] = _shift_in(kmat, k_new)
    vbo_ref[0, 0] = _shift_in(vmat, v_new)


def _sample_specs(bias0_a, bias0_b, sinks, tab_a, mult_a, tab_b, h_t, ka_t, va_t, kb_t, vb_t):
    ns, nh, hd, la = ka_t.shape
    _, nkv, _, lb = kb_t.shape
    assert nh // SAMPLE_HEADS == nkv and ns % LANES == 0
    smem = pl.BlockSpec(memory_space=pltpu.SMEM)
    lanes_of = lambda rows: pl.BlockSpec((rows, LANES), lambda i, j: (0, i // LANES))
    a_spec = pl.BlockSpec((1, SAMPLE_HEADS, hd, la), lambda i, j: (i, j, 0, 0))
    b_spec = pl.BlockSpec((1, 1, hd, lb), lambda i, j: (i, j, 0, 0))
    in_specs = [smem, smem, smem,
                pl.BlockSpec((SAMPLE_HEADS, 1, la), lambda i, j: (j, 0, 0)),
                pl.BlockSpec((1, la), lambda i, j: (0, 0)),
                pl.BlockSpec((G_B, 1, lb), lambda i, j: (j, 0, 0)),
                lanes_of(h_t.shape[0]), a_spec, a_spec, b_spec, b_spec]
    out_specs = [lanes_of(W_A + W_BQ), a_spec, a_spec, b_spec, b_spec]
    shapes = [jax.ShapeDtypeStruct((W_A + W_BQ, ns), F32)] + [jax.ShapeDtypeStruct(c.shape, F32)
                                                              for c in (ka_t, va_t, kb_t, vb_t)]
    return in_specs, out_specs, shapes


TAIL_TB = 512


def _unpermute(pt_ref, blocks_ref, exact):
    d = blocks_ref.shape[1]
    x = jnp.concatenate([blocks_ref[0, r] for r in range(d)], axis=0)
    if not exact:
        return _dot(pt_ref[...], x)
    return sum(_dot(pt_ref[...], part) for part in _split3(x))


def _tail_prompt_kernel(x_ref, o1_ref, l1_ref, o4_ref, l4_ref, o16_ref, l16_ref, ob_ref, pt4_ref, pt16_ref,
                        w_ref, g_ref, h_ref, xn_ref):
    o_g = [o1_ref[0].astype(F32), _unpermute(pt4_ref, o4_ref, False), _unpermute(pt16_ref, o16_ref, False)]
    l_g = [l1_ref[0], _unpermute(pt4_ref, l4_ref, True), _unpermute(pt16_ref, l16_ref, True)]
    m = jnp.maximum(jnp.maximum(l_g[0], l_g[1]), l_g[2])
    e_g = [jnp.exp(l - m) for l in l_g]
    den = e_g[0] + e_g[1] + e_g[2]
    w_g = [e / den for e in e_g]
    lane = lax.broadcasted_iota(jnp.int32, (x_ref.shape[1], LANES), 1)
    low = lane < HEAD_DIM
    parts = []
    for p in range(H_A // 2):
        acc = None
        for w, o in zip(w_g, o_g):
            wp = jnp.where(low, w[:, 2 * p:2 * p + 1], w[:, 2 * p + 1:2 * p + 2])
            term = wp * o[:, p * LANES:(p + 1) * LANES]
            acc = term if acc is None else acc + term
        parts.append(acc.astype(MXU_DTYPE))
    o = jnp.concatenate(parts + [ob_ref[0]], axis=-1)
    h = x_ref[0] + _dot(o, w_ref[...])
    h_ref[0] = h
    xn_ref[0] = _rms(h, g_ref[...])


def _tail_prompt(x, o1, l1, o4, l4, o16, l16, ob, w_out, g):
    n, s, dm = x.shape
    tb = TAIL_TB
    pt4 = jnp.asarray(_residue_perm(tb, 4).T, MXU_DTYPE)
    pt16 = jnp.asarray(_residue_perm(tb, 16).T, MXU_DTYPE)
    nat = lambda w: pl.BlockSpec((1, tb, w), lambda i, j: (i, j, 0))
    res = lambda d, w: pl.BlockSpec((1, d, tb // d, w), lambda i, j: (i, 0, j, 0))
    const = lambda a: pl.BlockSpec(a.shape, lambda i, j: (0,) * a.ndim)
    return pl.pallas_call(
        _tail_prompt_kernel,
        grid=(n, s // tb),
        in_specs=[nat(dm), nat(W_A), nat(LANES), res(4, W_A), res(4, LANES), res(16, W_A), res(16, LANES),
                  nat(W_BQ), const(pt4), const(pt16), const(w_out), const(g)],
        out_shape=[jax.ShapeDtypeStruct((n, s, dm), F32), jax.ShapeDtypeStruct((n, s, dm), F32)],
        out_specs=[nat(dm), nat(dm)],
        compiler_params=_cparams(("parallel", "parallel")),
        name="tail_prompt",
    )(x, o1, l1, o4, l4, o16, l16, ob, pt4, pt16, w_out, g)


def _tail_sample_kernel(x_ref, ot_ref, w_ref, g_ref, h_ref, xn_ref):
    h = x_ref[...] + _dot(ot_ref[...].T.astype(MXU_DTYPE), w_ref[...])
    h_ref[...] = h
    xn_ref[...] = _rms(h, g_ref[...])


def _tail_sample(x, o_t, w_out, g):
    return pl.pallas_call(
        _tail_sample_kernel,
        out_shape=[jax.ShapeDtypeStruct(x.shape, F32), jax.ShapeDtypeStruct(x.shape, F32)],
        compiler_params=_cparams(None),
        name="tail_sample",
    )(x, o_t, w_out, g)


def _sorting_network(n):
    size = 1 << (n - 1).bit_length()

    def merge(lo, hi, r):
        step = r * 2
        if step < hi - lo:
            yield from merge(lo, hi, step)
            yield from merge(lo + r, hi, step)
            yield from ((i, i + r) for i in range(lo + r, hi - r, step))
        else:
            yield (lo, lo + r)

    def sort(lo, hi):
        if hi - lo >= 1:
            mid = lo + (hi - lo) // 2
            yield from sort(lo, mid)
            yield from sort(mid + 1, hi)
            yield from merge(lo, hi, 1)

    return [(i, j) for i, j in sort(0, size - 1) if j < n]


def _top_rows(s, count):
    tiles = [s[v * SUBLANES:(v + 1) * SUBLANES] for v in range(s.shape[0] // SUBLANES)]
    for i, j in _sorting_network(len(tiles)):
        tiles[i], tiles[j] = jnp.maximum(tiles[i], tiles[j]), jnp.minimum(tiles[i], tiles[j])
    rows = []
    for t in range(count):
        m = jnp.max(tiles[0], axis=0, keepdims=True)
        rows.append(m)
        if t + 1 < count:
            popped = tiles[0] == m
            depth = min(len(tiles), count - t - 1)
            for k in range(depth):
                below = tiles[k + 1] if k + 1 < len(tiles) else NEG_INF
                tiles[k] = jnp.where(popped, below, tiles[k])
    return rows


def _pad_rows(rows, count):
    pad = [jnp.full_like(rows[0], NEG_INF)] * (count - len(rows))
    return jnp.concatenate(list(rows) + pad, axis=0)


def _candidate_sums(top1, top2):
    k = PEER_TOPK + 1
    wide = -(-k // SUBLANES) * SUBLANES
    narrow = -(-(k // 2) // SUBLANES) * SUBLANES
    assert k // (narrow + 1) <= 1
    v2_wide = _pad_rows(top2, wide)
    v2_narrow = v2_wide[:narrow]
    parts = [top1[0] + v2_wide] + [top1[a] + v2_narrow for a in range(1, narrow)]
    parts.append(_pad_rows(top1[narrow:], -(-(k - narrow) // SUBLANES) * SUBLANES) + top2[0])
    return jnp.concatenate(parts, axis=0)


def _route_kernel(x_ref, wqh_ref, wql_ref, skh_ref, skl_ref, thr_ref, p1_ref, p2_ref):
    xh, xl = _split2(x_ref[...])
    wqh = wqh_ref[...]
    q_t = _dot_nt(wqh, xh) + _dot_nt(wqh, xl) + _dot_nt(wql_ref[...], xh)
    half = D_KEY // 2
    for h in range(PEER_HEADS):
        scores = []
        for c in range(2):
            qh, ql = _split2(q_t[h * D_KEY + c * half:h * D_KEY + (c + 1) * half, :])
            scores.append(_dot(skh_ref[c], qh) + _dot(skh_ref[c], ql) + _dot(skl_ref[c], qh))
        for g in range(q_t.shape[1] // LANES):
            s1, s2 = (s[:, g * LANES:(g + 1) * LANES] for s in scores)
            top1, top2 = _top_rows(s1, PEER_TOPK + 1), _top_rows(s2, PEER_TOPK + 1)
            best = _top_rows(_candidate_sums(top1, top2), PEER_TOPK + 1)
            tau = 0.5 * (best[PEER_TOPK - 1] + best[PEER_TOPK])
            z = sum(jnp.exp(b - best[0]) for b in best[:PEER_TOPK])
            log_norm = best[0] + jnp.log(z)
            m2 = top2[0]
            thr_ref[h, g] = jnp.exp((tau - m2) - s1)
            p1_ref[h, g] = jnp.exp(s1 + (m2 - log_norm))
            p2_ref[h, g] = jnp.exp(s2 - m2)


def _route(xn, wq_t_hi, wq_t_lo, sk_hi, sk_lo, tb):
    t, dm = xn.shape
    const = lambda a: pl.BlockSpec(a.shape, lambda i: (0,) * a.ndim)
    gspec = pl.BlockSpec((PEER_HEADS, tb // LANES, N_KEYS, LANES), lambda i: (0, i, 0, 0))
    gshape = jax.ShapeDtypeStruct((PEER_HEADS, t // LANES, N_KEYS, LANES), F32)
    return pl.pallas_call(
        _route_kernel,
        grid=(t // tb,),
        in_specs=[pl.BlockSpec((tb, dm), lambda i: (i, 0)), const(wq_t_hi), const(wq_t_lo), const(sk_hi), const(sk_lo)],
        out_shape=[gshape] * 3,
        out_specs=[gspec] * 3,
        compiler_params=_cparams(("parallel",)),
        name="peer_route",
    )(xn, wq_t_hi, wq_t_lo, sk_hi, sk_lo)


EXPERT_CHUNK = SUBLANES * N_KEYS
GATE_JBLOCK = 2 * SUBLANES
INV_SQRT2 = 0.7071067811865476


def _gate_rows(thr_ref, p1_ref, p2_ref, w_ref):
    nv = GATE_JBLOCK // SUBLANES

    def row_group(i8, carry):
        base = pl.multiple_of(i8 * SUBLANES, SUBLANES)
        thr_t = [thr_ref[h, 0, pl.ds(base, SUBLANES), :] for h in range(PEER_HEADS)]
        p1_t = [p1_ref[h, 0, pl.ds(base, SUBLANES), :] for h in range(PEER_HEADS)]
        for jb in range(N_KEYS // GATE_JBLOCK):
            w = [[None] * nv for _ in range(SUBLANES)]
            for h in range(PEER_HEADS):
                p2 = [p2_ref[h, 0, pl.ds(jb * GATE_JBLOCK + v * SUBLANES, SUBLANES), :] for v in range(nv)]
                for r in range(SUBLANES):
                    thr = jnp.broadcast_to(thr_t[h][r:r + 1, :], (SUBLANES, LANES))
                    p1 = jnp.broadcast_to(p1_t[h][r:r + 1, :], (SUBLANES, LANES))
                    for v in range(nv):
                        term = jnp.where(p2[v] >= thr, p1 * p2[v], 0.0)
                        w[r][v] = term if w[r][v] is None else w[r][v] + term
            for r in range(SUBLANES):
                row0 = pl.multiple_of((base + r) * N_KEYS + jb * GATE_JBLOCK, GATE_JBLOCK)
                w_ref[0, pl.ds(row0, GATE_JBLOCK), :] = jnp.concatenate(w[r], axis=0).astype(w_ref.dtype)
        return carry

    lax.fori_loop(0, thr_ref.shape[2] // SUBLANES, row_group, 0)


GATE_ROWS = N_KEYS // 2


def _gate_specs(t):
    rows = pl.BlockSpec((PEER_HEADS, 1, GATE_ROWS, LANES), lambda i, j: (0, i, j, 0))
    full = pl.BlockSpec((PEER_HEADS, 1, N_KEYS, LANES), lambda i, j: (0, i, 0, 0))
    out = pl.BlockSpec((1, GATE_ROWS * N_KEYS, LANES), lambda i, j: (i, j, 0))
    shape = jax.ShapeDtypeStruct((t // LANES, N_KEYS * N_KEYS, LANES), MXU_DTYPE)
    return [rows, rows, full], out, shape


def _gates(thr, p1, p2):
    t = thr.shape[1] * LANES
    in_specs, out_spec, shape = _gate_specs(t)
    return pl.pallas_call(
        _gate_rows,
        grid=(t // LANES, N_KEYS // GATE_ROWS),
        in_specs=in_specs, out_specs=out_spec, out_shape=shape,
        compiler_params=_cparams(("parallel", "parallel")),
        name="peer_gates",
    )(thr, p1, p2)


def _sample_and_gates_kernel(*refs):
    n_in, n_gate = 11, 3
    sample_in, gate_in = refs[:n_in], refs[n_in:n_in + n_gate]
    sample_out, gate_out = refs[n_in + n_gate:-1], refs[-1]
    _sample_kernel(*sample_in, *sample_out)
    _gate_rows(*gate_in, gate_out)


def _sample_and_gates(sample_args, thr, p1, p2):
    t = thr.shape[1] * LANES
    ns, nkv = sample_args[7].shape[0], sample_args[9].shape[1]
    s_in, s_out, s_shapes = _sample_specs(*sample_args)
    if t // LANES != ns or N_KEYS // GATE_ROWS != nkv:
        outs = pl.pallas_call(
            _sample_kernel, grid=(ns, nkv), in_specs=s_in, out_specs=s_out, out_shape=s_shapes,
            compiler_params=_cparams(("arbitrary", "arbitrary")), name="sample_attn",
        )(*sample_args)
        return list(outs) + [_gates(thr, p1, p2)]
    g_in, g_out, g_shape = _gate_specs(t)
    return pl.pallas_call(
        _sample_and_gates_kernel,
        grid=(ns, nkv),
        in_specs=s_in + g_in, out_specs=s_out + [g_out], out_shape=s_shapes + [g_shape],
        compiler_params=_cparams(("arbitrary", "arbitrary")),
        name="sample_attn_and_gates",
    )(*sample_args, thr, p1, p2)


def _activate(g, a_ref, p_ref, w_ref):
    blk = 8 * SUBLANES
    for b in range(a_ref.shape[1] // blk):
        rows = pl.ds(b * blk, blk)
        a = a_ref[g, rows, :]
        half = 0.5 * a
        act = half + half * lax.erf(a * INV_SQRT2)
        p_ref[g, rows, :] = w_ref[g, rows, :] * act.astype(p_ref.dtype)


def _experts_kernel(xn_ref, res_ref, g_ref, down_ref, upt_ref, w_a_ref, w_b_ref,
                    y_ref, xb_ref, a0_ref, a1_ref, pb0_ref, pb1_ref, acc_ref):
    i, j = pl.program_id(0), pl.program_id(1)
    ch = EXPERT_CHUNK
    ng = a0_ref.shape[0]

    @pl.when((i == 0) & (j == 0))
    def _():
        for ref in (a0_ref, a1_ref, pb0_ref, pb1_ref):
            ref[...] = jnp.zeros_like(ref)

    @pl.when(j == 0)
    def _():
        xb_ref[...] = xn_ref[...].astype(MXU_DTYPE)

    keep = j >= 1

    def tick(half, a_in_ref, a_out_ref, p_out_ref, p_in_ref, w_ref):
        p_in = jnp.concatenate([p_in_ref[gg] for gg in range(ng)], axis=1)
        upd = _dot(upt_ref[:, half * ch:(half + 1) * ch], p_in)
        nxt = _dot_nt(down_ref[pl.ds(half * ch, ch), :], xb_ref[...])
        for g in range(ng):
            a_out_ref[g] = nxt[:, g * LANES:(g + 1) * LANES]
            _activate(g, a_in_ref, p_out_ref, w_ref)
        return upd

    upd = tick(0, a1_ref, a0_ref, pb1_ref, pb0_ref, w_a_ref)
    upd = upd + tick(1, a0_ref, a1_ref, pb0_ref, pb1_ref, w_b_ref)
    acc_ref[...] = jnp.where(keep, acc_ref[...] + upd, 0.0)

    @pl.when(j == pl.num_programs(1) - 1)
    def _():
        y = res_ref[...] + acc_ref[...].T
        y_ref[...] = _rms(y, g_ref[...])


def _experts(xn, res, g, down, up_t, w, tb):
    t, dm = xn.shape
    ne = down.shape[0]
    step = 2 * EXPERT_CHUNK
    nj = ne // step
    ng = tb // LANES
    tok = pl.BlockSpec((tb, dm), lambda i, j: (i, 0))
    w_a = pl.BlockSpec((ng, EXPERT_CHUNK, LANES), lambda i, j: (i, jnp.maximum(2 * j - 1, 0), 0))
    w_b = pl.BlockSpec((ng, EXPERT_CHUNK, LANES), lambda i, j: (i, jnp.minimum(2 * j, 2 * nj - 1), 0))
    return pl.pallas_call(
        _experts_kernel,
        grid=(t // tb, nj + 1),
        in_specs=[tok, tok, pl.BlockSpec((1, dm), lambda i, j: (0, 0)),
                  pl.BlockSpec((step, dm), lambda i, j: (jnp.minimum(j, nj - 1), 0)),
                  pl.BlockSpec((dm, step), lambda i, j: (0, jnp.maximum(j - 1, 0))),
                  w_a, w_b],
        out_shape=jax.ShapeDtypeStruct((t, dm), F32),
        out_specs=tok,
        scratch_shapes=[pltpu.VMEM((tb, dm), MXU_DTYPE),
                        pltpu.VMEM((ng, EXPERT_CHUNK, LANES), F32), pltpu.VMEM((ng, EXPERT_CHUNK, LANES), F32),
                        pltpu.VMEM((ng, EXPERT_CHUNK, LANES), MXU_DTYPE),
                        pltpu.VMEM((ng, EXPERT_CHUNK, LANES), MXU_DTYPE),
                        pltpu.VMEM((dm, tb), F32)],
        compiler_params=_cparams(("arbitrary", "arbitrary")),
        name="peer_experts",
    )(xn, res, g, down, up_t, w, w)


PEER_TB = 512


def _cache_to_feature_major(c):
    return jnp.transpose(c, (0, 2, 3, 1))


def _cache_from_feature_major(c):
    return jnp.transpose(c, (0, 3, 1, 2))


def _layer(xp, xs, cache_a_k, cache_a_v, cache_b_k, cache_b_v, norm_attn, w_in, rel_bias, sinks, w_out, norm_ffn,
           w_peer_q, peer_sub_keys, peer_down, peer_up, g_final):
    n, s, dm = xp.shape
    ns = xs.shape[0]
    la, lb = cache_a_k.shape[1], cache_b_k.shape[1]

    c3 = 3 * W_A
    w_q_scaled = jnp.concatenate([w_in[:, :W_A] * SCALE, w_in[:, W_A:c3], w_in[:, c3:c3 + W_BQ] * SCALE,
                                  w_in[:, c3 + W_BQ:]], axis=1)
    qb_cols = w_q_scaled[:, c3:c3 + W_BQ].reshape(dm, H_B // 2, 2, HEAD_DIM)[:, :, ::-1].reshape(dm, W_BQ)
    vb_cols = w_in[:, c3 + W_BQ + W_BKV:].reshape(dm, KV_B, HEAD_DIM)[:, ::-1].reshape(dm, W_BKV)
    w_ext = jnp.concatenate([w_q_scaled, qb_cols, vb_cols], axis=1).astype(MXU_DTYPE)
    w_nat_t = w_q_scaled.T.astype(MXU_DTYPE)
    w_out_b = w_out.astype(MXU_DTYPE)
    g_attn, g_ffn = norm_attn[None, :], norm_ffn[None, :]
    wq_t = w_peer_q.T
    wq_t_hi = wq_t.astype(MXU_DTYPE)
    wq_t_lo = (wq_t - wq_t_hi.astype(F32)).astype(MXU_DTYPE)
    sk_hi = peer_sub_keys.astype(MXU_DTYPE)
    sk_lo = (peer_sub_keys - sk_hi.astype(F32)).astype(MXU_DTYPE)
    peer_w = (wq_t_hi, wq_t_lo, sk_hi, sk_lo, peer_down.astype(MXU_DTYPE), peer_up.T.astype(MXU_DTYPE))

    tabs_a = [_bias_table(_window_bucket_matrix(d), rel_bias, 0, H_A) for d in DILATIONS]
    tab_b = _bias_table(_window_bucket_matrix(1), rel_bias, H_A, H_B)
    bidx_sa, mult_sa = _sample_tables(la)
    bidx_sb, _ = _sample_tables(lb)
    tab_sa = _bias_table(bidx_sa, rel_bias, 0, H_A)
    tab_sb = _bias_table(bidx_sb, rel_bias, H_A, H_B)
    bias0_a, bias0_b = rel_bias[0, :H_A], rel_bias[0, H_A:]

    (q1, k1, v1, kf, vf, q4, k4, v4, q16, k16, v16, qb, qbs, kb, vb, vbs, kbf, vbf) = _qkv_prompt(xp, g_attn, w_ext)
    o1, l1 = _swa(tabs_a[0], [q1], k1, [v1], HEAD_CFG_A)
    flat = lambda a: a.reshape((a.shape[0] * a.shape[1],) + a.shape[2:])
    o4, l4 = _swa(tabs_a[1], [flat(q4)], flat(k4), [flat(v4)], HEAD_CFG_A)
    o16, l16 = _swa(tabs_a[2], [flat(q16)], flat(k16), [flat(v16)], HEAD_CFG_A)
    ob = _swa(tab_b, [qb, qbs], kb, [vb, vbs], HEAD_CFG_B, sinks=sinks, want_lse=False)
    unflat = lambda a, d: a.reshape((n, d) + a.shape[1:])
    hp, xnp = _tail_prompt(xp, o1, l1, unflat(o4, 4), unflat(l4, 4), unflat(o16, 16), unflat(l16, 16), ob,
                           w_out_b, g_ffn)
    prompt_caches = (kf[:, s - min(la, s):].reshape(n, -1, H_A, HEAD_DIM),
                     vf[:, s - min(la, s):].reshape(n, -1, H_A, HEAD_DIM),
                     kbf[:, s - min(lb, s):].reshape(n, -1, KV_B, HEAD_DIM),
                     vbf[:, s - min(lb, s):].reshape(n, -1, KV_B, HEAD_DIM))

    hp, xnp = hp.reshape(n * s, dm), xnp.reshape(n * s, dm)
    route_w, (down_b, up_t_b) = peer_w[:4], peer_w[4:]
    routing_p = _route(xnp, *route_w, PEER_TB)

    hs_t = _qkv_sample(xs, g_attn, w_nat_t)
    sample_args = (bias0_a, bias0_b, sinks, tab_sa, jnp.asarray(mult_sa), tab_sb, hs_t,
                   _cache_to_feature_major(cache_a_k), _cache_to_feature_major(cache_a_v),
                   _cache_to_feature_major(cache_b_k), _cache_to_feature_major(cache_b_v))
    o_t, *shifted, gates_p = _sample_and_gates(sample_args, *routing_p)
    h_s, xn_s = _tail_sample(xs, o_t, w_out_b, g_ffn)
    sample_caches = tuple(_cache_from_feature_major(c) for c in shifted)

    y_p = _experts(xnp, hp, g_final, down_b, up_t_b, gates_p, PEER_TB)
    gates_s = _gates(*_route(xn_s, *route_w, LANES))
    y_s = _experts(xn_s, h_s, g_final, down_b, up_t_b, gates_s, LANES)
    return y_p.reshape(n, s, dm), y_s, prompt_caches, sample_caches


def kernel(x_prompt, x_sample, cache_a_k, cache_a_v, cache_b_k, cache_b_v, norm_attn, w_in, rel_bias, sinks, w_out,
           norm_ffn, w_peer_q, peer_sub_keys, peer_down, peer_up, norm_final):
    depth = w_in.shape[0]
    assert depth == 1, "single-layer trunk"
    n, s, dm = x_prompt.shape
    ns = x_sample.shape[0]
    assert x_sample.shape[1] == 1 and s % QKV_TB == 0 and (n * s) % PEER_TB == 0 and ns % LANES == 0
    l = 0
    y_prompt, y_sample, prompt_caches, sample_caches = _layer(
        x_prompt, x_sample[:, 0], cache_a_k[l], cache_a_v[l], cache_b_k[l], cache_b_v[l], norm_attn[l], w_in[l],
        rel_bias, sinks[l], w_out[l], norm_ffn[l], w_peer_q[l], peer_sub_keys[l], peer_down[l], peer_up[l],
        norm_final[None, :])
    return ((y_prompt, y_sample.reshape(ns, 1, dm)) + tuple(c[None] for c in prompt_caches)
            + tuple(c[None] for c in sample_caches))
```

```python
import functools
import math

import numpy as np
import jax
import jax.numpy as jnp
from jax import lax
from jax.experimental import pallas as pl
from jax.experimental.pallas import tpu as pltpu

HEAD_DIM = 64
H_A = 8
H_B = 8
KV_B = 2
G_B = H_B // KV_B
DILATIONS = (1, 4, 16)
SLOTS = 128
N_BUCKETS = 32
MAX_DISTANCE = 2048
N_KEYS = 128
PEER_HEADS = 8
PEER_TOPK = 16
D_KEY = 128
RMS_EPS = 1e-6
SCALE = HEAD_DIM ** -0.5
W_A = H_A * HEAD_DIM
W_BQ = H_B * HEAD_DIM
W_BKV = KV_B * HEAD_DIM

LANES = 128
SUBLANES = 8
VMEM_LIMIT_BYTES = 56 * 1024 * 1024

MXU_DTYPE = jnp.bfloat16
F32 = jnp.float32
NEG_INF = float("-inf")


def _cparams(sem, flags=None):
    return pltpu.CompilerParams(dimension_semantics=sem, vmem_limit_bytes=VMEM_LIMIT_BYTES, flags=flags)


def _t5_bucket_np(dist):
    exact = N_BUCKETS // 2
    d = np.maximum(dist, 0)
    logd = np.log(np.maximum(d, 1).astype(np.float32) / np.float32(exact)) / np.float32(math.log(MAX_DISTANCE / exact))
    large = np.minimum(exact + (logd * np.float32(N_BUCKETS - exact)).astype(np.int32), N_BUCKETS - 1)
    return np.where(d < exact, d, large).astype(np.int32)


def _window_bucket_matrix(dilation):
    i = np.arange(SLOTS)[:, None]
    j = np.arange(2 * SLOTS)[None, :]
    dist = i - j + SLOTS
    ok = (dist >= 0) & (dist <= SLOTS)
    return np.where(ok, _t5_bucket_np(dist * dilation), -1).astype(np.int32)


def _sample_tables(cache_len):
    delta = cache_len - np.arange(cache_len)
    mult = np.zeros(cache_len, np.float32)
    for d in DILATIONS:
        mult += ((delta % d == 0) & (delta <= SLOTS * d)).astype(np.float32)
    return _t5_bucket_np(delta)[None, :], mult[None, :]


def _residue_perm(tb, d):
    p = np.zeros((tb, tb), np.float32)
    rows = np.arange(tb)
    p[rows, (rows % (tb // d)) * d + rows // (tb // d)] = 1.0
    return p


def _rms(x, g):
    return x * lax.rsqrt(jnp.mean(x * x, axis=-1, keepdims=True) + RMS_EPS) * g


def _dot(a, b):
    return jnp.dot(a, b, preferred_element_type=F32)


def _dot_nt(a, b):
    return lax.dot_general(a, b, (((1,), (1,)), ((), ())), preferred_element_type=F32)


def _split2(x):
    hi = x.astype(MXU_DTYPE)
    lo = (x - hi.astype(F32)).astype(MXU_DTYPE)
    return hi, lo


def _split3(x):
    hi = x.astype(MXU_DTYPE)
    r1 = x - hi.astype(F32)
    mid = r1.astype(MXU_DTYPE)
    lo = (r1 - mid.astype(F32)).astype(MXU_DTYPE)
    return hi, mid, lo


def _bias_kernel(col0s, nheads, rb_ref, *refs):
    n = len(col0s)
    for col0, bidx_ref, o_ref in zip(col0s, refs[:n], refs[n:]):
        b = bidx_ref[...]
        for h in range(nheads):
            val = jnp.full(b.shape, NEG_INF, F32)
            for k in range(N_BUCKETS):
                val = jnp.where(b == k, rb_ref[k, col0 + h], val)
            o_ref[h] = val


def _bias_tables(rel_bias, nheads, tables):
    bidx = [jnp.asarray(b) for b, _ in tables]
    return pl.pallas_call(
        functools.partial(_bias_kernel, tuple(c for _, c in tables), nheads),
        out_shape=[jax.ShapeDtypeStruct((nheads,) + b.shape, F32) for b in bidx],
        in_specs=[pl.BlockSpec(memory_space=pltpu.SMEM)] + [pl.BlockSpec(b.shape, lambda: (0, 0)) for b in bidx],
        out_specs=[pl.BlockSpec((nheads,) + b.shape, lambda: (0, 0, 0)) for b in bidx],
        name="bias_tables",
    )(rel_bias, *bidx)


QKV_TB = 512


def _qkv_prompt_kernel(x_ref, g_ref, w_ref, p4_ref, p16_ref,
                       q1_ref, k1_ref, v1_ref, kf_ref, vf_ref,
                       q4_ref, k4_ref, v4_ref, q16_ref, k16_ref, v16_ref,
                       qb_ref, qbs_ref, kb_ref, vb_ref, vbs_ref, kbf_ref, vbf_ref):
    xn = _rms(x_ref[0], g_ref[...])
    h = _dot(xn.astype(MXU_DTYPE), w_ref[...])
    c0, c1, c2, c3 = 0, W_A, 2 * W_A, 3 * W_A
    c4, c5, c6 = c3 + W_BQ, c3 + W_BQ + W_BKV, c3 + W_BQ + 2 * W_BKV
    c7 = c6 + W_BQ
    ha = h[:, :c3].astype(MXU_DTYPE)
    q1_ref[0] = ha[:, c0:c1]
    k1_ref[0] = ha[:, c1:c2]
    v1_ref[0] = ha[:, c2:c3]
    kf_ref[0] = h[:, c1:c2]
    vf_ref[0] = h[:, c2:c3]
    qb_ref[0] = h[:, c3:c4].astype(MXU_DTYPE)
    kb_ref[0] = h[:, c4:c5].astype(MXU_DTYPE)
    vb_ref[0] = h[:, c5:c6].astype(MXU_DTYPE)
    kbf_ref[0] = h[:, c4:c5]
    vbf_ref[0] = h[:, c5:c6]
    qbs_ref[0] = h[:, c6:c7].astype(MXU_DTYPE)
    vbs_ref[0] = h[:, c7:].astype(MXU_DTYPE)
    tb = ha.shape[0]
    for d, p_ref, outs in ((4, p4_ref, (q4_ref, k4_ref, v4_ref)), (16, p16_ref, (q16_ref, k16_ref, v16_ref))):
        perm = _dot(p_ref[...], ha).astype(MXU_DTYPE)
        rows = tb // d
        for r in range(d):
            for t, o_ref in enumerate(outs):
                o_ref[0, r] = perm[r * rows:(r + 1) * rows, t * W_A:(t + 1) * W_A]


def _qkv_prompt(x, g, w_ext):
    n, s, dm = x.shape
    tb = QKV_TB
    nb = s // tb
    p4 = jnp.asarray(_residue_perm(tb, 4), MXU_DTYPE)
    p16 = jnp.asarray(_residue_perm(tb, 16), MXU_DTYPE)
    bf = MXU_DTYPE

    def nat(width, dt):
        return jax.ShapeDtypeStruct((n, s, width), dt), pl.BlockSpec((1, tb, width), lambda i, j: (i, j, 0))

    def res(d):
        return (jax.ShapeDtypeStruct((n, d, s // d, W_A), bf),
                pl.BlockSpec((1, d, tb // d, W_A), lambda i, j: (i, 0, j, 0)))

    outs = [nat(W_A, bf), nat(W_A, bf), nat(W_A, bf), nat(W_A, F32), nat(W_A, F32),
            res(4), res(4), res(4), res(16), res(16), res(16),
            nat(W_BQ, bf), nat(W_BQ, bf), nat(W_BKV, bf), nat(W_BKV, bf), nat(W_BKV, bf),
            nat(W_BKV, F32), nat(W_BKV, F32)]
    return pl.pallas_call(
        _qkv_prompt_kernel,
        grid=(n, nb),
        in_specs=[pl.BlockSpec((1, tb, dm), lambda i, j: (i, j, 0)),
                  pl.BlockSpec((1, dm), lambda i, j: (0, 0)),
                  pl.BlockSpec(w_ext.shape, lambda i, j: (0, 0)),
                  pl.BlockSpec((tb, tb), lambda i, j: (0, 0)),
                  pl.BlockSpec((tb, tb), lambda i, j: (0, 0))],
        out_shape=[o[0] for o in outs],
        out_specs=[o[1] for o in outs],
        compiler_params=_cparams(("parallel", "parallel")),
        name="qkv_prompt",
    )(x, g, w_ext, p4, p16)


def _qkv_sample_kernel(x_ref, g_ref, wt_ref, o_ref):
    xn = _rms(x_ref[...], g_ref[...])
    o_ref[...] = _dot_nt(wt_ref[...], xn.astype(MXU_DTYPE))


def _qkv_sample(x, g, w_t):
    return pl.pallas_call(
        _qkv_sample_kernel,
        out_shape=jax.ShapeDtypeStruct((w_t.shape[0], x.shape[0]), F32),
        compiler_params=_cparams(None),
        name="qkv_sample",
    )(x, g, w_t)


SWA_TILES = 4


def _swa_kernel(head_cfg, has_sink, want_lse, *refs):
    it = iter(refs)
    tab_ref = next(it)
    sink_ref = next(it) if has_sink else None
    q_refs = [next(it)]
    if any(c[0] == 1 for c in head_cfg):
        q_refs.append(next(it))
    kc_ref, kp_ref = next(it), next(it)
    v_refs = [(next(it), next(it))]
    if any(c[3] == 1 for c in head_cfg):
        v_refs.append((next(it), next(it)))
    o_ref = next(it)
    lse_ref = next(it) if want_lse else None

    first = pl.program_id(1) == 0
    col = lax.broadcasted_iota(jnp.int32, (SLOTS, 2 * SLOTS), 1)
    prev_pen = jnp.where((col < SLOTS) & first, NEG_INF, 0.0)
    lane = lax.broadcasted_iota(jnp.int32, (SLOTS, LANES), 1)
    low = lane < HEAD_DIM
    for t in range(kc_ref.shape[1] // SLOTS):
        rows = slice(t * SLOTS, (t + 1) * SLOTS)
        before = slice((t - 1) * SLOTS, t * SLOTS)

        def keys(cur_ref, prev_ref, lanes):
            prev = prev_ref[0, :, lanes] if t == 0 else cur_ref[0, before, lanes]
            return jnp.concatenate([prev, cur_ref[0, rows, lanes]], axis=0)

        lse_acc = jnp.zeros((SLOTS, LANES), F32)
        for p in range(len(head_cfg) // 2):
            halves = []
            for hh in range(2):
                h = 2 * p + hh
                q_src, q_half, k_tile, v_src, v_tile = head_cfg[h]
                q = q_refs[q_src][0, rows, p * LANES:(p + 1) * LANES]
                q = jnp.where(low if q_half == 0 else jnp.logical_not(low), q, jnp.zeros_like(q))
                kcat = keys(kc_ref, kp_ref, slice(k_tile * LANES, (k_tile + 1) * LANES))
                s = _dot_nt(q, kcat) + tab_ref[h]
                if t == 0:
                    s = s + prev_pen
                m = jnp.max(s, axis=-1, keepdims=True)
                if has_sink:
                    m = jnp.maximum(m, sink_ref[h])
                e = jnp.exp(s - m)
                l = jnp.sum(e, axis=-1, keepdims=True)
                if has_sink:
                    l = l + jnp.exp(sink_ref[h] - m)
                vc_ref, vp_ref = v_refs[v_src]
                vcat = keys(vc_ref, vp_ref, slice(v_tile * LANES, (v_tile + 1) * LANES))
                halves.append(_dot(e.astype(MXU_DTYPE), vcat) / l)
                if want_lse:
                    lse_acc = jnp.where(lane == h, m + jnp.log(l), lse_acc)
            o_ref[0, rows, p * LANES:(p + 1) * LANES] = jnp.where(low, halves[0], halves[1]).astype(o_ref.dtype)
        if want_lse:
            lse_ref[0, rows] = lse_acc


def _swa(tab, q_list, k, v_list, head_cfg, sinks=None, want_lse=True):
    r, s, _ = q_list[0].shape
    ck = k.shape[-1]
    tiles = math.gcd(SWA_TILES, s // SLOTS)
    rows = tiles * SLOTS
    assert s % rows == 0
    nb = s // rows
    cur = lambda i, j: (i, j, 0)
    prev = lambda i, j: (i, jnp.maximum(tiles * j - 1, 0), 0)
    in_specs = [pl.BlockSpec(tab.shape, lambda i, j: (0, 0, 0))]
    args = [tab]
    if sinks is not None:
        in_specs.append(pl.BlockSpec(memory_space=pltpu.SMEM))
        args.append(sinks)
    for q in q_list:
        in_specs.append(pl.BlockSpec((1, rows, q.shape[-1]), cur))
        args.append(q)
    in_specs += [pl.BlockSpec((1, rows, ck), cur), pl.BlockSpec((1, SLOTS, ck), prev)]
    args += [k, k]
    for v in v_list:
        in_specs += [pl.BlockSpec((1, rows, ck), cur), pl.BlockSpec((1, SLOTS, ck), prev)]
        args += [v, v]
    out_shape = [jax.ShapeDtypeStruct((r, s, W_A), MXU_DTYPE)]
    out_specs = [pl.BlockSpec((1, rows, W_A), cur)]
    if want_lse:
        out_shape.append(jax.ShapeDtypeStruct((r, s, LANES), F32))
        out_specs.append(pl.BlockSpec((1, rows, LANES), cur))
    out = pl.pallas_call(
        functools.partial(_swa_kernel, tuple(head_cfg), sinks is not None, want_lse),
        grid=(r, nb),
        in_specs=in_specs,
        out_shape=out_shape,
        out_specs=out_specs,
        compiler_params=_cparams(("parallel", "arbitrary")),
        name="swa",
    )(*args)
    return out if want_lse else out[0]


HEAD_CFG_A = tuple((0, h % 2, h // 2, 0, h // 2) for h in range(H_A))


def _head_cfg_b():
    cfg = []
    for h in range(H_B):
        c = h // G_B
        src = 0 if h % 2 == c else 1
        cfg.append((src, c, 0, src, 0))
    return tuple(cfg)


HEAD_CFG_B = _head_cfg_b()


def _shift_in(x, new_col):
    length = x.shape[-1]
    rolled = pltpu.roll(x, length - 1, axis=1)
    lane = lax.broadcasted_iota(jnp.int32, x.shape, 1)
    return jnp.where(lane == length - 1, new_col, rolled)


def _col_attention(q, kmat, vmat, k_new, v_new, bias_row, mult_row, bias_new, mult_new, sink):
    s = jnp.sum(q * kmat, axis=0, keepdims=True) + bias_row
    s_new = jnp.sum(q * k_new, axis=0, keepdims=True) + bias_new
    if mult_row is not None:
        s = jnp.where(mult_row > 0.0, s, NEG_INF)
    m = jnp.maximum(jnp.max(s, axis=-1, keepdims=True), s_new)
    if sink is not None:
        m = jnp.maximum(m, sink)
    e = jnp.exp(s - m)
    if mult_row is not None:
        e = e * mult_row
    e_new = mult_new * jnp.exp(s_new - m)
    l = jnp.sum(e, axis=-1, keepdims=True) + e_new
    if sink is not None:
        l = l + jnp.exp(sink - m)
    o = jnp.sum(vmat * e, axis=-1, keepdims=True) + v_new * e_new
    return o / l


SAMPLE_HEADS = H_A // KV_B
assert SAMPLE_HEADS == G_B


def _column(ht_ref, row0, nrows, pick):
    return jnp.sum(jnp.where(pick, ht_ref[pl.ds(row0, nrows), :], 0.0), axis=1, keepdims=True)


def _sample_kernel(bias0a_ref, bias0b_ref, sink_ref, taba_ref, mult_ref, tabb_ref, ht_ref,
                   ka_ref, va_ref, kb_ref, vb_ref, ot_ref, kao_ref, vao_ref, kbo_ref, vbo_ref):
    n, hb = pl.program_id(0), pl.program_id(1)
    lane = n % LANES
    pick = lax.broadcasted_iota(jnp.int32, (1, LANES), 1) == lane
    rows = SAMPLE_HEADS * HEAD_DIM
    blk = pl.multiple_of(hb * rows, rows)
    kvr = pl.multiple_of(hb * HEAD_DIM, HEAD_DIM)

    @pl.when((lane == 0) & (hb == 0))
    def _():
        ot_ref[...] = jnp.zeros_like(ot_ref)

    def put(row0, col):
        ot_ref[pl.ds(row0, rows), :] = jnp.where(pick, col, ot_ref[pl.ds(row0, rows), :])

    q, k_new, v_new = (_column(ht_ref, c0 + blk, rows, pick) for c0 in (0, W_A, 2 * W_A))
    mult = mult_ref[...]
    outs = []
    for hl in range(SAMPLE_HEADS):
        r = slice(hl * HEAD_DIM, (hl + 1) * HEAD_DIM)
        kmat, vmat = ka_ref[0, hl], va_ref[0, hl]
        outs.append(_col_attention(q[r], kmat, vmat, k_new[r], v_new[r], taba_ref[hl], mult,
                                   bias0a_ref[hb * SAMPLE_HEADS + hl], float(len(DILATIONS)), None))
        kao_ref[0, hl] = _shift_in(kmat, k_new[r])
        vao_ref[0, hl] = _shift_in(vmat, v_new[r])
    put(blk, jnp.concatenate(outs, axis=0))

    c3 = 3 * W_A
    q = _column(ht_ref, c3 + blk, rows, pick)
    k_new = _column(ht_ref, c3 + W_BQ + kvr, HEAD_DIM, pick)
    v_new = _column(ht_ref, c3 + W_BQ + W_BKV + kvr, HEAD_DIM, pick)
    kmat, vmat = kb_ref[0, 0], vb_ref[0, 0]
    outs = []
    for g in range(G_B):
        h = hb * G_B + g
        outs.append(_col_attention(q[g * HEAD_DIM:(g + 1) * HEAD_DIM], kmat, vmat, k_new, v_new, tabb_ref[g], None,
                                   bias0b_ref[h], 1.0, sink_ref[h]))
    put(W_A + blk, jnp.concatenate(outs, axis=0))
    kbo_ref[0, 0] = _shift_in(kmat, k_new)
    vbo_ref[0, 0] = _shift_in(vmat, v_new)


def _sample_specs(bias0_a, bias0_b, sinks, tab_a, mult_a, tab_b, h_t, ka_t, va_t, kb_t, vb_t):
    ns, nh, hd, la = ka_t.shape
    _, nkv, _, lb = kb_t.shape
    assert nh // SAMPLE_HEADS == nkv and ns % LANES == 0
    smem = pl.BlockSpec(memory_space=pltpu.SMEM)
    lanes_of = lambda rows: pl.BlockSpec((rows, LANES), lambda i, j: (0, i // LANES))
    a_spec = pl.BlockSpec((1, SAMPLE_HEADS, hd, la), lambda i, j: (i, j, 0, 0))
    b_spec = pl.BlockSpec((1, 1, hd, lb), lambda i, j: (i, j, 0, 0))
    in_specs = [smem, smem, smem,
                pl.BlockSpec((SAMPLE_HEADS, 1, la), lambda i, j: (j, 0, 0)),
                pl.BlockSpec((1, la), lambda i, j: (0, 0)),
                pl.BlockSpec((G_B, 1, lb), lambda i, j: (j, 0, 0)),
                lanes_of(h_t.shape[0]), a_spec, a_spec, b_spec, b_spec]
    out_specs = [lanes_of(W_A + W_BQ), a_spec, a_spec, b_spec, b_spec]
    shapes = [jax.ShapeDtypeStruct((W_A + W_BQ, ns), F32)] + [jax.ShapeDtypeStruct(c.shape, F32)
                                                              for c in (ka_t, va_t, kb_t, vb_t)]
    return in_specs, out_specs, shapes


TAIL_TB = 512


def _unpermute(pt_ref, blocks_ref, exact):
    d = blocks_ref.shape[1]
    x = jnp.concatenate([blocks_ref[0, r] for r in range(d)], axis=0)
    if not exact:
        return _dot(pt_ref[...], x)
    return sum(_dot(pt_ref[...], part) for part in _split3(x))


def _tail_prompt_kernel(x_ref, o1_ref, l1_ref, o4_ref, l4_ref, o16_ref, l16_ref, ob_ref, pt4_ref, pt16_ref,
                        w_ref, g_ref, h_ref, xn_ref):
    o_g = [o1_ref[0].astype(F32), _unpermute(pt4_ref, o4_ref, False), _unpermute(pt16_ref, o16_ref, False)]
    l_g = [l1_ref[0], _unpermute(pt4_ref, l4_ref, True), _unpermute(pt16_ref, l16_ref, True)]
    m = jnp.maximum(jnp.maximum(l_g[0], l_g[1]), l_g[2])
    e_g = [jnp.exp(l - m) for l in l_g]
    den = e_g[0] + e_g[1] + e_g[2]
    w_g = [e / den for e in e_g]
    lane = lax.broadcasted_iota(jnp.int32, (x_ref.shape[1], LANES), 1)
    low = lane < HEAD_DIM
    parts = []
    for p in range(H_A // 2):
        acc = None
        for w, o in zip(w_g, o_g):
            wp = jnp.where(low, w[:, 2 * p:2 * p + 1], w[:, 2 * p + 1:2 * p + 2])
            term = wp * o[:, p * LANES:(p + 1) * LANES]
            acc = term if acc is None else acc + term
        parts.append(acc.astype(MXU_DTYPE))
    o = jnp.concatenate(parts + [ob_ref[0]], axis=-1)
    h = x_ref[0] + _dot(o, w_ref[...])
    h_ref[0] = h
    xn_ref[0] = _rms(h, g_ref[...])


def _tail_prompt(x, o1, l1, o4, l4, o16, l16, ob, w_out, g):
    n, s, dm = x.shape
    tb = TAIL_TB
    pt4 = jnp.asarray(_residue_perm(tb, 4).T, MXU_DTYPE)
    pt16 = jnp.asarray(_residue_perm(tb, 16).T, MXU_DTYPE)
    nat = lambda w: pl.BlockSpec((1, tb, w), lambda i, j: (i, j, 0))
    res = lambda d, w: pl.BlockSpec((1, d, tb // d, w), lambda i, j: (i, 0, j, 0))
    const = lambda a: pl.BlockSpec(a.shape, lambda i, j: (0,) * a.ndim)
    return pl.pallas_call(
        _tail_prompt_kernel,
        grid=(n, s // tb),
        in_specs=[nat(dm), nat(W_A), nat(LANES), res(4, W_A), res(4, LANES), res(16, W_A), res(16, LANES),
                  nat(W_BQ), const(pt4), const(pt16), const(w_out), const(g)],
        out_shape=[jax.ShapeDtypeStruct((n, s, dm), F32), jax.ShapeDtypeStruct((n, s, dm), F32)],
        out_specs=[nat(dm), nat(dm)],
        compiler_params=_cparams(("parallel", "parallel")),
        name="tail_prompt",
    )(x, o1, l1, o4, l4, o16, l16, ob, pt4, pt16, w_out, g)


def _tail_sample_kernel(x_ref, ot_ref, w_ref, g_ref, h_ref, xn_ref):
    h = x_ref[...] + _dot(ot_ref[...].T.astype(MXU_DTYPE), w_ref[...])
    h_ref[...] = h
    xn_ref[...] = _rms(h, g_ref[...])


def _tail_sample(x, o_t, w_out, g):
    return pl.pallas_call(
        _tail_sample_kernel,
        out_shape=[jax.ShapeDtypeStruct(x.shape, F32), jax.ShapeDtypeStruct(x.shape, F32)],
        compiler_params=_cparams(None),
        name="tail_sample",
    )(x, o_t, w_out, g)


def _sorting_network(n):
    size = 1 << (n - 1).bit_length()

    def merge(lo, hi, r):
        step = r * 2
        if step < hi - lo:
            yield from merge(lo, hi, step)
            yield from merge(lo + r, hi, step)
            yield from ((i, i + r) for i in range(lo + r, hi - r, step))
        else:
            yield (lo, lo + r)

    def sort(lo, hi):
        if hi - lo >= 1:
            mid = lo + (hi - lo) // 2
            yield from sort(lo, mid)
            yield from sort(mid + 1, hi)
            yield from merge(lo, hi, 1)

    return [(i, j) for i, j in sort(0, size - 1) if j < n]


def _top_rows(s, count):
    tiles = [s[v * SUBLANES:(v + 1) * SUBLANES] for v in range(s.shape[0] // SUBLANES)]
    for i, j in _sorting_network(len(tiles)):
        tiles[i], tiles[j] = jnp.maximum(tiles[i], tiles[j]), jnp.minimum(tiles[i], tiles[j])
    rows = []
    for t in range(count):
        m = jnp.max(tiles[0], axis=0, keepdims=True)
        rows.append(m)
        if t + 1 < count:
            popped = tiles[0] == m
            depth = min(len(tiles), count - t - 1)
            for k in range(depth):
                below = tiles[k + 1] if k + 1 < len(tiles) else NEG_INF
                tiles[k] = jnp.where(popped, below, tiles[k])
    return rows


def _pad_rows(rows, count):
    pad = [jnp.full_like(rows[0], NEG_INF)] * (count - len(rows))
    return jnp.concatenate(list(rows) + pad, axis=0)


def _candidate_sums(top1, top2):
    k = PEER_TOPK + 1
    wide = -(-k // SUBLANES) * SUBLANES
    narrow = -(-(k // 2) // SUBLANES) * SUBLANES
    assert k // (narrow + 1) <= 1
    v2_wide = _pad_rows(top2, wide)
    v2_narrow = v2_wide[:narrow]
    parts = [top1[0] + v2_wide] + [top1[a] + v2_narrow for a in range(1, narrow)]
    parts.append(_pad_rows(top1[narrow:], -(-(k - narrow) // SUBLANES) * SUBLANES) + top2[0])
    return jnp.concatenate(parts, axis=0)


def _route_kernel(x_ref, wqh_ref, wql_ref, skh_ref, skl_ref, thr_ref, p1_ref, p2_ref):
    xh, xl = _split2(x_ref[...])
    wqh = wqh_ref[...]
    q_t = _dot_nt(wqh, xh) + _dot_nt(wqh, xl) + _dot_nt(wql_ref[...], xh)
    half = D_KEY // 2
    for h in range(PEER_HEADS):
        scores = []
        for c in range(2):
            qh, ql = _split2(q_t[h * D_KEY + c * half:h * D_KEY + (c + 1) * half, :])
            scores.append(_dot(skh_ref[c], qh) + _dot(skh_ref[c], ql) + _dot(skl_ref[c], qh))
        for g in range(q_t.shape[1] // LANES):
            s1, s2 = (s[:, g * LANES:(g + 1) * LANES] for s in scores)
            top1, top2 = _top_rows(s1, PEER_TOPK + 1), _top_rows(s2, PEER_TOPK + 1)
            best = _top_rows(_candidate_sums(top1, top2), PEER_TOPK + 1)
            tau = 0.5 * (best[PEER_TOPK - 1] + best[PEER_TOPK])
            z = sum(jnp.exp(b - best[0]) for b in best[:PEER_TOPK])
            log_norm = best[0] + jnp.log(z)
            m2 = top2[0]
            thr_ref[h, g] = jnp.exp((tau - m2) - s1)
            p1_ref[h, g] = jnp.exp(s1 + (m2 - log_norm))
            p2_ref[h, g] = jnp.exp(s2 - m2)


def _route(xn, wq_t_hi, wq_t_lo, sk_hi, sk_lo, tb):
    t, dm = xn.shape
    const = lambda a: pl.BlockSpec(a.shape, lambda i: (0,) * a.ndim)
    gspec = pl.BlockSpec((PEER_HEADS, tb // LANES, N_KEYS, LANES), lambda i: (0, i, 0, 0))
    gshape = jax.ShapeDtypeStruct((PEER_HEADS, t // LANES, N_KEYS, LANES), F32)
    return pl.pallas_call(
        _route_kernel,
        grid=(t // tb,),
        in_specs=[pl.BlockSpec((tb, dm), lambda i: (i, 0)), const(wq_t_hi), const(wq_t_lo), const(sk_hi), const(sk_lo)],
        out_shape=[gshape] * 3,
        out_specs=[gspec] * 3,
        compiler_params=_cparams(("parallel",)),
        name="peer_route",
    )(xn, wq_t_hi, wq_t_lo, sk_hi, sk_lo)


EXPERT_CHUNK = SUBLANES * N_KEYS
GATE_JBLOCK = 4 * SUBLANES
INV_SQRT2 = 0.7071067811865476


def _gate_rows(thr_ref, p1_ref, p2_ref, w_ref, unroll=False):
    nv = GATE_JBLOCK // SUBLANES

    def row_group(i8, carry):
        aligned = (lambda x, m: x) if unroll else pl.multiple_of
        base = aligned(i8 * SUBLANES, SUBLANES)
        thr_t = [thr_ref[h, 0, pl.ds(base, SUBLANES), :] for h in range(PEER_HEADS)]
        p1_t = [p1_ref[h, 0, pl.ds(base, SUBLANES), :] for h in range(PEER_HEADS)]
        for jb in range(N_KEYS // GATE_JBLOCK):
            w = [[None] * nv for _ in range(SUBLANES)]
            for h in range(PEER_HEADS):
                p2 = [p2_ref[h, 0, pl.ds(jb * GATE_JBLOCK + v * SUBLANES, SUBLANES), :] for v in range(nv)]
                for r in range(SUBLANES):
                    thr = jnp.broadcast_to(thr_t[h][r:r + 1, :], (SUBLANES, LANES))
                    p1 = jnp.broadcast_to(p1_t[h][r:r + 1, :], (SUBLANES, LANES))
                    for v in range(nv):
                        term = jnp.where(p2[v] >= thr, p1 * p2[v], 0.0)
                        w[r][v] = term if w[r][v] is None else w[r][v] + term
            for r in range(SUBLANES):
                row0 = aligned((base + r) * N_KEYS + jb * GATE_JBLOCK, GATE_JBLOCK)
                w_ref[0, pl.ds(row0, GATE_JBLOCK), :] = jnp.concatenate(w[r], axis=0).astype(w_ref.dtype)
        return carry

    if unroll:
        for i8 in range(thr_ref.shape[2] // SUBLANES):
            row_group(i8, 0)
    else:
        lax.fori_loop(0, thr_ref.shape[2] // SUBLANES, row_group, 0)


GATE_ROWS = N_KEYS // 2


def _gate_specs(t):
    rows = pl.BlockSpec((PEER_HEADS, 1, GATE_ROWS, LANES), lambda i, j: (0, i, j, 0))
    full = pl.BlockSpec((PEER_HEADS, 1, N_KEYS, LANES), lambda i, j: (0, i, 0, 0))
    out = pl.BlockSpec((1, GATE_ROWS * N_KEYS, LANES), lambda i, j: (i, j, 0))
    shape = jax.ShapeDtypeStruct((t // LANES, N_KEYS * N_KEYS, LANES), MXU_DTYPE)
    return [rows, rows, full], out, shape


def _gates(thr, p1, p2):
    t = thr.shape[1] * LANES
    in_specs, out_spec, shape = _gate_specs(t)
    return pl.pallas_call(
        _gate_rows,
        grid=(t // LANES, N_KEYS // GATE_ROWS),
        in_specs=in_specs, out_specs=out_spec, out_shape=shape,
        compiler_params=_cparams(("parallel", "parallel")),
        name="peer_gates",
    )(thr, p1, p2)


def _sample_and_gates_kernel(*refs):
    n_in, n_gate = 11, 3
    sample_in, gate_in = refs[:n_in], refs[n_in:n_in + n_gate]
    sample_out, gate_out = refs[n_in + n_gate:-1], refs[-1]
    _sample_kernel(*sample_in, *sample_out)
    _gate_rows(*gate_in, gate_out)


def _sample_and_gates(sample_args, thr, p1, p2):
    t = thr.shape[1] * LANES
    ns, nkv = sample_args[7].shape[0], sample_args[9].shape[1]
    s_in, s_out, s_shapes = _sample_specs(*sample_args)
    if t // LANES != ns or N_KEYS // GATE_ROWS != nkv:
        outs = pl.pallas_call(
            _sample_kernel, grid=(ns, nkv), in_specs=s_in, out_specs=s_out, out_shape=s_shapes,
            compiler_params=_cparams(("arbitrary", "arbitrary")), name="sample_attn",
        )(*sample_args)
        return list(outs) + [_gates(thr, p1, p2)]
    g_in, g_out, g_shape = _gate_specs(t)
    return pl.pallas_call(
        _sample_and_gates_kernel,
        grid=(ns, nkv),
        in_specs=s_in + g_in, out_specs=s_out + [g_out], out_shape=s_shapes + [g_shape],
        compiler_params=_cparams(("arbitrary", "arbitrary")),
        name="sample_attn_and_gates",
    )(*sample_args, thr, p1, p2)


def _activate(g, a_ref, p_ref, w_ref):
    blk = 8 * SUBLANES
    for b in range(a_ref.shape[1] // blk):
        rows = pl.ds(b * blk, blk)
        a = a_ref[g, rows, :]
        half = 0.5 * a
        act = half + half * lax.erf(a * INV_SQRT2)
        p_ref[g, rows, :] = w_ref[g, rows, :] * act.astype(p_ref.dtype)


def _experts_kernel(xn_ref, res_ref, g_ref, down_ref, upt_ref, w_a_ref, w_b_ref,
                    y_ref, xb_ref, a0_ref, a1_ref, pb0_ref, pb1_ref, acc_ref):
    i, j = pl.program_id(0), pl.program_id(1)
    ch = EXPERT_CHUNK
    ng = a0_ref.shape[0]

    @pl.when((i == 0) & (j == 0))
    def _():
        for ref in (a0_ref, a1_ref, pb0_ref, pb1_ref):
            ref[...] = jnp.zeros_like(ref)

    @pl.when(j == 0)
    def _():
        xb_ref[...] = xn_ref[...].T.astype(MXU_DTYPE)

    keep = j >= 1

    def tick(half, a_in_ref, a_out_ref, p_out_ref, p_in_ref, w_ref):
        p_in = jnp.concatenate([p_in_ref[gg] for gg in range(ng)], axis=1)
        upd = _dot(upt_ref[:, half * ch:(half + 1) * ch], p_in)
        nxt = _dot(down_ref[pl.ds(half * ch, ch), :], xb_ref[...])
        for g in range(ng):
            a_out_ref[g] = nxt[:, g * LANES:(g + 1) * LANES]
            _activate(g, a_in_ref, p_out_ref, w_ref)
        return upd

    upd = tick(0, a1_ref, a0_ref, pb1_ref, pb0_ref, w_a_ref)
    upd = upd + tick(1, a0_ref, a1_ref, pb0_ref, pb1_ref, w_b_ref)
    acc_ref[...] = jnp.where(keep, acc_ref[...] + upd, 0.0)

    @pl.when(j == pl.num_programs(1) - 1)
    def _():
        y = res_ref[...] + acc_ref[...].T
        y_ref[...] = _rms(y, g_ref[...])


def _experts(xn, res, g, down, up_t, w, tb):
    t, dm = xn.shape
    ne = down.shape[0]
    step = 2 * EXPERT_CHUNK
    nj = ne // step
    ng = tb // LANES
    tok = pl.BlockSpec((tb, dm), lambda i, j: (i, 0))
    w_a = pl.BlockSpec((ng, EXPERT_CHUNK, LANES), lambda i, j: (i, jnp.maximum(2 * j - 1, 0), 0))
    w_b = pl.BlockSpec((ng, EXPERT_CHUNK, LANES), lambda i, j: (i, jnp.minimum(2 * j, 2 * nj - 1), 0))
    return pl.pallas_call(
        _experts_kernel,
        grid=(t // tb, nj + 1),
        in_specs=[tok, tok, pl.BlockSpec((1, dm), lambda i, j: (0, 0)),
                  pl.BlockSpec((step, dm), lambda i, j: (jnp.minimum(j, nj - 1), 0)),
                  pl.BlockSpec((dm, step), lambda i, j: (0, jnp.maximum(j - 1, 0))),
                  w_a, w_b],
        out_shape=jax.ShapeDtypeStruct((t, dm), F32),
        out_specs=tok,
        scratch_shapes=[pltpu.VMEM((dm, tb), MXU_DTYPE),
                        pltpu.VMEM((ng, EXPERT_CHUNK, LANES), F32), pltpu.VMEM((ng, EXPERT_CHUNK, LANES), F32),
                        pltpu.VMEM((ng, EXPERT_CHUNK, LANES), MXU_DTYPE),
                        pltpu.VMEM((ng, EXPERT_CHUNK, LANES), MXU_DTYPE),
                        pltpu.VMEM((dm, tb), F32)],
        compiler_params=_cparams(("arbitrary", "arbitrary")),
        name="peer_experts",
    )(xn, res, g, down, up_t, w, w)


PEER_TB = 512


def _cache_to_feature_major(c):
    return jnp.transpose(c, (0, 2, 3, 1))


def _cache_from_feature_major(c):
    return jnp.transpose(c, (0, 3, 1, 2))


def _layer(xp, xs, cache_a_k, cache_a_v, cache_b_k, cache_b_v, norm_attn, w_in, rel_bias, sinks, w_out, norm_ffn,
           w_peer_q, peer_sub_keys, peer_down, peer_up, g_final):
    n, s, dm = xp.shape
    ns = xs.shape[0]
    la, lb = cache_a_k.shape[1], cache_b_k.shape[1]

    c3 = 3 * W_A
    w_q_scaled = jnp.concatenate([w_in[:, :W_A] * SCALE, w_in[:, W_A:c3], w_in[:, c3:c3 + W_BQ] * SCALE,
                                  w_in[:, c3 + W_BQ:]], axis=1)
    qb_cols = w_q_scaled[:, c3:c3 + W_BQ].reshape(dm, H_B // 2, 2, HEAD_DIM)[:, :, ::-1].reshape(dm, W_BQ)
    vb_cols = w_in[:, c3 + W_BQ + W_BKV:].reshape(dm, KV_B, HEAD_DIM)[:, ::-1].reshape(dm, W_BKV)
    w_ext = jnp.concatenate([w_q_scaled, qb_cols, vb_cols], axis=1).astype(MXU_DTYPE)
    w_nat_t = w_q_scaled.T.astype(MXU_DTYPE)
    w_out_b = w_out.astype(MXU_DTYPE)
    g_attn, g_ffn = norm_attn[None, :], norm_ffn[None, :]
    wq_t = w_peer_q.T
    wq_t_hi = wq_t.astype(MXU_DTYPE)
    wq_t_lo = (wq_t - wq_t_hi.astype(F32)).astype(MXU_DTYPE)
    sk_hi = peer_sub_keys.astype(MXU_DTYPE)
    sk_lo = (peer_sub_keys - sk_hi.astype(F32)).astype(MXU_DTYPE)
    peer_w = (wq_t_hi, wq_t_lo, sk_hi, sk_lo, peer_down.astype(MXU_DTYPE), peer_up.T.astype(MXU_DTYPE))

    bidx_sa, mult_sa = _sample_tables(la)
    bidx_sb, _ = _sample_tables(lb)
    assert H_A == H_B
    *tabs_a, tab_b, tab_sa, tab_sb = _bias_tables(
        rel_bias, H_A, [(_window_bucket_matrix(d), 0) for d in DILATIONS]
        + [(_window_bucket_matrix(1), H_A), (bidx_sa, 0), (bidx_sb, H_A)])
    bias0_a, bias0_b = rel_bias[0, :H_A], rel_bias[0, H_A:]

    (q1, k1, v1, kf, vf, q4, k4, v4, q16, k16, v16, qb, qbs, kb, vb, vbs, kbf, vbf) = _qkv_prompt(xp, g_attn, w_ext)
    o1, l1 = _swa(tabs_a[0], [q1], k1, [v1], HEAD_CFG_A)
    flat = lambda a: a.reshape((a.shape[0] * a.shape[1],) + a.shape[2:])
    o4, l4 = _swa(tabs_a[1], [flat(q4)], flat(k4), [flat(v4)], HEAD_CFG_A)
    o16, l16 = _swa(tabs_a[2], [flat(q16)], flat(k16), [flat(v16)], HEAD_CFG_A)
    ob = _swa(tab_b, [qb, qbs], kb, [vb, vbs], HEAD_CFG_B, sinks=sinks, want_lse=False)
    unflat = lambda a, d: a.reshape((n, d) + a.shape[1:])
    hp, xnp = _tail_prompt(xp, o1, l1, unflat(o4, 4), unflat(l4, 4), unflat(o16, 16), unflat(l16, 16), ob,
                           w_out_b, g_ffn)
    prompt_caches = (kf[:, s - min(la, s):].reshape(n, -1, H_A, HEAD_DIM),
                     vf[:, s - min(la, s):].reshape(n, -1, H_A, HEAD_DIM),
                     kbf[:, s - min(lb, s):].reshape(n, -1, KV_B, HEAD_DIM),
                     vbf[:, s - min(lb, s):].reshape(n, -1, KV_B, HEAD_DIM))

    hp, xnp = hp.reshape(n * s, dm), xnp.reshape(n * s, dm)
    route_w, (down_b, up_t_b) = peer_w[:4], peer_w[4:]
    routing_p = _route(xnp, *route_w, PEER_TB)

    hs_t = _qkv_sample(xs, g_attn, w_nat_t)
    sample_args = (bias0_a, bias0_b, sinks, tab_sa, jnp.asarray(mult_sa), tab_sb, hs_t,
                   _cache_to_feature_major(cache_a_k), _cache_to_feature_major(cache_a_v),
                   _cache_to_feature_major(cache_b_k), _cache_to_feature_major(cache_b_v))
    o_t, *shifted, gates_p = _sample_and_gates(sample_args, *routing_p)
    h_s, xn_s = _tail_sample(xs, o_t, w_out_b, g_ffn)
    sample_caches = tuple(_cache_from_feature_major(c) for c in shifted)

    y_p = _experts(xnp, hp, g_final, down_b, up_t_b, gates_p, PEER_TB)
    gates_s = _gates(*_route(xn_s, *route_w, LANES))
    y_s = _experts(xn_s, h_s, g_final, down_b, up_t_b, gates_s, LANES)
    return y_p.reshape(n, s, dm), y_s, prompt_caches, sample_caches


def kernel(x_prompt, x_sample, cache_a_k, cache_a_v, cache_b_k, cache_b_v, norm_attn, w_in, rel_bias, sinks, w_out,
           norm_ffn, w_peer_q, peer_sub_keys, peer_down, peer_up, norm_final):
    depth = w_in.shape[0]
    assert depth == 1, "single-layer trunk"
    n, s, dm = x_prompt.shape
    ns = x_sample.shape[0]
    assert x_sample.shape[1] == 1 and s % QKV_TB == 0 and (n * s) % PEER_TB == 0 and ns % LANES == 0
    l = 0
    y_prompt, y_sample, prompt_caches, sample_caches = _layer(
        x_prompt, x_sample[:, 0], cache_a_k[l], cache_a_v[l], cache_b_k[l], cache_b_v[l], norm_attn[l], w_in[l],
        rel_bias, sinks[l], w_out[l], norm_ffn[l], w_peer_q[l], peer_sub_keys[l], peer_down[l], peer_up[l],
        norm_final[None, :])
    return ((y_prompt, y_sample.reshape(ns, 1, dm)) + tuple(c[None] for c in prompt_caches)
            + tuple(c[None] for c in sample_caches))
```

```python
import functools
import math

import numpy as np
import jax
import jax.numpy as jnp
from jax import lax
from jax.experimental import pallas as pl
from jax.experimental.pallas import tpu as pltpu

HEAD_DIM = 64
H_A = 8
H_B = 8
KV_B = 2
G_B = H_B // KV_B
DILATIONS = (1, 4, 16)
SLOTS = 128
N_BUCKETS = 32
MAX_DISTANCE = 2048
N_KEYS = 128
PEER_HEADS = 8
PEER_TOPK = 16
D_KEY = 128
RMS_EPS = 1e-6
SCALE = HEAD_DIM ** -0.5
W_A = H_A * HEAD_DIM
W_BQ = H_B * HEAD_DIM
W_BKV = KV_B * HEAD_DIM

LANES = 128
SUBLANES = 8
VMEM_LIMIT_BYTES = 56 * 1024 * 1024

MXU_DTYPE = jnp.bfloat16
F32 = jnp.float32
NEG_INF = float("-inf")


def _cparams(sem, flags=None):
    return pltpu.CompilerParams(dimension_semantics=sem, vmem_limit_bytes=VMEM_LIMIT_BYTES, flags=flags)


def _t5_bucket_np(dist):
    exact = N_BUCKETS // 2
    d = np.maximum(dist, 0)
    logd = np.log(np.maximum(d, 1).astype(np.float32) / np.float32(exact)) / np.float32(math.log(MAX_DISTANCE / exact))
    large = np.minimum(exact + (logd * np.float32(N_BUCKETS - exact)).astype(np.int32), N_BUCKETS - 1)
    return np.where(d < exact, d, large).astype(np.int32)


def _window_bucket_matrix(dilation):
    i = np.arange(SLOTS)[:, None]
    j = np.arange(2 * SLOTS)[None, :]
    dist = i - j + SLOTS
    ok = (dist >= 0) & (dist <= SLOTS)
    return np.where(ok, _t5_bucket_np(dist * dilation), -1).astype(np.int32)


def _sample_tables(cache_len):
    delta = cache_len - np.arange(cache_len)
    mult = np.zeros(cache_len, np.float32)
    for d in DILATIONS:
        mult += ((delta % d == 0) & (delta <= SLOTS * d)).astype(np.float32)
    return _t5_bucket_np(delta)[None, :], mult[None, :]


def _residue_perm(tb, d):
    p = np.zeros((tb, tb), np.float32)
    rows = np.arange(tb)
    p[rows, (rows % (tb // d)) * d + rows // (tb // d)] = 1.0
    return p


def _rms(x, g):
    return x * lax.rsqrt(jnp.mean(x * x, axis=-1, keepdims=True) + RMS_EPS) * g


def _dot(a, b):
    return jnp.dot(a, b, preferred_element_type=F32)


def _dot_nt(a, b):
    return lax.dot_general(a, b, (((1,), (1,)), ((), ())), preferred_element_type=F32)


def _split2(x):
    hi = x.astype(MXU_DTYPE)
    lo = (x - hi.astype(F32)).astype(MXU_DTYPE)
    return hi, lo


def _split3(x):
    hi = x.astype(MXU_DTYPE)
    r1 = x - hi.astype(F32)
    mid = r1.astype(MXU_DTYPE)
    lo = (r1 - mid.astype(F32)).astype(MXU_DTYPE)
    return hi, mid, lo


def _bias_kernel(col0s, nheads, rb_ref, *refs):
    n = len(col0s)
    for col0, bidx_ref, o_ref in zip(col0s, refs[:n], refs[n:]):
        b = bidx_ref[...]
        for h in range(nheads):
            val = jnp.full(b.shape, NEG_INF, F32)
            for k in range(N_BUCKETS):
                val = jnp.where(b == k, rb_ref[k, col0 + h], val)
            o_ref[h] = val


def _bias_tables(rel_bias, nheads, tables):
    bidx = [jnp.asarray(b) for b, _ in tables]
    return pl.pallas_call(
        functools.partial(_bias_kernel, tuple(c for _, c in tables), nheads),
        out_shape=[jax.ShapeDtypeStruct((nheads,) + b.shape, F32) for b in bidx],
        in_specs=[pl.BlockSpec(memory_space=pltpu.SMEM)] + [pl.BlockSpec(b.shape, lambda: (0, 0)) for b in bidx],
        out_specs=[pl.BlockSpec((nheads,) + b.shape, lambda: (0, 0, 0)) for b in bidx],
        name="bias_tables",
    )(rel_bias, *bidx)


QKV_TB = 512


def _qkv_prompt_kernel(x_ref, g_ref, w_ref, p4_ref, p16_ref,
                       q1_ref, k1_ref, v1_ref, kf_ref, vf_ref,
                       q4_ref, k4_ref, v4_ref, q16_ref, k16_ref, v16_ref,
                       qb_ref, qbs_ref, kb_ref, vb_ref, vbs_ref, kbf_ref, vbf_ref):
    xn = _rms(x_ref[0], g_ref[...])
    h = _dot(xn.astype(MXU_DTYPE), w_ref[...])
    c0, c1, c2, c3 = 0, W_A, 2 * W_A, 3 * W_A
    c4, c5, c6 = c3 + W_BQ, c3 + W_BQ + W_BKV, c3 + W_BQ + 2 * W_BKV
    c7 = c6 + W_BQ
    ha = h[:, :c3].astype(MXU_DTYPE)
    q1_ref[0] = ha[:, c0:c1]
    k1_ref[0] = ha[:, c1:c2]
    v1_ref[0] = ha[:, c2:c3]
    kf_ref[0] = h[:, c1:c2]
    vf_ref[0] = h[:, c2:c3]
    qb_ref[0] = h[:, c3:c4].astype(MXU_DTYPE)
    kb_ref[0] = h[:, c4:c5].astype(MXU_DTYPE)
    vb_ref[0] = h[:, c5:c6].astype(MXU_DTYPE)
    kbf_ref[0] = h[:, c4:c5]
    vbf_ref[0] = h[:, c5:c6]
    qbs_ref[0] = h[:, c6:c7].astype(MXU_DTYPE)
    vbs_ref[0] = h[:, c7:].astype(MXU_DTYPE)
    tb = ha.shape[0]
    for d, p_ref, outs in ((4, p4_ref, (q4_ref, k4_ref, v4_ref)), (16, p16_ref, (q16_ref, k16_ref, v16_ref))):
        perm = _dot(p_ref[...], ha).astype(MXU_DTYPE)
        rows = tb // d
        for r in range(d):
            for t, o_ref in enumerate(outs):
                o_ref[0, r] = perm[r * rows:(r + 1) * rows, t * W_A:(t + 1) * W_A]


def _qkv_prompt(x, g, w_ext):
    n, s, dm = x.shape
    tb = QKV_TB
    nb = s // tb
    p4 = jnp.asarray(_residue_perm(tb, 4), MXU_DTYPE)
    p16 = jnp.asarray(_residue_perm(tb, 16), MXU_DTYPE)
    bf = MXU_DTYPE

    def nat(width, dt):
        return jax.ShapeDtypeStruct((n, s, width), dt), pl.BlockSpec((1, tb, width), lambda i, j: (i, j, 0))

    def res(d):
        return (jax.ShapeDtypeStruct((n, d, s // d, W_A), bf),
                pl.BlockSpec((1, d, tb // d, W_A), lambda i, j: (i, 0, j, 0)))

    outs = [nat(W_A, bf), nat(W_A, bf), nat(W_A, bf), nat(W_A, F32), nat(W_A, F32),
            res(4), res(4), res(4), res(16), res(16), res(16),
            nat(W_BQ, bf), nat(W_BQ, bf), nat(W_BKV, bf), nat(W_BKV, bf), nat(W_BKV, bf),
            nat(W_BKV, F32), nat(W_BKV, F32)]
    return pl.pallas_call(
        _qkv_prompt_kernel,
        grid=(n, nb),
        in_specs=[pl.BlockSpec((1, tb, dm), lambda i, j: (i, j, 0)),
                  pl.BlockSpec((1, dm), lambda i, j: (0, 0)),
                  pl.BlockSpec(w_ext.shape, lambda i, j: (0, 0)),
                  pl.BlockSpec((tb, tb), lambda i, j: (0, 0)),
                  pl.BlockSpec((tb, tb), lambda i, j: (0, 0))],
        out_shape=[o[0] for o in outs],
        out_specs=[o[1] for o in outs],
        compiler_params=_cparams(("parallel", "parallel")),
        name="qkv_prompt",
    )(x, g, w_ext, p4, p16)


def _qkv_sample_kernel(x_ref, g_ref, wt_ref, o_ref):
    xn = _rms(x_ref[...], g_ref[...])
    o_ref[...] = _dot_nt(wt_ref[...], xn.astype(MXU_DTYPE))


def _qkv_sample(x, g, w_t):
    return pl.pallas_call(
        _qkv_sample_kernel,
        out_shape=jax.ShapeDtypeStruct((w_t.shape[0], x.shape[0]), F32),
        compiler_params=_cparams(None),
        name="qkv_sample",
    )(x, g, w_t)


SWA_TILES = 4


def _swa_kernel(head_cfg, has_sink, want_lse, *refs):
    it = iter(refs)
    tab_ref = next(it)
    sink_ref = next(it) if has_sink else None
    q_refs = [next(it)]
    if any(c[0] == 1 for c in head_cfg):
        q_refs.append(next(it))
    kc_ref, kp_ref = next(it), next(it)
    v_refs = [(next(it), next(it))]
    if any(c[3] == 1 for c in head_cfg):
        v_refs.append((next(it), next(it)))
    o_ref = next(it)
    lse_ref = next(it) if want_lse else None

    first = pl.program_id(1) == 0
    col = lax.broadcasted_iota(jnp.int32, (SLOTS, 2 * SLOTS), 1)
    prev_pen = jnp.where((col < SLOTS) & first, NEG_INF, 0.0)
    lane = lax.broadcasted_iota(jnp.int32, (SLOTS, LANES), 1)
    low = lane < HEAD_DIM
    for t in range(kc_ref.shape[1] // SLOTS):
        rows = slice(t * SLOTS, (t + 1) * SLOTS)
        before = slice((t - 1) * SLOTS, t * SLOTS)

        def keys(cur_ref, prev_ref, lanes):
            prev = prev_ref[0, :, lanes] if t == 0 else cur_ref[0, before, lanes]
            return jnp.concatenate([prev, cur_ref[0, rows, lanes]], axis=0)

        lse_acc = jnp.zeros((SLOTS, LANES), F32)
        for p in range(len(head_cfg) // 2):
            halves = []
            for hh in range(2):
                h = 2 * p + hh
                q_src, q_half, k_tile, v_src, v_tile = head_cfg[h]
                q = q_refs[q_src][0, rows, p * LANES:(p + 1) * LANES]
                q = jnp.where(low if q_half == 0 else jnp.logical_not(low), q, jnp.zeros_like(q))
                kcat = keys(kc_ref, kp_ref, slice(k_tile * LANES, (k_tile + 1) * LANES))
                s = _dot_nt(q, kcat) + tab_ref[h]
                if t == 0:
                    s = s + prev_pen
                m = jnp.max(s, axis=-1, keepdims=True)
                if has_sink:
                    m = jnp.maximum(m, sink_ref[h])
                e = jnp.exp(s - m)
                l = jnp.sum(e, axis=-1, keepdims=True)
                if has_sink:
                    l = l + jnp.exp(sink_ref[h] - m)
                vc_ref, vp_ref = v_refs[v_src]
                vcat = keys(vc_ref, vp_ref, slice(v_tile * LANES, (v_tile + 1) * LANES))
                halves.append(_dot(e.astype(MXU_DTYPE), vcat) / l)
                if want_lse:
                    lse_acc = jnp.where(lane == h, m + jnp.log(l), lse_acc)
            o_ref[0, rows, p * LANES:(p + 1) * LANES] = jnp.where(low, halves[0], halves[1]).astype(o_ref.dtype)
        if want_lse:
            lse_ref[0, rows] = lse_acc


def _swa(tab, q_list, k, v_list, head_cfg, sinks=None, want_lse=True):
    r, s, _ = q_list[0].shape
    ck = k.shape[-1]
    tiles = math.gcd(SWA_TILES, s // SLOTS)
    rows = tiles * SLOTS
    assert s % rows == 0
    nb = s // rows
    cur = lambda i, j: (i, j, 0)
    prev = lambda i, j: (i, jnp.maximum(tiles * j - 1, 0), 0)
    in_specs = [pl.BlockSpec(tab.shape, lambda i, j: (0, 0, 0))]
    args = [tab]
    if sinks is not None:
        in_specs.append(pl.BlockSpec(memory_space=pltpu.SMEM))
        args.append(sinks)
    for q in q_list:
        in_specs.append(pl.BlockSpec((1, rows, q.shape[-1]), cur))
        args.append(q)
    in_specs += [pl.BlockSpec((1, rows, ck), cur), pl.BlockSpec((1, SLOTS, ck), prev)]
    args += [k, k]
    for v in v_list:
        in_specs += [pl.BlockSpec((1, rows, ck), cur), pl.BlockSpec((1, SLOTS, ck), prev)]
        args += [v, v]
    out_shape = [jax.ShapeDtypeStruct((r, s, W_A), MXU_DTYPE)]
    out_specs = [pl.BlockSpec((1, rows, W_A), cur)]
    if want_lse:
        out_shape.append(jax.ShapeDtypeStruct((r, s, LANES), F32))
        out_specs.append(pl.BlockSpec((1, rows, LANES), cur))
    out = pl.pallas_call(
        functools.partial(_swa_kernel, tuple(head_cfg), sinks is not None, want_lse),
        grid=(r, nb),
        in_specs=in_specs,
        out_shape=out_shape,
        out_specs=out_specs,
        compiler_params=_cparams(("parallel", "arbitrary")),
        name="swa",
    )(*args)
    return out if want_lse else out[0]


HEAD_CFG_A = tuple((0, h % 2, h // 2, 0, h // 2) for h in range(H_A))


def _head_cfg_b():
    cfg = []
    for h in range(H_B):
        c = h // G_B
        src = 0 if h % 2 == c else 1
        cfg.append((src, c, 0, src, 0))
    return tuple(cfg)


HEAD_CFG_B = _head_cfg_b()


def _shift_in(x, new_col):
    length = x.shape[-1]
    rolled = pltpu.roll(x, length - 1, axis=1)
    lane = lax.broadcasted_iota(jnp.int32, x.shape, 1)
    return jnp.where(lane == length - 1, new_col, rolled)


def _col_attention(q, kmat, vmat, k_new, v_new, bias_row, mult_row, bias_new, mult_new, sink):
    s = jnp.sum(q * kmat, axis=0, keepdims=True) + bias_row
    s_new = jnp.sum(q * k_new, axis=0, keepdims=True) + bias_new
    if mult_row is not None:
        s = jnp.where(mult_row > 0.0, s, NEG_INF)
    m = jnp.maximum(jnp.max(s, axis=-1, keepdims=True), s_new)
    if sink is not None:
        m = jnp.maximum(m, sink)
    e = jnp.exp(s - m)
    if mult_row is not None:
        e = e * mult_row
    e_new = mult_new * jnp.exp(s_new - m)
    l = jnp.sum(e, axis=-1, keepdims=True) + e_new
    if sink is not None:
        l = l + jnp.exp(sink - m)
    o = jnp.sum(vmat * e, axis=-1, keepdims=True) + v_new * e_new
    return o / l


SAMPLE_HEADS = H_A // KV_B
assert SAMPLE_HEADS == G_B
SAMPLES_PER_STEP = 2


def _column(ht_ref, row0, nrows, pick):
    return jnp.sum(jnp.where(pick, ht_ref[pl.ds(row0, nrows), :], 0.0), axis=1, keepdims=True)


def _sample_kernel(bias0a_ref, bias0b_ref, sink_ref, taba_ref, mult_ref, tabb_ref, ht_ref,
                   ka_ref, va_ref, kb_ref, vb_ref, ot_ref, kao_ref, vao_ref, kbo_ref, vbo_ref):
    i, hb = pl.program_id(0), pl.program_id(1)
    per_step = ka_ref.shape[0]
    first_lane = (i * per_step) % LANES
    rows = SAMPLE_HEADS * HEAD_DIM
    blk = pl.multiple_of(hb * rows, rows)
    kvr = pl.multiple_of(hb * HEAD_DIM, HEAD_DIM)

    @pl.when((first_lane == 0) & (hb == 0))
    def _():
        ot_ref[...] = jnp.zeros_like(ot_ref)

    mult = mult_ref[...]
    for s in range(per_step):
        pick = lax.broadcasted_iota(jnp.int32, (1, LANES), 1) == first_lane + s

        def put(row0, col):
            ot_ref[pl.ds(row0, rows), :] = jnp.where(pick, col, ot_ref[pl.ds(row0, rows), :])

        q, k_new, v_new = (_column(ht_ref, c0 + blk, rows, pick) for c0 in (0, W_A, 2 * W_A))
        outs = []
        for hl in range(SAMPLE_HEADS):
            r = slice(hl * HEAD_DIM, (hl + 1) * HEAD_DIM)
            kmat, vmat = ka_ref[s, hl], va_ref[s, hl]
            outs.append(_col_attention(q[r], kmat, vmat, k_new[r], v_new[r], taba_ref[hl], mult,
                                       bias0a_ref[hb * SAMPLE_HEADS + hl], float(len(DILATIONS)), None))
            kao_ref[s, hl] = _shift_in(kmat, k_new[r])
            vao_ref[s, hl] = _shift_in(vmat, v_new[r])
        put(blk, jnp.concatenate(outs, axis=0))

        c3 = 3 * W_A
        q = _column(ht_ref, c3 + blk, rows, pick)
        k_new = _column(ht_ref, c3 + W_BQ + kvr, HEAD_DIM, pick)
        v_new = _column(ht_ref, c3 + W_BQ + W_BKV + kvr, HEAD_DIM, pick)
        kmat, vmat = kb_ref[s, 0], vb_ref[s, 0]
        outs = []
        for g in range(G_B):
            h = hb * G_B + g
            outs.append(_col_attention(q[g * HEAD_DIM:(g + 1) * HEAD_DIM], kmat, vmat, k_new, v_new, tabb_ref[g],
                                       None, bias0b_ref[h], 1.0, sink_ref[h]))
        put(W_A + blk, jnp.concatenate(outs, axis=0))
        kbo_ref[s, 0] = _shift_in(kmat, k_new)
        vbo_ref[s, 0] = _shift_in(vmat, v_new)


def _sample_specs(bias0_a, bias0_b, sinks, tab_a, mult_a, tab_b, h_t, ka_t, va_t, kb_t, vb_t):
    ns, nh, hd, la = ka_t.shape
    _, nkv, _, lb = kb_t.shape
    sps = SAMPLES_PER_STEP
    assert nh // SAMPLE_HEADS == nkv and ns % LANES == 0 and LANES % sps == 0
    smem = pl.BlockSpec(memory_space=pltpu.SMEM)
    lanes_of = lambda rows: pl.BlockSpec((rows, LANES), lambda i, j: (0, (i * sps) // LANES))
    a_spec = pl.BlockSpec((sps, SAMPLE_HEADS, hd, la), lambda i, j: (i, j, 0, 0))
    b_spec = pl.BlockSpec((sps, 1, hd, lb), lambda i, j: (i, j, 0, 0))
    in_specs = [smem, smem, smem,
                pl.BlockSpec((SAMPLE_HEADS, 1, la), lambda i, j: (j, 0, 0)),
                pl.BlockSpec((1, la), lambda i, j: (0, 0)),
                pl.BlockSpec((G_B, 1, lb), lambda i, j: (j, 0, 0)),
                lanes_of(h_t.shape[0]), a_spec, a_spec, b_spec, b_spec]
    out_specs = [lanes_of(W_A + W_BQ), a_spec, a_spec, b_spec, b_spec]
    shapes = [jax.ShapeDtypeStruct((W_A + W_BQ, ns), F32)] + [jax.ShapeDtypeStruct(c.shape, F32)
                                                              for c in (ka_t, va_t, kb_t, vb_t)]
    return in_specs, out_specs, shapes


TAIL_TB = 512


def _unpermute(pt_ref, blocks_ref, exact):
    d = blocks_ref.shape[1]
    x = jnp.concatenate([blocks_ref[0, r] for r in range(d)], axis=0)
    if not exact:
        return _dot(pt_ref[...], x)
    return sum(_dot(pt_ref[...], part) for part in _split3(x))


def _tail_prompt_kernel(x_ref, o1_ref, l1_ref, o4_ref, l4_ref, o16_ref, l16_ref, ob_ref, pt4_ref, pt16_ref,
                        w_ref, g_ref, h_ref, xn_ref):
    o_g = [o1_ref[0].astype(F32), _unpermute(pt4_ref, o4_ref, False), _unpermute(pt16_ref, o16_ref, False)]
    l_g = [l1_ref[0], _unpermute(pt4_ref, l4_ref, True), _unpermute(pt16_ref, l16_ref, True)]
    m = jnp.maximum(jnp.maximum(l_g[0], l_g[1]), l_g[2])
    e_g = [jnp.exp(l - m) for l in l_g]
    den = e_g[0] + e_g[1] + e_g[2]
    w_g = [e / den for e in e_g]
    lane = lax.broadcasted_iota(jnp.int32, (x_ref.shape[1], LANES), 1)
    low = lane < HEAD_DIM
    parts = []
    for p in range(H_A // 2):
        acc = None
        for w, o in zip(w_g, o_g):
            wp = jnp.where(low, w[:, 2 * p:2 * p + 1], w[:, 2 * p + 1:2 * p + 2])
            term = wp * o[:, p * LANES:(p + 1) * LANES]
            acc = term if acc is None else acc + term
        parts.append(acc.astype(MXU_DTYPE))
    o = jnp.concatenate(parts + [ob_ref[0]], axis=-1)
    h = x_ref[0] + _dot(o, w_ref[...])
    h_ref[0] = h
    xn_ref[0] = _rms(h, g_ref[...])


def _tail_prompt(x, o1, l1, o4, l4, o16, l16, ob, w_out, g):
    n, s, dm = x.shape
    tb = TAIL_TB
    pt4 = jnp.asarray(_residue_perm(tb, 4).T, MXU_DTYPE)
    pt16 = jnp.asarray(_residue_perm(tb, 16).T, MXU_DTYPE)
    nat = lambda w: pl.BlockSpec((1, tb, w), lambda i, j: (i, j, 0))
    res = lambda d, w: pl.BlockSpec((1, d, tb // d, w), lambda i, j: (i, 0, j, 0))
    const = lambda a: pl.BlockSpec(a.shape, lambda i, j: (0,) * a.ndim)
    return pl.pallas_call(
        _tail_prompt_kernel,
        grid=(n, s // tb),
        in_specs=[nat(dm), nat(W_A), nat(LANES), res(4, W_A), res(4, LANES), res(16, W_A), res(16, LANES),
                  nat(W_BQ), const(pt4), const(pt16), const(w_out), const(g)],
        out_shape=[jax.ShapeDtypeStruct((n, s, dm), F32), jax.ShapeDtypeStruct((n, s, dm), F32)],
        out_specs=[nat(dm), nat(dm)],
        compiler_params=_cparams(("parallel", "parallel")),
        name="tail_prompt",
    )(x, o1, l1, o4, l4, o16, l16, ob, pt4, pt16, w_out, g)


def _tail_sample_kernel(x_ref, ot_ref, w_ref, g_ref, h_ref, xn_ref):
    h = x_ref[...] + _dot(ot_ref[...].T.astype(MXU_DTYPE), w_ref[...])
    h_ref[...] = h
    xn_ref[...] = _rms(h, g_ref[...])


def _tail_sample(x, o_t, w_out, g):
    return pl.pallas_call(
        _tail_sample_kernel,
        out_shape=[jax.ShapeDtypeStruct(x.shape, F32), jax.ShapeDtypeStruct(x.shape, F32)],
        compiler_params=_cparams(None),
        name="tail_sample",
    )(x, o_t, w_out, g)


def _sorting_network(n):
    size = 1 << (n - 1).bit_length()

    def merge(lo, hi, r):
        step = r * 2
        if step < hi - lo:
            yield from merge(lo, hi, step)
            yield from merge(lo + r, hi, step)
            yield from ((i, i + r) for i in range(lo + r, hi - r, step))
        else:
            yield (lo, lo + r)

    def sort(lo, hi):
        if hi - lo >= 1:
            mid = lo + (hi - lo) // 2
            yield from sort(lo, mid)
            yield from sort(mid + 1, hi)
            yield from merge(lo, hi, 1)

    return [(i, j) for i, j in sort(0, size - 1) if j < n]


def _top_rows(s, count):
    tiles = [s[v * SUBLANES:(v + 1) * SUBLANES] for v in range(s.shape[0] // SUBLANES)]
    for i, j in _sorting_network(len(tiles)):
        tiles[i], tiles[j] = jnp.maximum(tiles[i], tiles[j]), jnp.minimum(tiles[i], tiles[j])
    rows = []
    for t in range(count):
        m = jnp.max(tiles[0], axis=0, keepdims=True)
        rows.append(m)
        if t + 1 < count:
            popped = tiles[0] == m
            depth = min(len(tiles), count - t - 1)
            for k in range(depth):
                below = tiles[k + 1] if k + 1 < len(tiles) else NEG_INF
                tiles[k] = jnp.where(popped, below, tiles[k])
    return rows


def _pad_rows(rows, count):
    pad = [jnp.full_like(rows[0], NEG_INF)] * (count - len(rows))
    return jnp.concatenate(list(rows) + pad, axis=0)


def _candidate_sums(top1, top2):
    k = PEER_TOPK + 1
    wide = -(-k // SUBLANES) * SUBLANES
    narrow = -(-(k // 2) // SUBLANES) * SUBLANES
    assert k // (narrow + 1) <= 1
    v2_wide = _pad_rows(top2, wide)
    v2_narrow = v2_wide[:narrow]
    parts = [top1[0] + v2_wide] + [top1[a] + v2_narrow for a in range(1, narrow)]
    parts.append(_pad_rows(top1[narrow:], -(-(k - narrow) // SUBLANES) * SUBLANES) + top2[0])
    return jnp.concatenate(parts, axis=0)


def _route_kernel(x_ref, wqh_ref, wql_ref, skh_ref, skl_ref, thr_ref, p1_ref, p2_ref):
    xh, xl = _split2(x_ref[...])
    wqh = wqh_ref[...]
    q_t = _dot_nt(wqh, xh) + _dot_nt(wqh, xl) + _dot_nt(wql_ref[...], xh)
    half = D_KEY // 2
    for h in range(PEER_HEADS):
        scores = []
        for c in range(2):
            qh, ql = _split2(q_t[h * D_KEY + c * half:h * D_KEY + (c + 1) * half, :])
            scores.append(_dot(skh_ref[c], qh) + _dot(skh_ref[c], ql) + _dot(skl_ref[c], qh))
        for g in range(q_t.shape[1] // LANES):
            s1, s2 = (s[:, g * LANES:(g + 1) * LANES] for s in scores)
            top1, top2 = _top_rows(s1, PEER_TOPK + 1), _top_rows(s2, PEER_TOPK + 1)
            best = _top_rows(_candidate_sums(top1, top2), PEER_TOPK + 1)
            tau = 0.5 * (best[PEER_TOPK - 1] + best[PEER_TOPK])
            z = sum(jnp.exp(b - best[0]) for b in best[:PEER_TOPK])
            log_norm = best[0] + jnp.log(z)
            m2 = top2[0]
            thr_ref[h, g] = jnp.exp((tau - m2) - s1)
            p1_ref[h, g] = jnp.exp(s1 + (m2 - log_norm))
            p2_ref[h, g] = jnp.exp(s2 - m2)


def _route(xn, wq_t_hi, wq_t_lo, sk_hi, sk_lo, tb):
    t, dm = xn.shape
    const = lambda a: pl.BlockSpec(a.shape, lambda i: (0,) * a.ndim)
    gspec = pl.BlockSpec((PEER_HEADS, tb // LANES, N_KEYS, LANES), lambda i: (0, i, 0, 0))
    gshape = jax.ShapeDtypeStruct((PEER_HEADS, t // LANES, N_KEYS, LANES), F32)
    return pl.pallas_call(
        _route_kernel,
        grid=(t // tb,),
        in_specs=[pl.BlockSpec((tb, dm), lambda i: (i, 0)), const(wq_t_hi), const(wq_t_lo), const(sk_hi), const(sk_lo)],
        out_shape=[gshape] * 3,
        out_specs=[gspec] * 3,
        compiler_params=_cparams(("parallel",)),
        name="peer_route",
    )(xn, wq_t_hi, wq_t_lo, sk_hi, sk_lo)


EXPERT_CHUNK = SUBLANES * N_KEYS
GATE_JBLOCK = 2 * SUBLANES
INV_SQRT2 = 0.7071067811865476


def _gate_rows(thr_ref, p1_ref, p2_ref, w_ref):
    nv = GATE_JBLOCK // SUBLANES
    row_groups = thr_ref.shape[2] // SUBLANES

    def row_group(it, carry):
        g = it // row_groups
        base = pl.multiple_of((it % row_groups) * SUBLANES, SUBLANES)
        thr_t = [thr_ref[h, g, pl.ds(base, SUBLANES), :] for h in range(PEER_HEADS)]
        p1_t = [p1_ref[h, g, pl.ds(base, SUBLANES), :] for h in range(PEER_HEADS)]
        for jb in range(N_KEYS // GATE_JBLOCK):
            w = [[None] * nv for _ in range(SUBLANES)]
            for h in range(PEER_HEADS):
                p2 = [p2_ref[h, g, pl.ds(jb * GATE_JBLOCK + v * SUBLANES, SUBLANES), :] for v in range(nv)]
                for r in range(SUBLANES):
                    thr = jnp.broadcast_to(thr_t[h][r:r + 1, :], (SUBLANES, LANES))
                    p1 = jnp.broadcast_to(p1_t[h][r:r + 1, :], (SUBLANES, LANES))
                    for v in range(nv):
                        term = jnp.where(p2[v] >= thr, p1 * p2[v], 0.0)
                        w[r][v] = term if w[r][v] is None else w[r][v] + term
            for r in range(SUBLANES):
                row0 = pl.multiple_of((base + r) * N_KEYS + jb * GATE_JBLOCK, GATE_JBLOCK)
                w_ref[g, pl.ds(row0, GATE_JBLOCK), :] = jnp.concatenate(w[r], axis=0).astype(w_ref.dtype)
        return carry

    lax.fori_loop(0, thr_ref.shape[1] * row_groups, row_group, 0)


GATE_ROWS = N_KEYS // 2


def _gate_specs(t, groups):
    rows = pl.BlockSpec((PEER_HEADS, groups, GATE_ROWS, LANES), lambda i, j: (0, i, j, 0))
    full = pl.BlockSpec((PEER_HEADS, groups, N_KEYS, LANES), lambda i, j: (0, i, 0, 0))
    out = pl.BlockSpec((groups, GATE_ROWS * N_KEYS, LANES), lambda i, j: (i, j, 0))
    shape = jax.ShapeDtypeStruct((t // LANES, N_KEYS * N_KEYS, LANES), MXU_DTYPE)
    return [rows, rows, full], out, shape


def _gates(thr, p1, p2):
    t = thr.shape[1] * LANES
    in_specs, out_spec, shape = _gate_specs(t, 1)
    return pl.pallas_call(
        _gate_rows,
        grid=(t // LANES, N_KEYS // GATE_ROWS),
        in_specs=in_specs, out_specs=out_spec, out_shape=shape,
        compiler_params=_cparams(("parallel", "parallel")),
        name="peer_gates",
    )(thr, p1, p2)


def _sample_and_gates_kernel(*refs):
    n_in, n_gate = 11, 3
    sample_in, gate_in = refs[:n_in], refs[n_in:n_in + n_gate]
    sample_out, gate_out = refs[n_in + n_gate:-1], refs[-1]
    _sample_kernel(*sample_in, *sample_out)
    _gate_rows(*gate_in, gate_out)


def _sample_and_gates(sample_args, thr, p1, p2):
    t = thr.shape[1] * LANES
    ns, nkv = sample_args[7].shape[0], sample_args[9].shape[1]
    steps = ns // SAMPLES_PER_STEP
    s_in, s_out, s_shapes = _sample_specs(*sample_args)
    if t // LANES != ns or N_KEYS // GATE_ROWS != nkv:
        outs = pl.pallas_call(
            _sample_kernel, grid=(steps, nkv), in_specs=s_in, out_specs=s_out, out_shape=s_shapes,
            compiler_params=_cparams(("arbitrary", "arbitrary")), name="sample_attn",
        )(*sample_args)
        return list(outs) + [_gates(thr, p1, p2)]
    g_in, g_out, g_shape = _gate_specs(t, SAMPLES_PER_STEP)
    return pl.pallas_call(
        _sample_and_gates_kernel,
        grid=(steps, nkv),
        in_specs=s_in + g_in, out_specs=s_out + [g_out], out_shape=s_shapes + [g_shape],
        compiler_params=_cparams(("arbitrary", "arbitrary")),
        name="sample_attn_and_gates",
    )(*sample_args, thr, p1, p2)


def _activate(g, a_ref, p_ref, w_ref):
    blk = 8 * SUBLANES
    for b in range(a_ref.shape[1] // blk):
        rows = pl.ds(b * blk, blk)
        a = a_ref[g, rows, :]
        half = 0.5 * a
        act = half + half * lax.erf(a * INV_SQRT2)
        p_ref[g, rows, :] = w_ref[g, rows, :] * act.astype(p_ref.dtype)


def _experts_kernel(xn_ref, res_ref, g_ref, down_ref, upt_ref, w_a_ref, w_b_ref,
                    y_ref, xb_ref, a0_ref, a1_ref, pb0_ref, pb1_ref, acc_ref):
    i, j = pl.program_id(0), pl.program_id(1)
    ch = EXPERT_CHUNK
    ng = a0_ref.shape[0]

    @pl.when((i == 0) & (j == 0))
    def _():
        for ref in (a0_ref, a1_ref, pb0_ref, pb1_ref):
            ref[...] = jnp.zeros_like(ref)

    @pl.when(j == 0)
    def _():
        xb_ref[...] = xn_ref[...].astype(MXU_DTYPE)

    keep = j >= 1

    def tick(half, a_in_ref, a_out_ref, p_out_ref, p_in_ref, w_ref):
        p_in = jnp.concatenate([p_in_ref[gg] for gg in range(ng)], axis=1)
        upd = _dot(upt_ref[:, half * ch:(half + 1) * ch], p_in)
        nxt = _dot_nt(down_ref[pl.ds(half * ch, ch), :], xb_ref[...])
        for g in range(ng):
            a_out_ref[g] = nxt[:, g * LANES:(g + 1) * LANES]
            _activate(g, a_in_ref, p_out_ref, w_ref)
        return upd

    upd = tick(0, a1_ref, a0_ref, pb1_ref, pb0_ref, w_a_ref)
    upd = upd + tick(1, a0_ref, a1_ref, pb0_ref, pb1_ref, w_b_ref)
    acc_ref[...] = jnp.where(keep, acc_ref[...] + upd, 0.0)

    @pl.when(j == pl.num_programs(1) - 1)
    def _():
        y = res_ref[...] + acc_ref[...].T
        y_ref[...] = _rms(y, g_ref[...])


def _experts(xn, res, g, down, up_t, w, tb):
    t, dm = xn.shape
    ne = down.shape[0]
    step = 2 * EXPERT_CHUNK
    nj = ne // step
    ng = tb // LANES
    tok = pl.BlockSpec((tb, dm), lambda i, j: (i, 0))
    w_a = pl.BlockSpec((ng, EXPERT_CHUNK, LANES), lambda i, j: (i, jnp.maximum(2 * j - 1, 0), 0))
    w_b = pl.BlockSpec((ng, EXPERT_CHUNK, LANES), lambda i, j: (i, jnp.minimum(2 * j, 2 * nj - 1), 0))
    return pl.pallas_call(
        _experts_kernel,
        grid=(t // tb, nj + 1),
        in_specs=[tok, tok, pl.BlockSpec((1, dm), lambda i, j: (0, 0)),
                  pl.BlockSpec((step, dm), lambda i, j: (jnp.minimum(j, nj - 1), 0)),
                  pl.BlockSpec((dm, step), lambda i, j: (0, jnp.maximum(j - 1, 0))),
                  w_a, w_b],
        out_shape=jax.ShapeDtypeStruct((t, dm), F32),
        out_specs=tok,
        scratch_shapes=[pltpu.VMEM((tb, dm), MXU_DTYPE),
                        pltpu.VMEM((ng, EXPERT_CHUNK, LANES), F32), pltpu.VMEM((ng, EXPERT_CHUNK, LANES), F32),
                        pltpu.VMEM((ng, EXPERT_CHUNK, LANES), MXU_DTYPE),
                        pltpu.VMEM((ng, EXPERT_CHUNK, LANES), MXU_DTYPE),
                        pltpu.VMEM((dm, tb), F32)],
        compiler_params=_cparams(("arbitrary", "arbitrary")),
        name="peer_experts",
    )(xn, res, g, down, up_t, w, w)


PEER_TB = 512


def _cache_to_feature_major(c):
    return jnp.transpose(c, (0, 2, 3, 1))


def _cache_from_feature_major(c):
    return jnp.transpose(c, (0, 3, 1, 2))


def _layer(xp, xs, cache_a_k, cache_a_v, cache_b_k, cache_b_v, norm_attn, w_in, rel_bias, sinks, w_out, norm_ffn,
           w_peer_q, peer_sub_keys, peer_down, peer_up, g_final):
    n, s, dm = xp.shape
    ns = xs.shape[0]
    la, lb = cache_a_k.shape[1], cache_b_k.shape[1]

    c3 = 3 * W_A
    w_q_scaled = jnp.concatenate([w_in[:, :W_A] * SCALE, w_in[:, W_A:c3], w_in[:, c3:c3 + W_BQ] * SCALE,
                                  w_in[:, c3 + W_BQ:]], axis=1)
    qb_cols = w_q_scaled[:, c3:c3 + W_BQ].reshape(dm, H_B // 2, 2, HEAD_DIM)[:, :, ::-1].reshape(dm, W_BQ)
    vb_cols = w_in[:, c3 + W_BQ + W_BKV:].reshape(dm, KV_B, HEAD_DIM)[:, ::-1].reshape(dm, W_BKV)
    w_ext = jnp.concatenate([w_q_scaled, qb_cols, vb_cols], axis=1).astype(MXU_DTYPE)
    w_nat_t = w_q_scaled.T.astype(MXU_DTYPE)
    w_out_b = w_out.astype(MXU_DTYPE)
    g_attn, g_ffn = norm_attn[None, :], norm_ffn[None, :]
    wq_t = w_peer_q.T
    wq_t_hi = wq_t.astype(MXU_DTYPE)
    wq_t_lo = (wq_t - wq_t_hi.astype(F32)).astype(MXU_DTYPE)
    sk_hi = peer_sub_keys.astype(MXU_DTYPE)
    sk_lo = (peer_sub_keys - sk_hi.astype(F32)).astype(MXU_DTYPE)
    peer_w = (wq_t_hi, wq_t_lo, sk_hi, sk_lo, peer_down.astype(MXU_DTYPE), peer_up.T.astype(MXU_DTYPE))

    bidx_sa, mult_sa = _sample_tables(la)
    bidx_sb, _ = _sample_tables(lb)
    assert H_A == H_B
    *tabs_a, tab_b, tab_sa, tab_sb = _bias_tables(
        rel_bias, H_A, [(_window_bucket_matrix(d), 0) for d in DILATIONS]
        + [(_window_bucket_matrix(1), H_A), (bidx_sa, 0), (bidx_sb, H_A)])
    bias0_a, bias0_b = rel_bias[0, :H_A], rel_bias[0, H_A:]

    (q1, k1, v1, kf, vf, q4, k4, v4, q16, k16, v16, qb, qbs, kb, vb, vbs, kbf, vbf) = _qkv_prompt(xp, g_attn, w_ext)
    o1, l1 = _swa(tabs_a[0], [q1], k1, [v1], HEAD_CFG_A)
    flat = lambda a: a.reshape((a.shape[0] * a.shape[1],) + a.shape[2:])
    o4, l4 = _swa(tabs_a[1], [flat(q4)], flat(k4), [flat(v4)], HEAD_CFG_A)
    o16, l16 = _swa(tabs_a[2], [flat(q16)], flat(k16), [flat(v16)], HEAD_CFG_A)
    ob = _swa(tab_b, [qb, qbs], kb, [vb, vbs], HEAD_CFG_B, sinks=sinks, want_lse=False)
    unflat = lambda a, d: a.reshape((n, d) + a.shape[1:])
    hp, xnp = _tail_prompt(xp, o1, l1, unflat(o4, 4), unflat(l4, 4), unflat(o16, 16), unflat(l16, 16), ob,
                           w_out_b, g_ffn)
    prompt_caches = (kf[:, s - min(la, s):].reshape(n, -1, H_A, HEAD_DIM),
                     vf[:, s - min(la, s):].reshape(n, -1, H_A, HEAD_DIM),
                     kbf[:, s - min(lb, s):].reshape(n, -1, KV_B, HEAD_DIM),
                     vbf[:, s - min(lb, s):].reshape(n, -1, KV_B, HEAD_DIM))

    hp, xnp = hp.reshape(n * s, dm), xnp.reshape(n * s, dm)
    route_w, (down_b, up_t_b) = peer_w[:4], peer_w[4:]
    routing_p = _route(xnp, *route_w, PEER_TB)

    hs_t = _qkv_sample(xs, g_attn, w_nat_t)
    sample_args = (bias0_a, bias0_b, sinks, tab_sa, jnp.asarray(mult_sa), tab_sb, hs_t,
                   _cache_to_feature_major(cache_a_k), _cache_to_feature_major(cache_a_v),
                   _cache_to_feature_major(cache_b_k), _cache_to_feature_major(cache_b_v))
    o_t, *shifted, gates_p = _sample_and_gates(sample_args, *routing_p)
    h_s, xn_s = _tail_sample(xs, o_t, w_out_b, g_ffn)
    sample_caches = tuple(_cache_from_feature_major(c) for c in shifted)

    y_p = _experts(xnp, hp, g_final, down_b, up_t_b, gates_p, PEER_TB)
    gates_s = _gates(*_route(xn_s, *route_w, LANES))
    y_s = _experts(xn_s, h_s, g_final, down_b, up_t_b, gates_s, LANES)
    return y_p.reshape(n, s, dm), y_s, prompt_caches, sample_caches


def kernel(x_prompt, x_sample, cache_a_k, cache_a_v, cache_b_k, cache_b_v, norm_attn, w_in, rel_bias, sinks, w_out,
           norm_ffn, w_peer_q, peer_sub_keys, peer_down, peer_up, norm_final):
    depth = w_in.shape[0]
    assert depth == 1, "single-layer trunk"
    n, s, dm = x_prompt.shape
    ns = x_sample.shape[0]
    assert x_sample.shape[1] == 1 and s % QKV_TB == 0 and (n * s) % PEER_TB == 0 and ns % LANES == 0
    l = 0
    y_prompt, y_sample, prompt_caches, sample_caches = _layer(
        x_prompt, x_sample[:, 0], cache_a_k[l], cache_a_v[l], cache_b_k[l], cache_b_v[l], norm_attn[l], w_in[l],
        rel_bias, sinks[l], w_out[l], norm_ffn[l], w_peer_q[l], peer_sub_keys[l], peer_down[l], peer_up[l],
        norm_final[None, :])
    return ((y_prompt, y_sample.reshape(ns, 1, dm)) + tuple(c[None] for c in prompt_caches)
            + tuple(c[None] for c in sample_caches))
```

```python
import functools
import math

import numpy as np
import jax
import jax.numpy as jnp
from jax import lax
from jax.experimental import pallas as pl
from jax.experimental.pallas import tpu as pltpu

HEAD_DIM = 64
H_A = 8
H_B = 8
KV_B = 2
G_B = H_B // KV_B
DILATIONS = (1, 4, 16)
SLOTS = 128
N_BUCKETS = 32
MAX_DISTANCE = 2048
N_KEYS = 128
PEER_HEADS = 8
PEER_TOPK = 16
D_KEY = 128
RMS_EPS = 1e-6
SCALE = HEAD_DIM ** -0.5
W_A = H_A * HEAD_DIM
W_BQ = H_B * HEAD_DIM
W_BKV = KV_B * HEAD_DIM

LANES = 128
SUBLANES = 8
VMEM_LIMIT_BYTES = 56 * 1024 * 1024

MXU_DTYPE = jnp.bfloat16
F32 = jnp.float32
NEG_INF = float("-inf")


def _cparams(sem):
    return pltpu.CompilerParams(dimension_semantics=sem, vmem_limit_bytes=VMEM_LIMIT_BYTES)


def _t5_bucket_np(dist):
    exact = N_BUCKETS // 2
    d = np.maximum(dist, 0)
    logd = np.log(np.maximum(d, 1).astype(np.float32) / np.float32(exact)) / np.float32(math.log(MAX_DISTANCE / exact))
    large = np.minimum(exact + (logd * np.float32(N_BUCKETS - exact)).astype(np.int32), N_BUCKETS - 1)
    return np.where(d < exact, d, large).astype(np.int32)


def _window_bucket_matrix(dilation):
    i = np.arange(SLOTS)[:, None]
    j = np.arange(2 * SLOTS)[None, :]
    dist = i - j + SLOTS
    ok = (dist >= 0) & (dist <= SLOTS)
    return np.where(ok, _t5_bucket_np(dist * dilation), -1).astype(np.int32)


def _sample_tables(cache_len):
    delta = cache_len - np.arange(cache_len)
    mult = np.zeros(cache_len, np.float32)
    for d in DILATIONS:
        mult += ((delta % d == 0) & (delta <= SLOTS * d)).astype(np.float32)
    return _t5_bucket_np(delta)[None, :], mult[None, :]


def _residue_perm(tb, d):
    p = np.zeros((tb, tb), np.float32)
    rows = np.arange(tb)
    p[rows, (rows % (tb // d)) * d + rows // (tb // d)] = 1.0
    return p


def _rms(x, g):
    return x * lax.rsqrt(jnp.mean(x * x, axis=-1, keepdims=True) + RMS_EPS) * g


def _dot(a, b):
    return jnp.dot(a, b, preferred_element_type=F32)


def _dot_nt(a, b):
    return lax.dot_general(a, b, (((1,), (1,)), ((), ())), preferred_element_type=F32)


def _split2(x):
    hi = x.astype(MXU_DTYPE)
    lo = (x - hi.astype(F32)).astype(MXU_DTYPE)
    return hi, lo


def _split3(x):
    hi = x.astype(MXU_DTYPE)
    r1 = x - hi.astype(F32)
    mid = r1.astype(MXU_DTYPE)
    lo = (r1 - mid.astype(F32)).astype(MXU_DTYPE)
    return hi, mid, lo


def _bias_kernel(col0s, nheads, rb_ref, *refs):
    n = len(col0s)
    for col0, bidx_ref, o_ref in zip(col0s, refs[:n], refs[n:]):
        b = bidx_ref[...]
        for h in range(nheads):
            val = jnp.full(b.shape, NEG_INF, F32)
            for k in range(N_BUCKETS):
                val = jnp.where(b == k, rb_ref[k, col0 + h], val)
            o_ref[h] = val


def _bias_tables(rel_bias, nheads, tables):
    bidx = [jnp.asarray(b) for b, _ in tables]
    return pl.pallas_call(
        functools.partial(_bias_kernel, tuple(c for _, c in tables), nheads),
        out_shape=[jax.ShapeDtypeStruct((nheads,) + b.shape, F32) for b in bidx],
        in_specs=[pl.BlockSpec(memory_space=pltpu.SMEM)] + [pl.BlockSpec(b.shape, lambda: (0, 0)) for b in bidx],
        out_specs=[pl.BlockSpec((nheads,) + b.shape, lambda: (0, 0, 0)) for b in bidx],
        name="bias_tables",
    )(rel_bias, *bidx)


QKV_TB = 512


def _qkv_prompt_kernel(x_ref, g_ref, w_ref, p4_ref, p16_ref,
                       q1_ref, k1_ref, v1_ref, kf_ref, vf_ref,
                       q4_ref, k4_ref, v4_ref, q16_ref, k16_ref, v16_ref,
                       qb_ref, qbs_ref, kb_ref, vb_ref, vbs_ref, kbf_ref, vbf_ref):
    xn = _rms(x_ref[0], g_ref[...])
    h = _dot(xn.astype(MXU_DTYPE), w_ref[...])
    c0, c1, c2, c3 = 0, W_A, 2 * W_A, 3 * W_A
    c4, c5, c6 = c3 + W_BQ, c3 + W_BQ + W_BKV, c3 + W_BQ + 2 * W_BKV
    c7 = c6 + W_BQ
    ha = h[:, :c3].astype(MXU_DTYPE)
    q1_ref[0] = ha[:, c0:c1]
    k1_ref[0] = ha[:, c1:c2]
    v1_ref[0] = ha[:, c2:c3]
    kf_ref[0] = h[:, c1:c2]
    vf_ref[0] = h[:, c2:c3]
    qb_ref[0] = h[:, c3:c4].astype(MXU_DTYPE)
    kb_ref[0] = h[:, c4:c5].astype(MXU_DTYPE)
    vb_ref[0] = h[:, c5:c6].astype(MXU_DTYPE)
    kbf_ref[0] = h[:, c4:c5]
    vbf_ref[0] = h[:, c5:c6]
    qbs_ref[0] = h[:, c6:c7].astype(MXU_DTYPE)
    vbs_ref[0] = h[:, c7:].astype(MXU_DTYPE)
    tb = ha.shape[0]
    for d, p_ref, outs in ((4, p4_ref, (q4_ref, k4_ref, v4_ref)), (16, p16_ref, (q16_ref, k16_ref, v16_ref))):
        perm = _dot(p_ref[...], ha).astype(MXU_DTYPE)
        rows = tb // d
        for r in range(d):
            for t, o_ref in enumerate(outs):
                o_ref[0, r] = perm[r * rows:(r + 1) * rows, t * W_A:(t + 1) * W_A]


def _qkv_prompt(x, g, w_ext):
    n, s, dm = x.shape
    tb = QKV_TB
    nb = s // tb
    p4 = jnp.asarray(_residue_perm(tb, 4), MXU_DTYPE)
    p16 = jnp.asarray(_residue_perm(tb, 16), MXU_DTYPE)
    bf = MXU_DTYPE

    def nat(width, dt):
        return jax.ShapeDtypeStruct((n, s, width), dt), pl.BlockSpec((1, tb, width), lambda i, j: (i, j, 0))

    def res(d):
        return (jax.ShapeDtypeStruct((n, d, s // d, W_A), bf),
                pl.BlockSpec((1, d, tb // d, W_A), lambda i, j: (i, 0, j, 0)))

    outs = [nat(W_A, bf), nat(W_A, bf), nat(W_A, bf), nat(W_A, F32), nat(W_A, F32),
            res(4), res(4), res(4), res(16), res(16), res(16),
            nat(W_BQ, bf), nat(W_BQ, bf), nat(W_BKV, bf), nat(W_BKV, bf), nat(W_BKV, bf),
            nat(W_BKV, F32), nat(W_BKV, F32)]
    return pl.pallas_call(
        _qkv_prompt_kernel,
        grid=(n, nb),
        in_specs=[pl.BlockSpec((1, tb, dm), lambda i, j: (i, j, 0)),
                  pl.BlockSpec((1, dm), lambda i, j: (0, 0)),
                  pl.BlockSpec(w_ext.shape, lambda i, j: (0, 0)),
                  pl.BlockSpec((tb, tb), lambda i, j: (0, 0)),
                  pl.BlockSpec((tb, tb), lambda i, j: (0, 0))],
        out_shape=[o[0] for o in outs],
        out_specs=[o[1] for o in outs],
        compiler_params=_cparams(("parallel", "parallel")),
        name="qkv_prompt",
    )(x, g, w_ext, p4, p16)


def _qkv_sample_kernel(x_ref, g_ref, wt_ref, o_ref):
    xn = _rms(x_ref[...], g_ref[...])
    o_ref[...] = _dot_nt(wt_ref[...], xn.astype(MXU_DTYPE))


def _qkv_sample(x, g, w_t):
    return pl.pallas_call(
        _qkv_sample_kernel,
        out_shape=jax.ShapeDtypeStruct((w_t.shape[0], x.shape[0]), F32),
        compiler_params=_cparams(None),
        name="qkv_sample",
    )(x, g, w_t)


SWA_TILES = 4


def _swa_kernel(head_cfg, has_sink, want_lse, *refs):
    it = iter(refs)
    tab_ref = next(it)
    sink_ref = next(it) if has_sink else None
    q_refs = [next(it)]
    if any(c[0] == 1 for c in head_cfg):
        q_refs.append(next(it))
    kc_ref, kp_ref = next(it), next(it)
    v_refs = [(next(it), next(it))]
    if any(c[3] == 1 for c in head_cfg):
        v_refs.append((next(it), next(it)))
    o_ref = next(it)
    lse_ref = next(it) if want_lse else None

    first = pl.program_id(1) == 0
    col = lax.broadcasted_iota(jnp.int32, (SLOTS, 2 * SLOTS), 1)
    prev_pen = jnp.where((col < SLOTS) & first, NEG_INF, 0.0)
    lane = lax.broadcasted_iota(jnp.int32, (SLOTS, LANES), 1)
    low = lane < HEAD_DIM
    for t in range(kc_ref.shape[1] // SLOTS):
        rows = slice(t * SLOTS, (t + 1) * SLOTS)
        before = slice((t - 1) * SLOTS, t * SLOTS)

        def keys(cur_ref, prev_ref, lanes):
            prev = prev_ref[0, :, lanes] if t == 0 else cur_ref[0, before, lanes]
            return jnp.concatenate([prev, cur_ref[0, rows, lanes]], axis=0)

        lse_acc = jnp.zeros((SLOTS, LANES), F32)
        for p in range(len(head_cfg) // 2):
            halves = []
            for hh in range(2):
                h = 2 * p + hh
                q_src, q_half, k_tile, v_src, v_tile = head_cfg[h]
                q = q_refs[q_src][0, rows, p * LANES:(p + 1) * LANES]
                q = jnp.where(low if q_half == 0 else jnp.logical_not(low), q, jnp.zeros_like(q))
                kcat = keys(kc_ref, kp_ref, slice(k_tile * LANES, (k_tile + 1) * LANES))
                s = _dot_nt(q, kcat) + tab_ref[h]
                if t == 0:
                    s = s + prev_pen
                m = jnp.max(s, axis=-1, keepdims=True)
                if has_sink:
                    m = jnp.maximum(m, sink_ref[h])
                e = jnp.exp(s - m)
                l = jnp.sum(e, axis=-1, keepdims=True)
                if has_sink:
                    l = l + jnp.exp(sink_ref[h] - m)
                vc_ref, vp_ref = v_refs[v_src]
                vcat = keys(vc_ref, vp_ref, slice(v_tile * LANES, (v_tile + 1) * LANES))
                halves.append(_dot(e.astype(MXU_DTYPE), vcat) / l)
                if want_lse:
                    lse_acc = jnp.where(lane == h, m + jnp.log(l), lse_acc)
            o_ref[0, rows, p * LANES:(p + 1) * LANES] = jnp.where(low, halves[0], halves[1]).astype(o_ref.dtype)
        if want_lse:
            lse_ref[0, rows] = lse_acc


def _swa(tab, q_list, k, v_list, head_cfg, sinks=None, want_lse=True):
    r, s, _ = q_list[0].shape
    ck = k.shape[-1]
    tiles = math.gcd(SWA_TILES, s // SLOTS)
    rows = tiles * SLOTS
    assert s % rows == 0
    nb = s // rows
    cur = lambda i, j: (i, j, 0)
    prev = lambda i, j: (i, jnp.maximum(tiles * j - 1, 0), 0)
    in_specs = [pl.BlockSpec(tab.shape, lambda i, j: (0, 0, 0))]
    args = [tab]
    if sinks is not None:
        in_specs.append(pl.BlockSpec(memory_space=pltpu.SMEM))
        args.append(sinks)
    for q in q_list:
        in_specs.append(pl.BlockSpec((1, rows, q.shape[-1]), cur))
        args.append(q)
    in_specs += [pl.BlockSpec((1, rows, ck), cur), pl.BlockSpec((1, SLOTS, ck), prev)]
    args += [k, k]
    for v in v_list:
        in_specs += [pl.BlockSpec((1, rows, ck), cur), pl.BlockSpec((1, SLOTS, ck), prev)]
        args += [v, v]
    out_shape = [jax.ShapeDtypeStruct((r, s, W_A), MXU_DTYPE)]
    out_specs = [pl.BlockSpec((1, rows, W_A), cur)]
    if want_lse:
        out_shape.append(jax.ShapeDtypeStruct((r, s, LANES), F32))
        out_specs.append(pl.BlockSpec((1, rows, LANES), cur))
    out = pl.pallas_call(
        functools.partial(_swa_kernel, tuple(head_cfg), sinks is not None, want_lse),
        grid=(r, nb),
        in_specs=in_specs,
        out_shape=out_shape,
        out_specs=out_specs,
        compiler_params=_cparams(("parallel", "arbitrary")),
        name="swa",
    )(*args)
    return out if want_lse else out[0]


HEAD_CFG_A = tuple((0, h % 2, h // 2, 0, h // 2) for h in range(H_A))


def _head_cfg_b():
    cfg = []
    for h in range(H_B):
        c = h // G_B
        src = 0 if h % 2 == c else 1
        cfg.append((src, c, 0, src, 0))
    return tuple(cfg)


HEAD_CFG_B = _head_cfg_b()


def _shift_in(x, new_col):
    length = x.shape[-1]
    rolled = pltpu.roll(x, length - 1, axis=1)
    lane = lax.broadcasted_iota(jnp.int32, x.shape, 1)
    return jnp.where(lane == length - 1, new_col, rolled)


def _col_attention(q, kmat, vmat, k_new, v_new, bias_row, mult_row, bias_new, mult_new, sink):
    s = jnp.sum(q * kmat, axis=0, keepdims=True) + bias_row
    s_new = jnp.sum(q * k_new, axis=0, keepdims=True) + bias_new
    if mult_row is not None:
        s = jnp.where(mult_row > 0.0, s, NEG_INF)
    m = jnp.maximum(jnp.max(s, axis=-1, keepdims=True), s_new)
    if sink is not None:
        m = jnp.maximum(m, sink)
    e = jnp.exp(s - m)
    if mult_row is not None:
        e = e * mult_row
    e_new = mult_new * jnp.exp(s_new - m)
    l = jnp.sum(e, axis=-1, keepdims=True) + e_new
    if sink is not None:
        l = l + jnp.exp(sink - m)
    o = jnp.sum(vmat * e, axis=-1, keepdims=True) + v_new * e_new
    return o / l


SAMPLE_HEADS = H_A // KV_B
assert SAMPLE_HEADS == G_B


def _column(ht_ref, row0, nrows, pick):
    return jnp.sum(jnp.where(pick, ht_ref[pl.ds(row0, nrows), :], 0.0), axis=1, keepdims=True)


def _sample_kernel(bias0a_ref, bias0b_ref, sink_ref, taba_ref, mult_ref, tabb_ref, ht_ref,
                   ka_ref, va_ref, kb_ref, vb_ref, ot_ref, kao_ref, vao_ref, kbo_ref, vbo_ref):
    n, hb = pl.program_id(0), pl.program_id(1)
    lane = n % LANES
    pick = lax.broadcasted_iota(jnp.int32, (1, LANES), 1) == lane
    rows = SAMPLE_HEADS * HEAD_DIM
    blk = pl.multiple_of(hb * rows, rows)
    kvr = pl.multiple_of(hb * HEAD_DIM, HEAD_DIM)

    @pl.when((lane == 0) & (hb == 0))
    def _():
        ot_ref[...] = jnp.zeros_like(ot_ref)

    def put(row0, col):
        ot_ref[pl.ds(row0, rows), :] = jnp.where(pick, col, ot_ref[pl.ds(row0, rows), :])

    q, k_new, v_new = (_column(ht_ref, c0 + blk, rows, pick) for c0 in (0, W_A, 2 * W_A))
    mult = mult_ref[...]
    outs = []
    for hl in range(SAMPLE_HEADS):
        r = slice(hl * HEAD_DIM, (hl + 1) * HEAD_DIM)
        kmat, vmat = ka_ref[0, hl], va_ref[0, hl]
        outs.append(_col_attention(q[r], kmat, vmat, k_new[r], v_new[r], taba_ref[hl], mult,
                                   bias0a_ref[hb * SAMPLE_HEADS + hl], float(len(DILATIONS)), None))
        kao_ref[0, hl] = _shift_in(kmat, k_new[r])
        vao_ref[0, hl] = _shift_in(vmat, v_new[r])
    put(blk, jnp.concatenate(outs, axis=0))

    c3 = 3 * W_A
    q = _column(ht_ref, c3 + blk, rows, pick)
    k_new = _column(ht_ref, c3 + W_BQ + kvr, HEAD_DIM, pick)
    v_new = _column(ht_ref, c3 + W_BQ + W_BKV + kvr, HEAD_DIM, pick)
    kmat, vmat = kb_ref[0, 0], vb_ref[0, 0]
    outs = []
    for g in range(G_B):
        h = hb * G_B + g
        outs.append(_col_attention(q[g * HEAD_DIM:(g + 1) * HEAD_DIM], kmat, vmat, k_new, v_new, tabb_ref[g], None,
                                   bias0b_ref[h], 1.0, sink_ref[h]))
    put(W_A + blk, jnp.concatenate(outs, axis=0))
    kbo_ref[0, 0] = _shift_in(kmat, k_new)
    vbo_ref[0, 0] = _shift_in(vmat, v_new)


def _sample_specs(bias0_a, bias0_b, sinks, tab_a, mult_a, tab_b, h_t, ka_t, va_t, kb_t, vb_t):
    ns, nh, hd, la = ka_t.shape
    _, nkv, _, lb = kb_t.shape
    assert nh // SAMPLE_HEADS == nkv and ns % LANES == 0
    smem = pl.BlockSpec(memory_space=pltpu.SMEM)
    lanes_of = lambda rows: pl.BlockSpec((rows, LANES), lambda i, j: (0, i // LANES))
    a_spec = pl.BlockSpec((1, SAMPLE_HEADS, hd, la), lambda i, j: (i, j, 0, 0))
    b_spec = pl.BlockSpec((1, 1, hd, lb), lambda i, j: (i, j, 0, 0))
    in_specs = [smem, smem, smem,
                pl.BlockSpec((SAMPLE_HEADS, 1, la), lambda i, j: (j, 0, 0)),
                pl.BlockSpec((1, la), lambda i, j: (0, 0)),
                pl.BlockSpec((G_B, 1, lb), lambda i, j: (j, 0, 0)),
                lanes_of(h_t.shape[0]), a_spec, a_spec, b_spec, b_spec]
    out_specs = [lanes_of(W_A + W_BQ), a_spec, a_spec, b_spec, b_spec]
    shapes = [jax.ShapeDtypeStruct((W_A + W_BQ, ns), F32)] + [jax.ShapeDtypeStruct(c.shape, F32)
                                                              for c in (ka_t, va_t, kb_t, vb_t)]
    return in_specs, out_specs, shapes


TAIL_TB = 512


def _unpermute(pt_ref, blocks_ref, exact):
    d = blocks_ref.shape[1]
    x = jnp.concatenate([blocks_ref[0, r] for r in range(d)], axis=0)
    if not exact:
        return _dot(pt_ref[...], x)
    return sum(_dot(pt_ref[...], part) for part in _split3(x))


def _tail_prompt_kernel(x_ref, o1_ref, l1_ref, o4_ref, l4_ref, o16_ref, l16_ref, ob_ref, pt4_ref, pt16_ref,
                        w_ref, g_ref, h_ref, xn_ref):
    o_g = [o1_ref[0].astype(F32), _unpermute(pt4_ref, o4_ref, False), _unpermute(pt16_ref, o16_ref, False)]
    l_g = [l1_ref[0], _unpermute(pt4_ref, l4_ref, True), _unpermute(pt16_ref, l16_ref, True)]
    m = jnp.maximum(jnp.maximum(l_g[0], l_g[1]), l_g[2])
    e_g = [jnp.exp(l - m) for l in l_g]
    den = e_g[0] + e_g[1] + e_g[2]
    w_g = [e / den for e in e_g]
    lane = lax.broadcasted_iota(jnp.int32, (x_ref.shape[1], LANES), 1)
    low = lane < HEAD_DIM
    parts = []
    for p in range(H_A // 2):
        acc = None
        for w, o in zip(w_g, o_g):
            wp = jnp.where(low, w[:, 2 * p:2 * p + 1], w[:, 2 * p + 1:2 * p + 2])
            term = wp * o[:, p * LANES:(p + 1) * LANES]
            acc = term if acc is None else acc + term
        parts.append(acc.astype(MXU_DTYPE))
    o = jnp.concatenate(parts + [ob_ref[0]], axis=-1)
    h = x_ref[0] + _dot(o, w_ref[...])
    h_ref[0] = h
    xn_ref[0] = _rms(h, g_ref[...])


def _tail_prompt(x, o1, l1, o4, l4, o16, l16, ob, w_out, g):
    n, s, dm = x.shape
    tb = TAIL_TB
    pt4 = jnp.asarray(_residue_perm(tb, 4).T, MXU_DTYPE)
    pt16 = jnp.asarray(_residue_perm(tb, 16).T, MXU_DTYPE)
    nat = lambda w: pl.BlockSpec((1, tb, w), lambda i, j: (i, j, 0))
    res = lambda d, w: pl.BlockSpec((1, d, tb // d, w), lambda i, j: (i, 0, j, 0))
    const = lambda a: pl.BlockSpec(a.shape, lambda i, j: (0,) * a.ndim)
    return pl.pallas_call(
        _tail_prompt_kernel,
        grid=(n, s // tb),
        in_specs=[nat(dm), nat(W_A), nat(LANES), res(4, W_A), res(4, LANES), res(16, W_A), res(16, LANES),
                  nat(W_BQ), const(pt4), const(pt16), const(w_out), const(g)],
        out_shape=[jax.ShapeDtypeStruct((n, s, dm), F32), jax.ShapeDtypeStruct((n, s, dm), F32)],
        out_specs=[nat(dm), nat(dm)],
        compiler_params=_cparams(("parallel", "parallel")),
        name="tail_prompt",
    )(x, o1, l1, o4, l4, o16, l16, ob, pt4, pt16, w_out, g)


def _tail_sample_kernel(x_ref, ot_ref, w_ref, g_ref, h_ref, xn_ref):
    h = x_ref[...] + _dot(ot_ref[...].T.astype(MXU_DTYPE), w_ref[...])
    h_ref[...] = h
    xn_ref[...] = _rms(h, g_ref[...])


def _tail_sample(x, o_t, w_out, g):
    return pl.pallas_call(
        _tail_sample_kernel,
        out_shape=[jax.ShapeDtypeStruct(x.shape, F32), jax.ShapeDtypeStruct(x.shape, F32)],
        compiler_params=_cparams(None),
        name="tail_sample",
    )(x, o_t, w_out, g)


def _sorting_network(n):
    size = 1 << (n - 1).bit_length()

    def merge(lo, hi, r):
        step = r * 2
        if step < hi - lo:
            yield from merge(lo, hi, step)
            yield from merge(lo + r, hi, step)
            yield from ((i, i + r) for i in range(lo + r, hi - r, step))
        else:
            yield (lo, lo + r)

    def sort(lo, hi):
        if hi - lo >= 1:
            mid = lo + (hi - lo) // 2
            yield from sort(lo, mid)
            yield from sort(mid + 1, hi)
            yield from merge(lo, hi, 1)

    return [(i, j) for i, j in sort(0, size - 1) if j < n]


def _top_rows(s, count):
    tiles = [s[v * SUBLANES:(v + 1) * SUBLANES] for v in range(s.shape[0] // SUBLANES)]
    for i, j in _sorting_network(len(tiles)):
        tiles[i], tiles[j] = jnp.maximum(tiles[i], tiles[j]), jnp.minimum(tiles[i], tiles[j])
    rows = []
    for t in range(count):
        m = jnp.max(tiles[0], axis=0, keepdims=True)
        rows.append(m)
        if t + 1 < count:
            popped = tiles[0] == m
            depth = min(len(tiles), count - t - 1)
            for k in range(depth):
                below = tiles[k + 1] if k + 1 < len(tiles) else NEG_INF
                tiles[k] = jnp.where(popped, below, tiles[k])
    return rows


def _pad_rows(rows, count):
    pad = [jnp.full_like(rows[0], NEG_INF)] * (count - len(rows))
    return jnp.concatenate(list(rows) + pad, axis=0)


def _candidate_sums(top1, top2):
    k = PEER_TOPK + 1
    wide = -(-k // SUBLANES) * SUBLANES
    narrow = -(-(k // 2) // SUBLANES) * SUBLANES
    assert k // (narrow + 1) <= 1
    v2_wide = _pad_rows(top2, wide)
    v2_narrow = v2_wide[:narrow]
    parts = [top1[0] + v2_wide] + [top1[a] + v2_narrow for a in range(1, narrow)]
    parts.append(_pad_rows(top1[narrow:], -(-(k - narrow) // SUBLANES) * SUBLANES) + top2[0])
    return jnp.concatenate(parts, axis=0)


def _route_kernel(x_ref, wqh_ref, wql_ref, skh_ref, skl_ref, thr_ref, p1_ref, p2_ref):
    xh, xl = _split2(x_ref[...])
    wqh = wqh_ref[...]
    q_t = _dot_nt(wqh, xh) + _dot_nt(wqh, xl) + _dot_nt(wql_ref[...], xh)
    half = D_KEY // 2
    for h in range(PEER_HEADS):
        scores = []
        for c in range(2):
            qh, ql = _split2(q_t[h * D_KEY + c * half:h * D_KEY + (c + 1) * half, :])
            scores.append(_dot(skh_ref[c], qh) + _dot(skh_ref[c], ql) + _dot(skl_ref[c], qh))
        for g in range(q_t.shape[1] // LANES):
            s1, s2 = (s[:, g * LANES:(g + 1) * LANES] for s in scores)
            top1, top2 = _top_rows(s1, PEER_TOPK + 1), _top_rows(s2, PEER_TOPK + 1)
            best = _top_rows(_candidate_sums(top1, top2), PEER_TOPK + 1)
            tau = 0.5 * (best[PEER_TOPK - 1] + best[PEER_TOPK])
            z = sum(jnp.exp(b - best[0]) for b in best[:PEER_TOPK])
            log_norm = best[0] + jnp.log(z)
            m2 = top2[0]
            thr_ref[h, g] = jnp.exp((tau - m2) - s1)
            p1_ref[h, g] = jnp.exp(s1 + (m2 - log_norm))
            p2_ref[h, g] = jnp.exp(s2 - m2)


def _route(xn, wq_t_hi, wq_t_lo, sk_hi, sk_lo, tb):
    t, dm = xn.shape
    const = lambda a: pl.BlockSpec(a.shape, lambda i: (0,) * a.ndim)
    gspec = pl.BlockSpec((PEER_HEADS, tb // LANES, N_KEYS, LANES), lambda i: (0, i, 0, 0))
    gshape = jax.ShapeDtypeStruct((PEER_HEADS, t // LANES, N_KEYS, LANES), F32)
    return pl.pallas_call(
        _route_kernel,
        grid=(t // tb,),
        in_specs=[pl.BlockSpec((tb, dm), lambda i: (i, 0)), const(wq_t_hi), const(wq_t_lo), const(sk_hi), const(sk_lo)],
        out_shape=[gshape] * 3,
        out_specs=[gspec] * 3,
        compiler_params=_cparams(("parallel",)),
        name="peer_route",
    )(xn, wq_t_hi, wq_t_lo, sk_hi, sk_lo)


EXPERT_CHUNK = SUBLANES * N_KEYS
GATE_JBLOCK = 2 * SUBLANES
INV_SQRT2 = 0.7071067811865476


def _gate_rows(thr_ref, p1_ref, p2_ref, w_ref):
    nv = GATE_JBLOCK // SUBLANES

    def row_group(i8, carry):
        base = pl.multiple_of(i8 * SUBLANES, SUBLANES)
        thr_t = [thr_ref[h, 0, pl.ds(base, SUBLANES), :] for h in range(PEER_HEADS)]
        p1_t = [p1_ref[h, 0, pl.ds(base, SUBLANES), :] for h in range(PEER_HEADS)]
        for jb in range(N_KEYS // GATE_JBLOCK):
            w = [[None] * nv for _ in range(SUBLANES)]
            for h in range(PEER_HEADS):
                p2 = [p2_ref[h, 0, pl.ds(jb * GATE_JBLOCK + v * SUBLANES, SUBLANES), :] for v in range(nv)]
                for r in range(SUBLANES):
                    thr = jnp.broadcast_to(thr_t[h][r:r + 1, :], (SUBLANES, LANES))
                    p1 = jnp.broadcast_to(p1_t[h][r:r + 1, :], (SUBLANES, LANES))
                    for v in range(nv):
                        term = jnp.where(p2[v] >= thr, p1 * p2[v], 0.0)
                        w[r][v] = term if w[r][v] is None else w[r][v] + term
            for r in range(SUBLANES):
                row0 = pl.multiple_of((base + r) * N_KEYS + jb * GATE_JBLOCK, GATE_JBLOCK)
                w_ref[0, pl.ds(row0, GATE_JBLOCK), :] = jnp.concatenate(w[r], axis=0).astype(w_ref.dtype)
        return carry

    lax.fori_loop(0, thr_ref.shape[2] // SUBLANES, row_group, 0)


GATE_ROWS = N_KEYS // 2


def _gate_specs(t):
    rows = pl.BlockSpec((PEER_HEADS, 1, GATE_ROWS, LANES), lambda i, j: (0, i, j, 0))
    full = pl.BlockSpec((PEER_HEADS, 1, N_KEYS, LANES), lambda i, j: (0, i, 0, 0))
    out = pl.BlockSpec((1, GATE_ROWS * N_KEYS, LANES), lambda i, j: (i, j, 0))
    shape = jax.ShapeDtypeStruct((t // LANES, N_KEYS * N_KEYS, LANES), MXU_DTYPE)
    return [rows, rows, full], out, shape


def _gates(thr, p1, p2):
    t = thr.shape[1] * LANES
    in_specs, out_spec, shape = _gate_specs(t)
    return pl.pallas_call(
        _gate_rows,
        grid=(t // LANES, N_KEYS // GATE_ROWS),
        in_specs=in_specs, out_specs=out_spec, out_shape=shape,
        compiler_params=_cparams(("parallel", "parallel")),
        name="peer_gates",
    )(thr, p1, p2)


def _sample_and_gates_kernel(*refs):
    n_in, n_gate = 11, 3
    sample_in, gate_in = refs[:n_in], refs[n_in:n_in + n_gate]
    sample_out, gate_out = refs[n_in + n_gate:-1], refs[-1]
    _sample_kernel(*sample_in, *sample_out)
    _gate_rows(*gate_in, gate_out)


def _sample_and_gates(sample_args, thr, p1, p2):
    t = thr.shape[1] * LANES
    ns, nkv = sample_args[7].shape[0], sample_args[9].shape[1]
    s_in, s_out, s_shapes = _sample_specs(*sample_args)
    if t // LANES != ns or N_KEYS // GATE_ROWS != nkv:
        outs = pl.pallas_call(
            _sample_kernel, grid=(ns, nkv), in_specs=s_in, out_specs=s_out, out_shape=s_shapes,
            compiler_params=_cparams(("arbitrary", "arbitrary")), name="sample_attn",
        )(*sample_args)
        return list(outs) + [_gates(thr, p1, p2)]
    g_in, g_out, g_shape = _gate_specs(t)
    return pl.pallas_call(
        _sample_and_gates_kernel,
        grid=(ns, nkv),
        in_specs=s_in + g_in, out_specs=s_out + [g_out], out_shape=s_shapes + [g_shape],
        compiler_params=_cparams(("arbitrary", "arbitrary")),
        name="sample_attn_and_gates",
    )(*sample_args, thr, p1, p2)


def _activate(g, a_ref, p_ref, w_ref):
    blk = 8 * SUBLANES
    for b in range(a_ref.shape[1] // blk):
        rows = pl.ds(b * blk, blk)
        a = a_ref[g, rows, :]
        half = 0.5 * a
        act = half + half * lax.erf(a * INV_SQRT2)
        p_ref[g, rows, :] = w_ref[g, rows, :] * act.astype(p_ref.dtype)


def _experts_kernel(xn_ref, res_ref, g_ref, down_ref, upt_ref, w_a_ref, w_b_ref,
                    y_ref, xb_ref, a0_ref, a1_ref, pb0_ref, pb1_ref, acc_ref):
    i, j = pl.program_id(0), pl.program_id(1)
    ch = EXPERT_CHUNK
    ng = a0_ref.shape[0]

    @pl.when((i == 0) & (j == 0))
    def _():
        for ref in (a0_ref, a1_ref, pb0_ref, pb1_ref):
            ref[...] = jnp.zeros_like(ref)

    @pl.when(j == 0)
    def _():
        xb_ref[...] = xn_ref[...].astype(MXU_DTYPE)

    keep = j >= 1

    def tick(half, a_in_ref, a_out_ref, p_out_ref, p_in_ref, w_ref):
        p_in = jnp.concatenate([p_in_ref[gg] for gg in range(ng)], axis=1)
        upd = _dot(upt_ref[:, half * ch:(half + 1) * ch], p_in)
        nxt = _dot_nt(down_ref[pl.ds(half * ch, ch), :], xb_ref[...])
        for g in range(ng):
            a_out_ref[g] = nxt[:, g * LANES:(g + 1) * LANES]
            _activate(g, a_in_ref, p_out_ref, w_ref)
        return upd

    upd = tick(0, a1_ref, a0_ref, pb1_ref, pb0_ref, w_a_ref)
    upd = upd + tick(1, a0_ref, a1_ref, pb0_ref, pb1_ref, w_b_ref)
    acc_ref[...] = jnp.where(keep, acc_ref[...] + upd, 0.0)

    @pl.when(j == pl.num_programs(1) - 1)
    def _():
        y = res_ref[...] + acc_ref[...].T
        y_ref[...] = _rms(y, g_ref[...])


def _experts(xn, res, g, down, up_t, w, tb):
    t, dm = xn.shape
    ne = down.shape[0]
    step = 2 * EXPERT_CHUNK
    nj = ne // step
    ng = tb // LANES
    tok = pl.BlockSpec((tb, dm), lambda i, j: (i, 0))
    w_a = pl.BlockSpec((ng, EXPERT_CHUNK, LANES), lambda i, j: (i, jnp.maximum(2 * j - 1, 0), 0))
    w_b = pl.BlockSpec((ng, EXPERT_CHUNK, LANES), lambda i, j: (i, jnp.minimum(2 * j, 2 * nj - 1), 0))
    return pl.pallas_call(
        _experts_kernel,
        grid=(t // tb, nj + 1),
        in_specs=[tok, tok, pl.BlockSpec((1, dm), lambda i, j: (0, 0)),
                  pl.BlockSpec((step, dm), lambda i, j: (jnp.minimum(j, nj - 1), 0)),
                  pl.BlockSpec((dm, step), lambda i, j: (0, jnp.maximum(j - 1, 0))),
                  w_a, w_b],
        out_shape=jax.ShapeDtypeStruct((t, dm), F32),
        out_specs=tok,
        scratch_shapes=[pltpu.VMEM((tb, dm), MXU_DTYPE),
                        pltpu.VMEM((ng, EXPERT_CHUNK, LANES), F32), pltpu.VMEM((ng, EXPERT_CHUNK, LANES), F32),
                        pltpu.VMEM((ng, EXPERT_CHUNK, LANES), MXU_DTYPE),
                        pltpu.VMEM((ng, EXPERT_CHUNK, LANES), MXU_DTYPE),
                        pltpu.VMEM((dm, tb), F32)],
        compiler_params=_cparams(("arbitrary", "arbitrary")),
        name="peer_experts",
    )(xn, res, g, down, up_t, w, w)


PEER_TB = 512


def _cache_to_feature_major(c):
    return jnp.transpose(c, (0, 2, 3, 1))


def _cache_from_feature_major(c):
    return jnp.transpose(c, (0, 3, 1, 2))


def _layer(xp, xs, cache_a_k, cache_a_v, cache_b_k, cache_b_v, norm_attn, w_in, rel_bias, sinks, w_out, norm_ffn,
           w_peer_q, peer_sub_keys, peer_down, peer_up, g_final):
    n, s, dm = xp.shape
    ns = xs.shape[0]
    la, lb = cache_a_k.shape[1], cache_b_k.shape[1]

    c3 = 3 * W_A
    w_q_scaled = jnp.concatenate([w_in[:, :W_A] * SCALE, w_in[:, W_A:c3], w_in[:, c3:c3 + W_BQ] * SCALE,
                                  w_in[:, c3 + W_BQ:]], axis=1)
    qb_cols = w_q_scaled[:, c3:c3 + W_BQ].reshape(dm, H_B // 2, 2, HEAD_DIM)[:, :, ::-1].reshape(dm, W_BQ)
    vb_cols = w_in[:, c3 + W_BQ + W_BKV:].reshape(dm, KV_B, HEAD_DIM)[:, ::-1].reshape(dm, W_BKV)
    w_ext = jnp.concatenate([w_q_scaled, qb_cols, vb_cols], axis=1).astype(MXU_DTYPE)
    w_nat_t = w_q_scaled.T.astype(MXU_DTYPE)
    w_out_b = w_out.astype(MXU_DTYPE)
    g_attn, g_ffn = norm_attn[None, :], norm_ffn[None, :]
    wq_t = w_peer_q.T
    wq_t_hi = wq_t.astype(MXU_DTYPE)
    wq_t_lo = (wq_t - wq_t_hi.astype(F32)).astype(MXU_DTYPE)
    sk_hi = peer_sub_keys.astype(MXU_DTYPE)
    sk_lo = (peer_sub_keys - sk_hi.astype(F32)).astype(MXU_DTYPE)
    peer_w = (wq_t_hi, wq_t_lo, sk_hi, sk_lo, peer_down.astype(MXU_DTYPE), peer_up.T.astype(MXU_DTYPE))

    bidx_sa, mult_sa = _sample_tables(la)
    bidx_sb, _ = _sample_tables(lb)
    assert H_A == H_B
    *tabs_a, tab_b, tab_sa, tab_sb = _bias_tables(
        rel_bias, H_A, [(_window_bucket_matrix(d), 0) for d in DILATIONS]
        + [(_window_bucket_matrix(1), H_A), (bidx_sa, 0), (bidx_sb, H_A)])
    bias0_a, bias0_b = rel_bias[0, :H_A], rel_bias[0, H_A:]

    (q1, k1, v1, kf, vf, q4, k4, v4, q16, k16, v16, qb, qbs, kb, vb, vbs, kbf, vbf) = _qkv_prompt(xp, g_attn, w_ext)
    o1, l1 = _swa(tabs_a[0], [q1], k1, [v1], HEAD_CFG_A)
    flat = lambda a: a.reshape((a.shape[0] * a.shape[1],) + a.shape[2:])
    o4, l4 = _swa(tabs_a[1], [flat(q4)], flat(k4), [flat(v4)], HEAD_CFG_A)
    o16, l16 = _swa(tabs_a[2], [flat(q16)], flat(k16), [flat(v16)], HEAD_CFG_A)
    ob = _swa(tab_b, [qb, qbs], kb, [vb, vbs], HEAD_CFG_B, sinks=sinks, want_lse=False)
    unflat = lambda a, d: a.reshape((n, d) + a.shape[1:])
    hp, xnp = _tail_prompt(xp, o1, l1, unflat(o4, 4), unflat(l4, 4), unflat(o16, 16), unflat(l16, 16), ob,
                           w_out_b, g_ffn)
    prompt_caches = (kf[:, s - min(la, s):].reshape(n, -1, H_A, HEAD_DIM),
                     vf[:, s - min(la, s):].reshape(n, -1, H_A, HEAD_DIM),
                     kbf[:, s - min(lb, s):].reshape(n, -1, KV_B, HEAD_DIM),
                     vbf[:, s - min(lb, s):].reshape(n, -1, KV_B, HEAD_DIM))

    hp, xnp = hp.reshape(n * s, dm), xnp.reshape(n * s, dm)
    route_w, (down_b, up_t_b) = peer_w[:4], peer_w[4:]
    routing_p = _route(xnp, *route_w, PEER_TB)

    hs_t = _qkv_sample(xs, g_attn, w_nat_t)
    sample_args = (bias0_a, bias0_b, sinks, tab_sa, jnp.asarray(mult_sa), tab_sb, hs_t,
                   _cache_to_feature_major(cache_a_k), _cache_to_feature_major(cache_a_v),
                   _cache_to_feature_major(cache_b_k), _cache_to_feature_major(cache_b_v))
    o_t, *shifted, gates_p = _sample_and_gates(sample_args, *routing_p)
    h_s, xn_s = _tail_sample(xs, o_t, w_out_b, g_ffn)
    sample_caches = tuple(_cache_from_feature_major(c) for c in shifted)

    y_p = _experts(xnp, hp, g_final, down_b, up_t_b, gates_p, PEER_TB)
    gates_s = _gates(*_route(xn_s, *route_w, LANES))
    y_s = _experts(xn_s, h_s, g_final, down_b, up_t_b, gates_s, LANES)
    return y_p.reshape(n, s, dm), y_s, prompt_caches, sample_caches


def kernel(x_prompt, x_sample, cache_a_k, cache_a_v, cache_b_k, cache_b_v, norm_attn, w_in, rel_bias, sinks, w_out,
           norm_ffn, w_peer_q, peer_sub_keys, peer_down, peer_up, norm_final):
    depth = w_in.shape[0]
    assert depth == 1, "single-layer trunk"
    n, s, dm = x_prompt.shape
    ns = x_sample.shape[0]
    assert x_sample.shape[1] == 1 and s % QKV_TB == 0 and (n * s) % PEER_TB == 0 and ns % LANES == 0
    l = 0
    y_prompt, y_sample, prompt_caches, sample_caches = _layer(
        x_prompt, x_sample[:, 0], cache_a_k[l], cache_a_v[l], cache_b_k[l], cache_b_v[l], norm_attn[l], w_in[l],
        rel_bias, sinks[l], w_out[l], norm_ffn[l], w_peer_q[l], peer_sub_keys[l], peer_down[l], peer_up[l],
        norm_final[None, :])
    return ((y_prompt, y_sample.reshape(ns, 1, dm)) + tuple(c[None] for c in prompt_caches)
            + tuple(c[None] for c in sample_caches))
```

```python
import functools
import math

import numpy as np
import jax
import jax.numpy as jnp
from jax import lax
from jax.experimental import pallas as pl
from jax.experimental.pallas import tpu as pltpu

HEAD_DIM = 64
H_A = 8
H_B = 8
KV_B = 2
G_B = H_B // KV_B
DILATIONS = (1, 4, 16)
SLOTS = 128
N_BUCKETS = 32
MAX_DISTANCE = 2048
N_KEYS = 128
PEER_HEADS = 8
PEER_TOPK = 16
D_KEY = 128
RMS_EPS = 1e-6
SCALE = HEAD_DIM ** -0.5
W_A = H_A * HEAD_DIM
W_BQ = H_B * HEAD_DIM
W_BKV = KV_B * HEAD_DIM

LANES = 128
SUBLANES = 8
VMEM_LIMIT_BYTES = 56 * 1024 * 1024

MXU_DTYPE = jnp.bfloat16
F32 = jnp.float32
NEG_INF = float("-inf")


def _cparams(sem):
    return pltpu.CompilerParams(dimension_semantics=sem, vmem_limit_bytes=VMEM_LIMIT_BYTES)


def _t5_bucket_np(dist):
    exact = N_BUCKETS // 2
    d = np.maximum(dist, 0)
    logd = np.log(np.maximum(d, 1).astype(np.float32) / np.float32(exact)) / np.float32(math.log(MAX_DISTANCE / exact))
    large = np.minimum(exact + (logd * np.float32(N_BUCKETS - exact)).astype(np.int32), N_BUCKETS - 1)
    return np.where(d < exact, d, large).astype(np.int32)


def _window_bucket_matrix(dilation):
    i = np.arange(SLOTS)[:, None]
    j = np.arange(2 * SLOTS)[None, :]
    dist = i - j + SLOTS
    ok = (dist >= 0) & (dist <= SLOTS)
    return np.where(ok, _t5_bucket_np(dist * dilation), -1).astype(np.int32)


def _sample_tables(cache_len):
    delta = cache_len - np.arange(cache_len)
    mult = np.zeros(cache_len, np.float32)
    for d in DILATIONS:
        mult += ((delta % d == 0) & (delta <= SLOTS * d)).astype(np.float32)
    return _t5_bucket_np(delta)[None, :], mult[None, :]


def _residue_perm(tb, d):
    p = np.zeros((tb, tb), np.float32)
    rows = np.arange(tb)
    p[rows, (rows % (tb // d)) * d + rows // (tb // d)] = 1.0
    return p


def _rms(x, g):
    return x * lax.rsqrt(jnp.mean(x * x, axis=-1, keepdims=True) + RMS_EPS) * g


def _dot(a, b):
    return jnp.dot(a, b, preferred_element_type=F32)


def _dot_nt(a, b):
    return lax.dot_general(a, b, (((1,), (1,)), ((), ())), preferred_element_type=F32)


def _split2(x):
    hi = x.astype(MXU_DTYPE)
    lo = (x - hi.astype(F32)).astype(MXU_DTYPE)
    return hi, lo


def _split3(x):
    hi = x.astype(MXU_DTYPE)
    r1 = x - hi.astype(F32)
    mid = r1.astype(MXU_DTYPE)
    lo = (r1 - mid.astype(F32)).astype(MXU_DTYPE)
    return hi, mid, lo


def _bias_kernel(col0s, nheads, rb_ref, *refs):
    n = len(col0s)
    for col0, bidx_ref, o_ref in zip(col0s, refs[:n], refs[n:]):
        b = bidx_ref[...]
        for h in range(nheads):
            val = jnp.full(b.shape, NEG_INF, F32)
            for k in range(N_BUCKETS):
                val = jnp.where(b == k, rb_ref[k, col0 + h], val)
            o_ref[h] = val


def _bias_tables(rel_bias, nheads, tables):
    bidx = [jnp.asarray(b) for b, _ in tables]
    return pl.pallas_call(
        functools.partial(_bias_kernel, tuple(c for _, c in tables), nheads),
        out_shape=[jax.ShapeDtypeStruct((nheads,) + b.shape, F32) for b in bidx],
        in_specs=[pl.BlockSpec(memory_space=pltpu.SMEM)] + [pl.BlockSpec(b.shape, lambda: (0, 0)) for b in bidx],
        out_specs=[pl.BlockSpec((nheads,) + b.shape, lambda: (0, 0, 0)) for b in bidx],
        name="bias_tables",
    )(rel_bias, *bidx)


QKV_TB = 512
PERM_CHUNK = {4: 128, 16: 256}


def _qkv_prompt_kernel(x_ref, g_ref, w_ref, p4_ref, p16_ref,
                       q1_ref, k1_ref, v1_ref, kf_ref, vf_ref,
                       q4_ref, k4_ref, v4_ref, q16_ref, k16_ref, v16_ref,
                       qb_ref, qbs_ref, kb_ref, vb_ref, vbs_ref, kbf_ref, vbf_ref):
    xn = _rms(x_ref[0], g_ref[...])
    h = _dot(xn.astype(MXU_DTYPE), w_ref[...])
    c0, c1, c2, c3 = 0, W_A, 2 * W_A, 3 * W_A
    c4, c5, c6 = c3 + W_BQ, c3 + W_BQ + W_BKV, c3 + W_BQ + 2 * W_BKV
    c7 = c6 + W_BQ
    ha = h[:, :c3].astype(MXU_DTYPE)
    q1_ref[0] = ha[:, c0:c1]
    k1_ref[0] = ha[:, c1:c2]
    v1_ref[0] = ha[:, c2:c3]
    kf_ref[0] = h[:, c1:c2]
    vf_ref[0] = h[:, c2:c3]
    qb_ref[0] = h[:, c3:c4].astype(MXU_DTYPE)
    kb_ref[0] = h[:, c4:c5].astype(MXU_DTYPE)
    vb_ref[0] = h[:, c5:c6].astype(MXU_DTYPE)
    kbf_ref[0] = h[:, c4:c5]
    vbf_ref[0] = h[:, c5:c6]
    qbs_ref[0] = h[:, c6:c7].astype(MXU_DTYPE)
    vbs_ref[0] = h[:, c7:].astype(MXU_DTYPE)
    tb = ha.shape[0]
    for d, p_ref, outs in ((4, p4_ref, (q4_ref, k4_ref, v4_ref)), (16, p16_ref, (q16_ref, k16_ref, v16_ref))):
        chunk = p_ref.shape[0]
        per = chunk // d
        for c in range(tb // chunk):
            perm = _dot(p_ref[...], ha[c * chunk:(c + 1) * chunk]).astype(MXU_DTYPE)
            for r in range(d):
                for t, o_ref in enumerate(outs):
                    o_ref[0, r, c * per:(c + 1) * per] = perm[r * per:(r + 1) * per, t * W_A:(t + 1) * W_A]


def _qkv_prompt(x, g, w_ext):
    n, s, dm = x.shape
    tb = QKV_TB
    nb = s // tb
    p4 = jnp.asarray(_residue_perm(PERM_CHUNK[4], 4), MXU_DTYPE)
    p16 = jnp.asarray(_residue_perm(PERM_CHUNK[16], 16), MXU_DTYPE)
    bf = MXU_DTYPE

    def nat(width, dt):
        return jax.ShapeDtypeStruct((n, s, width), dt), pl.BlockSpec((1, tb, width), lambda i, j: (i, j, 0))

    def res(d):
        return (jax.ShapeDtypeStruct((n, d, s // d, W_A), bf),
                pl.BlockSpec((1, d, tb // d, W_A), lambda i, j: (i, 0, j, 0)))

    outs = [nat(W_A, bf), nat(W_A, bf), nat(W_A, bf), nat(W_A, F32), nat(W_A, F32),
            res(4), res(4), res(4), res(16), res(16), res(16),
            nat(W_BQ, bf), nat(W_BQ, bf), nat(W_BKV, bf), nat(W_BKV, bf), nat(W_BKV, bf),
            nat(W_BKV, F32), nat(W_BKV, F32)]
    return pl.pallas_call(
        _qkv_prompt_kernel,
        grid=(n, nb),
        in_specs=[pl.BlockSpec((1, tb, dm), lambda i, j: (i, j, 0)),
                  pl.BlockSpec((1, dm), lambda i, j: (0, 0)),
                  pl.BlockSpec(w_ext.shape, lambda i, j: (0, 0)),
                  pl.BlockSpec(p4.shape, lambda i, j: (0, 0)),
                  pl.BlockSpec(p16.shape, lambda i, j: (0, 0))],
        out_shape=[o[0] for o in outs],
        out_specs=[o[1] for o in outs],
        compiler_params=_cparams(("parallel", "parallel")),
        name="qkv_prompt",
    )(x, g, w_ext, p4, p16)


def _qkv_sample_kernel(x_ref, g_ref, wt_ref, o_ref):
    xn = _rms(x_ref[...], g_ref[...])
    o_ref[...] = _dot_nt(wt_ref[...], xn.astype(MXU_DTYPE))


def _qkv_sample(x, g, w_t):
    return pl.pallas_call(
        _qkv_sample_kernel,
        out_shape=jax.ShapeDtypeStruct((w_t.shape[0], x.shape[0]), F32),
        compiler_params=_cparams(None),
        name="qkv_sample",
    )(x, g, w_t)


SWA_TILES = 4


def _swa_kernel(head_cfg, has_sink, want_lse, *refs):
    it = iter(refs)
    tab_ref = next(it)
    sink_ref = next(it) if has_sink else None
    q_refs = [next(it)]
    if any(c[0] == 1 for c in head_cfg):
        q_refs.append(next(it))
    kc_ref, kp_ref = next(it), next(it)
    v_refs = [(next(it), next(it))]
    if any(c[3] == 1 for c in head_cfg):
        v_refs.append((next(it), next(it)))
    o_ref = next(it)
    lse_ref = next(it) if want_lse else None

    first = pl.program_id(1) == 0
    col = lax.broadcasted_iota(jnp.int32, (SLOTS, 2 * SLOTS), 1)
    prev_pen = jnp.where((col < SLOTS) & first, NEG_INF, 0.0)
    lane = lax.broadcasted_iota(jnp.int32, (SLOTS, LANES), 1)
    low = lane < HEAD_DIM
    for t in range(kc_ref.shape[1] // SLOTS):
        rows = slice(t * SLOTS, (t + 1) * SLOTS)
        before = slice((t - 1) * SLOTS, t * SLOTS)

        def keys(cur_ref, prev_ref, lanes):
            prev = prev_ref[0, :, lanes] if t == 0 else cur_ref[0, before, lanes]
            return jnp.concatenate([prev, cur_ref[0, rows, lanes]], axis=0)

        lse_acc = jnp.zeros((SLOTS, LANES), F32)
        for p in range(len(head_cfg) // 2):
            halves = []
            for hh in range(2):
                h = 2 * p + hh
                q_src, q_half, k_tile, v_src, v_tile = head_cfg[h]
                q = q_refs[q_src][0, rows, p * LANES:(p + 1) * LANES]
                q = jnp.where(low if q_half == 0 else jnp.logical_not(low), q, jnp.zeros_like(q))
                kcat = keys(kc_ref, kp_ref, slice(k_tile * LANES, (k_tile + 1) * LANES))
                s = _dot_nt(q, kcat) + tab_ref[h]
                if t == 0:
                    s = s + prev_pen
                m = jnp.max(s, axis=-1, keepdims=True)
                if has_sink:
                    m = jnp.maximum(m, sink_ref[h])
                e = jnp.exp(s - m)
                l = jnp.sum(e, axis=-1, keepdims=True)
                if has_sink:
                    l = l + jnp.exp(sink_ref[h] - m)
                vc_ref, vp_ref = v_refs[v_src]
                vcat = keys(vc_ref, vp_ref, slice(v_tile * LANES, (v_tile + 1) * LANES))
                halves.append(_dot(e.astype(MXU_DTYPE), vcat) / l)
                if want_lse:
                    lse_acc = jnp.where(lane == h, m + jnp.log(l), lse_acc)
            o_ref[0, rows, p * LANES:(p + 1) * LANES] = jnp.where(low, halves[0], halves[1]).astype(o_ref.dtype)
        if want_lse:
            lse_ref[0, rows] = lse_acc


def _swa(tab, q_list, k, v_list, head_cfg, sinks=None, want_lse=True):
    r, s, _ = q_list[0].shape
    ck = k.shape[-1]
    tiles = math.gcd(SWA_TILES, s // SLOTS)
    rows = tiles * SLOTS
    assert s % rows == 0
    nb = s // rows
    cur = lambda i, j: (i, j, 0)
    prev = lambda i, j: (i, jnp.maximum(tiles * j - 1, 0), 0)
    in_specs = [pl.BlockSpec(tab.shape, lambda i, j: (0, 0, 0))]
    args = [tab]
    if sinks is not None:
        in_specs.append(pl.BlockSpec(memory_space=pltpu.SMEM))
        args.append(sinks)
    for q in q_list:
        in_specs.append(pl.BlockSpec((1, rows, q.shape[-1]), cur))
        args.append(q)
    in_specs += [pl.BlockSpec((1, rows, ck), cur), pl.BlockSpec((1, SLOTS, ck), prev)]
    args += [k, k]
    for v in v_list:
        in_specs += [pl.BlockSpec((1, rows, ck), cur), pl.BlockSpec((1, SLOTS, ck), prev)]
        args += [v, v]
    out_shape = [jax.ShapeDtypeStruct((r, s, W_A), MXU_DTYPE)]
    out_specs = [pl.BlockSpec((1, rows, W_A), cur)]
    if want_lse:
        out_shape.append(jax.ShapeDtypeStruct((r, s, LANES), F32))
        out_specs.append(pl.BlockSpec((1, rows, LANES), cur))
    out = pl.pallas_call(
        functools.partial(_swa_kernel, tuple(head_cfg), sinks is not None, want_lse),
        grid=(r, nb),
        in_specs=in_specs,
        out_shape=out_shape,
        out_specs=out_specs,
        compiler_params=_cparams(("parallel", "arbitrary")),
        name="swa",
    )(*args)
    return out if want_lse else out[0]


HEAD_CFG_A = tuple((0, h % 2, h // 2, 0, h // 2) for h in range(H_A))


def _head_cfg_b():
    cfg = []
    for h in range(H_B):
        c = h // G_B
        src = 0 if h % 2 == c else 1
        cfg.append((src, c, 0, src, 0))
    return tuple(cfg)


HEAD_CFG_B = _head_cfg_b()


def _shift_in(x, new_col):
    length = x.shape[-1]
    rolled = pltpu.roll(x, length - 1, axis=1)
    lane = lax.broadcasted_iota(jnp.int32, x.shape, 1)
    return jnp.where(lane == length - 1, new_col, rolled)


def _col_attention(q, kmat, vmat, k_new, v_new, bias_row, mult_row, bias_new, mult_new, sink):
    s = jnp.sum(q * kmat, axis=0, keepdims=True) + bias_row
    s_new = jnp.sum(q * k_new, axis=0, keepdims=True) + bias_new
    if mult_row is not None:
        s = jnp.where(mult_row > 0.0, s, NEG_INF)
    m = jnp.maximum(jnp.max(s, axis=-1, keepdims=True), s_new)
    if sink is not None:
        m = jnp.maximum(m, sink)
    e = jnp.exp(s - m)
    if mult_row is not None:
        e = e * mult_row
    e_new = mult_new * jnp.exp(s_new - m)
    l = jnp.sum(e, axis=-1, keepdims=True) + e_new
    if sink is not None:
        l = l + jnp.exp(sink - m)
    o = jnp.sum(vmat * e, axis=-1, keepdims=True) + v_new * e_new
    return o / l


SAMPLE_HEADS = H_A // KV_B
assert SAMPLE_HEADS == G_B


def _column(ht_ref, row0, nrows, pick):
    return jnp.sum(jnp.where(pick, ht_ref[pl.ds(row0, nrows), :], 0.0), axis=1, keepdims=True)


def _sample_kernel(bias0a_ref, bias0b_ref, sink_ref, taba_ref, mult_ref, tabb_ref, ht_ref,
                   ka_ref, va_ref, kb_ref, vb_ref, ot_ref, kao_ref, vao_ref, kbo_ref, vbo_ref):
    n, hb = pl.program_id(0), pl.program_id(1)
    lane = n % LANES
    pick = lax.broadcasted_iota(jnp.int32, (1, LANES), 1) == lane
    rows = SAMPLE_HEADS * HEAD_DIM
    blk = pl.multiple_of(hb * rows, rows)
    kvr = pl.multiple_of(hb * HEAD_DIM, HEAD_DIM)

    @pl.when((lane == 0) & (hb == 0))
    def _():
        ot_ref[...] = jnp.zeros_like(ot_ref)

    def put(row0, col):
        ot_ref[pl.ds(row0, rows), :] = jnp.where(pick, col, ot_ref[pl.ds(row0, rows), :])

    q, k_new, v_new = (_column(ht_ref, c0 + blk, rows, pick) for c0 in (0, W_A, 2 * W_A))
    mult = mult_ref[...]
    outs = []
    for hl in range(SAMPLE_HEADS):
        r = slice(hl * HEAD_DIM, (hl + 1) * HEAD_DIM)
        kmat, vmat = ka_ref[0, hl], va_ref[0, hl]
        outs.append(_col_attention(q[r], kmat, vmat, k_new[r], v_new[r], taba_ref[hl], mult,
                                   bias0a_ref[hb * SAMPLE_HEADS + hl], float(len(DILATIONS)), None))
        kao_ref[0, hl] = _shift_in(kmat, k_new[r])
        vao_ref[0, hl] = _shift_in(vmat, v_new[r])
    put(blk, jnp.concatenate(outs, axis=0))

    c3 = 3 * W_A
    q = _column(ht_ref, c3 + blk, rows, pick)
    k_new = _column(ht_ref, c3 + W_BQ + kvr, HEAD_DIM, pick)
    v_new = _column(ht_ref, c3 + W_BQ + W_BKV + kvr, HEAD_DIM, pick)
    kmat, vmat = kb_ref[0, 0], vb_ref[0, 0]
    outs = []
    for g in range(G_B):
        h = hb * G_B + g
        outs.append(_col_attention(q[g * HEAD_DIM:(g + 1) * HEAD_DIM], kmat, vmat, k_new, v_new, tabb_ref[g], None,
                                   bias0b_ref[h], 1.0, sink_ref[h]))
    put(W_A + blk, jnp.concatenate(outs, axis=0))
    kbo_ref[0, 0] = _shift_in(kmat, k_new)
    vbo_ref[0, 0] = _shift_in(vmat, v_new)


def _sample_specs(bias0_a, bias0_b, sinks, tab_a, mult_a, tab_b, h_t, ka_t, va_t, kb_t, vb_t):
    ns, nh, hd, la = ka_t.shape
    _, nkv, _, lb = kb_t.shape
    assert nh // SAMPLE_HEADS == nkv and ns % LANES == 0
    smem = pl.BlockSpec(memory_space=pltpu.SMEM)
    lanes_of = lambda rows: pl.BlockSpec((rows, LANES), lambda i, j: (0, i // LANES))
    a_spec = pl.BlockSpec((1, SAMPLE_HEADS, hd, la), lambda i, j: (i, j, 0, 0))
    b_spec = pl.BlockSpec((1, 1, hd, lb), lambda i, j: (i, j, 0, 0))
    in_specs = [smem, smem, smem,
                pl.BlockSpec((SAMPLE_HEADS, 1, la), lambda i, j: (j, 0, 0)),
                pl.BlockSpec((1, la), lambda i, j: (0, 0)),
                pl.BlockSpec((G_B, 1, lb), lambda i, j: (j, 0, 0)),
                lanes_of(h_t.shape[0]), a_spec, a_spec, b_spec, b_spec]
    out_specs = [lanes_of(W_A + W_BQ), a_spec, a_spec, b_spec, b_spec]
    shapes = [jax.ShapeDtypeStruct((W_A + W_BQ, ns), F32)] + [jax.ShapeDtypeStruct(c.shape, F32)
                                                              for c in (ka_t, va_t, kb_t, vb_t)]
    return in_specs, out_specs, shapes


TAIL_TB = 512


def _unpermute(pt_ref, blocks_ref, exact):
    d = blocks_ref.shape[1]
    chunk = pt_ref.shape[0]
    per = chunk // d
    out = []
    for c in range(d * blocks_ref.shape[2] // chunk):
        x = jnp.concatenate([blocks_ref[0, r, c * per:(c + 1) * per] for r in range(d)], axis=0)
        if exact:
            out.append(sum(_dot(pt_ref[...], part) for part in _split3(x)))
        else:
            out.append(_dot(pt_ref[...], x))
    return jnp.concatenate(out, axis=0)


def _tail_prompt_kernel(x_ref, o1_ref, l1_ref, o4_ref, l4_ref, o16_ref, l16_ref, ob_ref, pt4_ref, pt16_ref,
                        spread_ref, w_ref, g_ref, h_ref, xn_ref):
    o_g = [o1_ref[0].astype(F32), _unpermute(pt4_ref, o4_ref, False), _unpermute(pt16_ref, o16_ref, False)]
    l_g = [l1_ref[0], _unpermute(pt4_ref, l4_ref, True), _unpermute(pt16_ref, l16_ref, True)]
    m = jnp.maximum(jnp.maximum(l_g[0], l_g[1]), l_g[2])
    e_g = [jnp.exp(l - m) for l in l_g]
    den = e_g[0] + e_g[1] + e_g[2]
    oa = None
    for e, o in zip(e_g, o_g):
        w = e / den
        w_wide = sum(_dot(part, spread_ref[...]) for part in _split2(w))
        oa = w_wide * o if oa is None else oa + w_wide * o
    o = jnp.concatenate([oa.astype(MXU_DTYPE), ob_ref[0]], axis=-1)
    h = x_ref[0] + _dot(o, w_ref[...])
    h_ref[0] = h
    xn_ref[0] = _rms(h, g_ref[...])


def _tail_prompt(x, o1, l1, o4, l4, o16, l16, ob, w_out, g):
    n, s, dm = x.shape
    tb = TAIL_TB
    pt4 = jnp.asarray(_residue_perm(PERM_CHUNK[4], 4).T, MXU_DTYPE)
    pt16 = jnp.asarray(_residue_perm(PERM_CHUNK[16], 16).T, MXU_DTYPE)
    spread = np.zeros((LANES, W_A), np.float32)
    spread[np.arange(W_A) // HEAD_DIM, np.arange(W_A)] = 1.0
    spread = jnp.asarray(spread, MXU_DTYPE)
    nat = lambda w: pl.BlockSpec((1, tb, w), lambda i, j: (i, j, 0))
    res = lambda d, w: pl.BlockSpec((1, d, tb // d, w), lambda i, j: (i, 0, j, 0))
    const = lambda a: pl.BlockSpec(a.shape, lambda i, j: (0,) * a.ndim)
    return pl.pallas_call(
        _tail_prompt_kernel,
        grid=(n, s // tb),
        in_specs=[nat(dm), nat(W_A), nat(LANES), res(4, W_A), res(4, LANES), res(16, W_A), res(16, LANES),
                  nat(W_BQ), const(pt4), const(pt16), const(spread), const(w_out), const(g)],
        out_shape=[jax.ShapeDtypeStruct((n, s, dm), F32), jax.ShapeDtypeStruct((n, s, dm), F32)],
        out_specs=[nat(dm), nat(dm)],
        compiler_params=_cparams(("parallel", "parallel")),
        name="tail_prompt",
    )(x, o1, l1, o4, l4, o16, l16, ob, pt4, pt16, spread, w_out, g)


def _tail_sample_kernel(x_ref, ot_ref, w_ref, g_ref, h_ref, xn_ref):
    h = x_ref[...] + _dot(ot_ref[...].T.astype(MXU_DTYPE), w_ref[...])
    h_ref[...] = h
    xn_ref[...] = _rms(h, g_ref[...])


def _tail_sample(x, o_t, w_out, g):
    return pl.pallas_call(
        _tail_sample_kernel,
        out_shape=[jax.ShapeDtypeStruct(x.shape, F32), jax.ShapeDtypeStruct(x.shape, F32)],
        compiler_params=_cparams(None),
        name="tail_sample",
    )(x, o_t, w_out, g)


def _sorting_network(n):
    size = 1 << (n - 1).bit_length()

    def merge(lo, hi, r):
        step = r * 2
        if step < hi - lo:
            yield from merge(lo, hi, step)
            yield from merge(lo + r, hi, step)
            yield from ((i, i + r) for i in range(lo + r, hi - r, step))
        else:
            yield (lo, lo + r)

    def sort(lo, hi):
        if hi - lo >= 1:
            mid = lo + (hi - lo) // 2
            yield from sort(lo, mid)
            yield from sort(mid + 1, hi)
            yield from merge(lo, hi, 1)

    return [(i, j) for i, j in sort(0, size - 1) if j < n]


def _top_rows(s, count):
    tiles = [s[v * SUBLANES:(v + 1) * SUBLANES] for v in range(s.shape[0] // SUBLANES)]
    for i, j in _sorting_network(len(tiles)):
        tiles[i], tiles[j] = jnp.maximum(tiles[i], tiles[j]), jnp.minimum(tiles[i], tiles[j])
    rows = []
    for t in range(count):
        m = jnp.max(tiles[0], axis=0, keepdims=True)
        rows.append(m)
        if t + 1 < count:
            popped = tiles[0] == m
            depth = min(len(tiles), count - t - 1)
            for k in range(depth):
                below = tiles[k + 1] if k + 1 < len(tiles) else NEG_INF
                tiles[k] = jnp.where(popped, below, tiles[k])
    return rows


def _pad_rows(rows, count):
    pad = [jnp.full_like(rows[0], NEG_INF)] * (count - len(rows))
    return jnp.concatenate(list(rows) + pad, axis=0)


def _candidate_sums(top1, top2):
    k = PEER_TOPK + 1
    wide = -(-k // SUBLANES) * SUBLANES
    narrow = -(-(k // 2) // SUBLANES) * SUBLANES
    assert k // (narrow + 1) <= 1
    v2_wide = _pad_rows(top2, wide)
    v2_narrow = v2_wide[:narrow]
    parts = [top1[0] + v2_wide] + [top1[a] + v2_narrow for a in range(1, narrow)]
    parts.append(_pad_rows(top1[narrow:], -(-(k - narrow) // SUBLANES) * SUBLANES) + top2[0])
    return jnp.concatenate(parts, axis=0)


def _route_kernel(x_ref, wqh_ref, wql_ref, skh_ref, skl_ref, thr_ref, p1_ref, p2_ref):
    xh, xl = _split2(x_ref[...])
    wqh = wqh_ref[...]
    q_t = _dot_nt(wqh, xh) + _dot_nt(wqh, xl) + _dot_nt(wql_ref[...], xh)
    half = D_KEY // 2
    for h in range(PEER_HEADS):
        scores = []
        for c in range(2):
            qh, ql = _split2(q_t[h * D_KEY + c * half:h * D_KEY + (c + 1) * half, :])
            scores.append(_dot(skh_ref[c], qh) + _dot(skh_ref[c], ql) + _dot(skl_ref[c], qh))
        for g in range(q_t.shape[1] // LANES):
            s1, s2 = (s[:, g * LANES:(g + 1) * LANES] for s in scores)
            top1, top2 = _top_rows(s1, PEER_TOPK + 1), _top_rows(s2, PEER_TOPK + 1)
            best = _top_rows(_candidate_sums(top1, top2), PEER_TOPK + 1)
            tau = 0.5 * (best[PEER_TOPK - 1] + best[PEER_TOPK])
            z = sum(jnp.exp(b - best[0]) for b in best[:PEER_TOPK])
            log_norm = best[0] + jnp.log(z)
            m2 = top2[0]
            thr_ref[h, g] = jnp.exp((tau - m2) - s1)
            p1_ref[h, g] = jnp.exp(s1 + (m2 - log_norm))
            p2_ref[h, g] = jnp.exp(s2 - m2)


def _route(xn, wq_t_hi, wq_t_lo, sk_hi, sk_lo, tb):
    t, dm = xn.shape
    const = lambda a: pl.BlockSpec(a.shape, lambda i: (0,) * a.ndim)
    gspec = pl.BlockSpec((PEER_HEADS, tb // LANES, N_KEYS, LANES), lambda i: (0, i, 0, 0))
    gshape = jax.ShapeDtypeStruct((PEER_HEADS, t // LANES, N_KEYS, LANES), F32)
    return pl.pallas_call(
        _route_kernel,
        grid=(t // tb,),
        in_specs=[pl.BlockSpec((tb, dm), lambda i: (i, 0)), const(wq_t_hi), const(wq_t_lo), const(sk_hi), const(sk_lo)],
        out_shape=[gshape] * 3,
        out_specs=[gspec] * 3,
        compiler_params=_cparams(("parallel",)),
        name="peer_route",
    )(xn, wq_t_hi, wq_t_lo, sk_hi, sk_lo)


EXPERT_CHUNK = SUBLANES * N_KEYS
GATE_JBLOCK = 2 * SUBLANES
INV_SQRT2 = 0.7071067811865476


def _gate_rows(thr_ref, p1_ref, p2_ref, w_ref):
    nv = GATE_JBLOCK // SUBLANES

    def row_group(i8, carry):
        base = pl.multiple_of(i8 * SUBLANES, SUBLANES)
        thr_t = [thr_ref[h, 0, pl.ds(base, SUBLANES), :] for h in range(PEER_HEADS)]
        p1_t = [p1_ref[h, 0, pl.ds(base, SUBLANES), :] for h in range(PEER_HEADS)]
        for jb in range(N_KEYS // GATE_JBLOCK):
            w = [[None] * nv for _ in range(SUBLANES)]
            for h in range(PEER_HEADS):
                p2 = [p2_ref[h, 0, pl.ds(jb * GATE_JBLOCK + v * SUBLANES, SUBLANES), :] for v in range(nv)]
                for r in range(SUBLANES):
                    thr = jnp.broadcast_to(thr_t[h][r:r + 1, :], (SUBLANES, LANES))
                    p1 = jnp.broadcast_to(p1_t[h][r:r + 1, :], (SUBLANES, LANES))
                    for v in range(nv):
                        term = jnp.where(p2[v] >= thr, p1 * p2[v], 0.0)
                        w[r][v] = term if w[r][v] is None else w[r][v] + term
            for r in range(SUBLANES):
                row0 = pl.multiple_of((base + r) * N_KEYS + jb * GATE_JBLOCK, GATE_JBLOCK)
                w_ref[0, pl.ds(row0, GATE_JBLOCK), :] = jnp.concatenate(w[r], axis=0).astype(w_ref.dtype)
        return carry

    lax.fori_loop(0, thr_ref.shape[2] // SUBLANES, row_group, 0)


GATE_ROWS = N_KEYS // 2


def _gate_specs(t):
    rows = pl.BlockSpec((PEER_HEADS, 1, GATE_ROWS, LANES), lambda i, j: (0, i, j, 0))
    full = pl.BlockSpec((PEER_HEADS, 1, N_KEYS, LANES), lambda i, j: (0, i, 0, 0))
    out = pl.BlockSpec((1, GATE_ROWS * N_KEYS, LANES), lambda i, j: (i, j, 0))
    shape = jax.ShapeDtypeStruct((t // LANES, N_KEYS * N_KEYS, LANES), MXU_DTYPE)
    return [rows, rows, full], out, shape


def _gates(thr, p1, p2):
    t = thr.shape[1] * LANES
    in_specs, out_spec, shape = _gate_specs(t)
    return pl.pallas_call(
        _gate_rows,
        grid=(t // LANES, N_KEYS // GATE_ROWS),
        in_specs=in_specs, out_specs=out_spec, out_shape=shape,
        compiler_params=_cparams(("parallel", "parallel")),
        name="peer_gates",
    )(thr, p1, p2)


def _sample_and_gates_kernel(*refs):
    n_in, n_gate = 11, 3
    sample_in, gate_in = refs[:n_in], refs[n_in:n_in + n_gate]
    sample_out, gate_out = refs[n_in + n_gate:-1], refs[-1]
    _sample_kernel(*sample_in, *sample_out)
    _gate_rows(*gate_in, gate_out)


def _sample_and_gates(sample_args, thr, p1, p2):
    t = thr.shape[1] * LANES
    ns, nkv = sample_args[7].shape[0], sample_args[9].shape[1]
    s_in, s_out, s_shapes = _sample_specs(*sample_args)
    if t // LANES != ns or N_KEYS // GATE_ROWS != nkv:
        outs = pl.pallas_call(
            _sample_kernel, grid=(ns, nkv), in_specs=s_in, out_specs=s_out, out_shape=s_shapes,
            compiler_params=_cparams(("arbitrary", "arbitrary")), name="sample_attn",
        )(*sample_args)
        return list(outs) + [_gates(thr, p1, p2)]
    g_in, g_out, g_shape = _gate_specs(t)
    return pl.pallas_call(
        _sample_and_gates_kernel,
        grid=(ns, nkv),
        in_specs=s_in + g_in, out_specs=s_out + [g_out], out_shape=s_shapes + [g_shape],
        compiler_params=_cparams(("arbitrary", "arbitrary")),
        name="sample_attn_and_gates",
    )(*sample_args, thr, p1, p2)


def _activate(g, a_ref, p_ref, w_ref):
    blk = 8 * SUBLANES
    for b in range(a_ref.shape[1] // blk):
        rows = pl.ds(b * blk, blk)
        a = a_ref[g, rows, :]
        half = 0.5 * a
        act = half + half * lax.erf(a * INV_SQRT2)
        p_ref[g, rows, :] = w_ref[g, rows, :] * act.astype(p_ref.dtype)


def _experts_kernel(xn_ref, res_ref, g_ref, down_ref, upt_ref, w_a_ref, w_b_ref,
                    y_ref, xb_ref, a0_ref, a1_ref, pb0_ref, pb1_ref, acc_ref):
    i, j = pl.program_id(0), pl.program_id(1)
    ch = EXPERT_CHUNK
    ng = a0_ref.shape[0]

    @pl.when((i == 0) & (j == 0))
    def _():
        for ref in (a0_ref, a1_ref, pb0_ref, pb1_ref):
            ref[...] = jnp.zeros_like(ref)

    @pl.when(j == 0)
    def _():
        xb_ref[...] = xn_ref[...].astype(MXU_DTYPE)

    keep = j >= 1

    def tick(half, a_in_ref, a_out_ref, p_out_ref, p_in_ref, w_ref):
        p_in = jnp.concatenate([p_in_ref[gg] for gg in range(ng)], axis=1)
        upd = _dot(upt_ref[:, half * ch:(half + 1) * ch], p_in)
        nxt = _dot_nt(down_ref[pl.ds(half * ch, ch), :], xb_ref[...])
        for g in range(ng):
            a_out_ref[g] = nxt[:, g * LANES:(g + 1) * LANES]
            _activate(g, a_in_ref, p_out_ref, w_ref)
        return upd

    upd = tick(0, a1_ref, a0_ref, pb1_ref, pb0_ref, w_a_ref)
    upd = upd + tick(1, a0_ref, a1_ref, pb0_ref, pb1_ref, w_b_ref)
    acc_ref[...] = jnp.where(keep, acc_ref[...] + upd, 0.0)

    @pl.when(j == pl.num_programs(1) - 1)
    def _():
        y = res_ref[...] + acc_ref[...].T
        y_ref[...] = _rms(y, g_ref[...])


def _experts(xn, res, g, down, up_t, w, tb):
    t, dm = xn.shape
    ne = down.shape[0]
    step = 2 * EXPERT_CHUNK
    nj = ne // step
    ng = tb // LANES
    tok = pl.BlockSpec((tb, dm), lambda i, j: (i, 0))
    w_a = pl.BlockSpec((ng, EXPERT_CHUNK, LANES), lambda i, j: (i, jnp.maximum(2 * j - 1, 0), 0))
    w_b = pl.BlockSpec((ng, EXPERT_CHUNK, LANES), lambda i, j: (i, jnp.minimum(2 * j, 2 * nj - 1), 0))
    return pl.pallas_call(
        _experts_kernel,
        grid=(t // tb, nj + 1),
        in_specs=[tok, tok, pl.BlockSpec((1, dm), lambda i, j: (0, 0)),
                  pl.BlockSpec((step, dm), lambda i, j: (jnp.minimum(j, nj - 1), 0)),
                  pl.BlockSpec((dm, step), lambda i, j: (0, jnp.maximum(j - 1, 0))),
                  w_a, w_b],
        out_shape=jax.ShapeDtypeStruct((t, dm), F32),
        out_specs=tok,
        scratch_shapes=[pltpu.VMEM((tb, dm), MXU_DTYPE),
                        pltpu.VMEM((ng, EXPERT_CHUNK, LANES), F32), pltpu.VMEM((ng, EXPERT_CHUNK, LANES), F32),
                        pltpu.VMEM((ng, EXPERT_CHUNK, LANES), MXU_DTYPE),
                        pltpu.VMEM((ng, EXPERT_CHUNK, LANES), MXU_DTYPE),
                        pltpu.VMEM((dm, tb), F32)],
        compiler_params=_cparams(("arbitrary", "arbitrary")),
        name="peer_experts",
    )(xn, res, g, down, up_t, w, w)


PEER_TB = 512


def _cache_to_feature_major(c):
    return jnp.transpose(c, (0, 2, 3, 1))


def _cache_from_feature_major(c):
    return jnp.transpose(c, (0, 3, 1, 2))


def _layer(xp, xs, cache_a_k, cache_a_v, cache_b_k, cache_b_v, norm_attn, w_in, rel_bias, sinks, w_out, norm_ffn,
           w_peer_q, peer_sub_keys, peer_down, peer_up, g_final):
    n, s, dm = xp.shape
    ns = xs.shape[0]
    la, lb = cache_a_k.shape[1], cache_b_k.shape[1]

    c3 = 3 * W_A
    w_q_scaled = jnp.concatenate([w_in[:, :W_A] * SCALE, w_in[:, W_A:c3], w_in[:, c3:c3 + W_BQ] * SCALE,
                                  w_in[:, c3 + W_BQ:]], axis=1)
    qb_cols = w_q_scaled[:, c3:c3 + W_BQ].reshape(dm, H_B // 2, 2, HEAD_DIM)[:, :, ::-1].reshape(dm, W_BQ)
    vb_cols = w_in[:, c3 + W_BQ + W_BKV:].reshape(dm, KV_B, HEAD_DIM)[:, ::-1].reshape(dm, W_BKV)
    w_ext = jnp.concatenate([w_q_scaled, qb_cols, vb_cols], axis=1).astype(MXU_DTYPE)
    w_nat_t = w_q_scaled.T.astype(MXU_DTYPE)
    w_out_b = w_out.astype(MXU_DTYPE)
    g_attn, g_ffn = norm_attn[None, :], norm_ffn[None, :]
    wq_t = w_peer_q.T
    wq_t_hi = wq_t.astype(MXU_DTYPE)
    wq_t_lo = (wq_t - wq_t_hi.astype(F32)).astype(MXU_DTYPE)
    sk_hi = peer_sub_keys.astype(MXU_DTYPE)
    sk_lo = (peer_sub_keys - sk_hi.astype(F32)).astype(MXU_DTYPE)
    peer_w = (wq_t_hi, wq_t_lo, sk_hi, sk_lo, peer_down.astype(MXU_DTYPE), peer_up.T.astype(MXU_DTYPE))

    bidx_sa, mult_sa = _sample_tables(la)
    bidx_sb, _ = _sample_tables(lb)
    assert H_A == H_B
    *tabs_a, tab_b, tab_sa, tab_sb = _bias_tables(
        rel_bias, H_A, [(_window_bucket_matrix(d), 0) for d in DILATIONS]
        + [(_window_bucket_matrix(1), H_A), (bidx_sa, 0), (bidx_sb, H_A)])
    bias0_a, bias0_b = rel_bias[0, :H_A], rel_bias[0, H_A:]

    (q1, k1, v1, kf, vf, q4, k4, v4, q16, k16, v16, qb, qbs, kb, vb, vbs, kbf, vbf) = _qkv_prompt(xp, g_attn, w_ext)
    o1, l1 = _swa(tabs_a[0], [q1], k1, [v1], HEAD_CFG_A)
    flat = lambda a: a.reshape((a.shape[0] * a.shape[1],) + a.shape[2:])
    o4, l4 = _swa(tabs_a[1], [flat(q4)], flat(k4), [flat(v4)], HEAD_CFG_A)
    o16, l16 = _swa(tabs_a[2], [flat(q16)], flat(k16), [flat(v16)], HEAD_CFG_A)
    ob = _swa(tab_b, [qb, qbs], kb, [vb, vbs], HEAD_CFG_B, sinks=sinks, want_lse=False)
    unflat = lambda a, d: a.reshape((n, d) + a.shape[1:])
    hp, xnp = _tail_prompt(xp, o1, l1, unflat(o4, 4), unflat(l4, 4), unflat(o16, 16), unflat(l16, 16), ob,
                           w_out_b, g_ffn)
    prompt_caches = (kf[:, s - min(la, s):].reshape(n, -1, H_A, HEAD_DIM),
                     vf[:, s - min(la, s):].reshape(n, -1, H_A, HEAD_DIM),
                     kbf[:, s - min(lb, s):].reshape(n, -1, KV_B, HEAD_DIM),
                     vbf[:, s - min(lb, s):].reshape(n, -1, KV_B, HEAD_DIM))

    hp, xnp = hp.reshape(n * s, dm), xnp.reshape(n * s, dm)
    route_w, (down_b, up_t_b) = peer_w[:4], peer_w[4:]
    routing_p = _route(xnp, *route_w, PEER_TB)

    hs_t = _qkv_sample(xs, g_attn, w_nat_t)
    sample_args = (bias0_a, bias0_b, sinks, tab_sa, jnp.asarray(mult_sa), tab_sb, hs_t,
                   _cache_to_feature_major(cache_a_k), _cache_to_feature_major(cache_a_v),
                   _cache_to_feature_major(cache_b_k), _cache_to_feature_major(cache_b_v))
    o_t, *shifted, gates_p = _sample_and_gates(sample_args, *routing_p)
    h_s, xn_s = _tail_sample(xs, o_t, w_out_b, g_ffn)
    sample_caches = tuple(_cache_from_feature_major(c) for c in shifted)

    y_p = _experts(xnp, hp, g_final, down_b, up_t_b, gates_p, PEER_TB)
    gates_s = _gates(*_route(xn_s, *route_w, LANES))
    y_s = _experts(xn_s, h_s, g_final, down_b, up_t_b, gates_s, LANES)
    return y_p.reshape(n, s, dm), y_s, prompt_caches, sample_caches


def kernel(x_prompt, x_sample, cache_a_k, cache_a_v, cache_b_k, cache_b_v, norm_attn, w_in, rel_bias, sinks, w_out,
           norm_ffn, w_peer_q, peer_sub_keys, peer_down, peer_up, norm_final):
    depth = w_in.shape[0]
    assert depth == 1, "single-layer trunk"
    n, s, dm = x_prompt.shape
    ns = x_sample.shape[0]
    assert x_sample.shape[1] == 1 and s % QKV_TB == 0 and (n * s) % PEER_TB == 0 and ns % LANES == 0
    l = 0
    y_prompt, y_sample, prompt_caches, sample_caches = _layer(
        x_prompt, x_sample[:, 0], cache_a_k[l], cache_a_v[l], cache_b_k[l], cache_b_v[l], norm_attn[l], w_in[l],
        rel_bias, sinks[l], w_out[l], norm_ffn[l], w_peer_q[l], peer_sub_keys[l], peer_down[l], peer_up[l],
        norm_final[None, :])
    return ((y_prompt, y_sample.reshape(ns, 1, dm)) + tuple(c[None] for c in prompt_caches)
            + tuple(c[None] for c in sample_caches))
```

```python
import functools
import math

import numpy as np
import jax
import jax.numpy as jnp
from jax import lax
from jax.experimental import pallas as pl
from jax.experimental.pallas import tpu as pltpu

HEAD_DIM = 64
H_A = 8
H_B = 8
KV_B = 2
G_B = H_B // KV_B
DILATIONS = (1, 4, 16)
SLOTS = 128
N_BUCKETS = 32
MAX_DISTANCE = 2048
N_KEYS = 128
PEER_HEADS = 8
PEER_TOPK = 16
D_KEY = 128
RMS_EPS = 1e-6
SCALE = HEAD_DIM ** -0.5
W_A = H_A * HEAD_DIM
W_BQ = H_B * HEAD_DIM
W_BKV = KV_B * HEAD_DIM

LANES = 128
SUBLANES = 8
VMEM_LIMIT_BYTES = 56 * 1024 * 1024

MXU_DTYPE = jnp.bfloat16
F32 = jnp.float32
NEG_INF = float("-inf")


def _cparams(sem):
    return pltpu.CompilerParams(dimension_semantics=sem, vmem_limit_bytes=VMEM_LIMIT_BYTES)


def _t5_bucket_np(dist):
    exact = N_BUCKETS // 2
    d = np.maximum(dist, 0)
    logd = np.log(np.maximum(d, 1).astype(np.float32) / np.float32(exact)) / np.float32(math.log(MAX_DISTANCE / exact))
    large = np.minimum(exact + (logd * np.float32(N_BUCKETS - exact)).astype(np.int32), N_BUCKETS - 1)
    return np.where(d < exact, d, large).astype(np.int32)


def _window_bucket_matrix(dilation):
    i = np.arange(SLOTS)[:, None]
    j = np.arange(2 * SLOTS)[None, :]
    dist = i - j + SLOTS
    ok = (dist >= 0) & (dist <= SLOTS)
    return np.where(ok, _t5_bucket_np(dist * dilation), -1).astype(np.int32)


def _sample_tables(cache_len):
    delta = cache_len - np.arange(cache_len)
    mult = np.zeros(cache_len, np.float32)
    for d in DILATIONS:
        mult += ((delta % d == 0) & (delta <= SLOTS * d)).astype(np.float32)
    return _t5_bucket_np(delta)[None, :], mult[None, :]


def _residue_perm(tb, d):
    p = np.zeros((tb, tb), np.float32)
    rows = np.arange(tb)
    p[rows, (rows % (tb // d)) * d + rows // (tb // d)] = 1.0
    return p


def _rms(x, g):
    return x * lax.rsqrt(jnp.mean(x * x, axis=-1, keepdims=True) + RMS_EPS) * g


def _dot(a, b):
    return jnp.dot(a, b, preferred_element_type=F32)


def _dot_nt(a, b):
    return lax.dot_general(a, b, (((1,), (1,)), ((), ())), preferred_element_type=F32)


def _split2(x):
    hi = x.astype(MXU_DTYPE)
    lo = (x - hi.astype(F32)).astype(MXU_DTYPE)
    return hi, lo


def _split3(x):
    hi = x.astype(MXU_DTYPE)
    r1 = x - hi.astype(F32)
    mid = r1.astype(MXU_DTYPE)
    lo = (r1 - mid.astype(F32)).astype(MXU_DTYPE)
    return hi, mid, lo


def _bias_kernel(col0s, nheads, rb_ref, *refs):
    n = len(col0s)
    for col0, bidx_ref, o_ref in zip(col0s, refs[:n], refs[n:]):
        b = bidx_ref[...]
        for h in range(nheads):
            val = jnp.full(b.shape, NEG_INF, F32)
            for k in range(N_BUCKETS):
                val = jnp.where(b == k, rb_ref[k, col0 + h], val)
            o_ref[h] = val


def _bias_tables(rel_bias, nheads, tables):
    bidx = [jnp.asarray(b) for b, _ in tables]
    return pl.pallas_call(
        functools.partial(_bias_kernel, tuple(c for _, c in tables), nheads),
        out_shape=[jax.ShapeDtypeStruct((nheads,) + b.shape, F32) for b in bidx],
        in_specs=[pl.BlockSpec(memory_space=pltpu.SMEM)] + [pl.BlockSpec(b.shape, lambda: (0, 0)) for b in bidx],
        out_specs=[pl.BlockSpec((nheads,) + b.shape, lambda: (0, 0, 0)) for b in bidx],
        name="bias_tables",
    )(rel_bias, *bidx)


QKV_TB = 512
PERM_CHUNK = {4: 128, 16: 256}


def _qkv_prompt_kernel(x_ref, g_ref, w_ref, p4_ref, p16_ref,
                       q1_ref, k1_ref, v1_ref, kf_ref, vf_ref,
                       q4_ref, k4_ref, v4_ref, q16_ref, k16_ref, v16_ref,
                       qb_ref, qbs_ref, kb_ref, vb_ref, vbs_ref, kbf_ref, vbf_ref):
    xn = _rms(x_ref[0], g_ref[...])
    h = _dot(xn.astype(MXU_DTYPE), w_ref[...])
    c0, c1, c2, c3 = 0, W_A, 2 * W_A, 3 * W_A
    c4, c5, c6 = c3 + W_BQ, c3 + W_BQ + W_BKV, c3 + W_BQ + 2 * W_BKV
    swap_halves = lambda t: pltpu.roll(t, HEAD_DIM, axis=1)
    ha = h[:, :c3].astype(MXU_DTYPE)
    q1_ref[0] = ha[:, c0:c1]
    k1_ref[0] = ha[:, c1:c2]
    v1_ref[0] = ha[:, c2:c3]
    kf_ref[0] = h[:, c1:c2]
    vf_ref[0] = h[:, c2:c3]
    qb_ref[0] = h[:, c3:c4].astype(MXU_DTYPE)
    kb_ref[0] = h[:, c4:c5].astype(MXU_DTYPE)
    vb_ref[0] = h[:, c5:c6].astype(MXU_DTYPE)
    kbf_ref[0] = h[:, c4:c5]
    vbf_ref[0] = h[:, c5:c6]
    qbs_ref[0] = jnp.concatenate([swap_halves(h[:, c:c + LANES]) for c in range(c3, c4, LANES)],
                                 axis=1).astype(MXU_DTYPE)
    vbs_ref[0] = swap_halves(h[:, c5:c6]).astype(MXU_DTYPE)
    tb = ha.shape[0]
    for d, p_ref, outs in ((4, p4_ref, (q4_ref, k4_ref, v4_ref)), (16, p16_ref, (q16_ref, k16_ref, v16_ref))):
        chunk = p_ref.shape[0]
        per = chunk // d
        for c in range(tb // chunk):
            perm = _dot(p_ref[...], ha[c * chunk:(c + 1) * chunk]).astype(MXU_DTYPE)
            for r in range(d):
                for t, o_ref in enumerate(outs):
                    o_ref[0, r, c * per:(c + 1) * per] = perm[r * per:(r + 1) * per, t * W_A:(t + 1) * W_A]


def _qkv_prompt(x, g, w_ext):
    n, s, dm = x.shape
    tb = QKV_TB
    nb = s // tb
    p4 = jnp.asarray(_residue_perm(PERM_CHUNK[4], 4), MXU_DTYPE)
    p16 = jnp.asarray(_residue_perm(PERM_CHUNK[16], 16), MXU_DTYPE)
    bf = MXU_DTYPE

    def nat(width, dt):
        return jax.ShapeDtypeStruct((n, s, width), dt), pl.BlockSpec((1, tb, width), lambda i, j: (i, j, 0))

    def res(d):
        return (jax.ShapeDtypeStruct((n, d, s // d, W_A), bf),
                pl.BlockSpec((1, d, tb // d, W_A), lambda i, j: (i, 0, j, 0)))

    outs = [nat(W_A, bf), nat(W_A, bf), nat(W_A, bf), nat(W_A, F32), nat(W_A, F32),
            res(4), res(4), res(4), res(16), res(16), res(16),
            nat(W_BQ, bf), nat(W_BQ, bf), nat(W_BKV, bf), nat(W_BKV, bf), nat(W_BKV, bf),
            nat(W_BKV, F32), nat(W_BKV, F32)]
    return pl.pallas_call(
        _qkv_prompt_kernel,
        grid=(n, nb),
        in_specs=[pl.BlockSpec((1, tb, dm), lambda i, j: (i, j, 0)),
                  pl.BlockSpec((1, dm), lambda i, j: (0, 0)),
                  pl.BlockSpec(w_ext.shape, lambda i, j: (0, 0)),
                  pl.BlockSpec(p4.shape, lambda i, j: (0, 0)),
                  pl.BlockSpec(p16.shape, lambda i, j: (0, 0))],
        out_shape=[o[0] for o in outs],
        out_specs=[o[1] for o in outs],
        compiler_params=_cparams(("parallel", "parallel")),
        name="qkv_prompt",
    )(x, g, w_ext, p4, p16)


def _qkv_sample_kernel(x_ref, g_ref, wt_ref, o_ref):
    xn = _rms(x_ref[...], g_ref[...])
    o_ref[...] = _dot_nt(wt_ref[...], xn.astype(MXU_DTYPE))


def _qkv_sample(x, g, w_t):
    return pl.pallas_call(
        _qkv_sample_kernel,
        out_shape=jax.ShapeDtypeStruct((w_t.shape[0], x.shape[0]), F32),
        compiler_params=_cparams(None),
        name="qkv_sample",
    )(x, g, w_t)


SWA_TILES = 4


def _swa_kernel(head_cfg, has_sink, want_lse, *refs):
    it = iter(refs)
    tab_ref = next(it)
    sink_ref = next(it) if has_sink else None
    q_refs = [next(it)]
    if any(c[0] == 1 for c in head_cfg):
        q_refs.append(next(it))
    kc_ref, kp_ref = next(it), next(it)
    v_refs = [(next(it), next(it))]
    if any(c[3] == 1 for c in head_cfg):
        v_refs.append((next(it), next(it)))
    o_ref = next(it)
    lse_ref = next(it) if want_lse else None

    first = pl.program_id(1) == 0
    col = lax.broadcasted_iota(jnp.int32, (SLOTS, 2 * SLOTS), 1)
    prev_pen = jnp.where((col < SLOTS) & first, NEG_INF, 0.0)
    lane = lax.broadcasted_iota(jnp.int32, (SLOTS, LANES), 1)
    low = lane < HEAD_DIM
    for t in range(kc_ref.shape[1] // SLOTS):
        rows = slice(t * SLOTS, (t + 1) * SLOTS)
        before = slice((t - 1) * SLOTS, t * SLOTS)

        def keys(cur_ref, prev_ref, lanes):
            prev = prev_ref[0, :, lanes] if t == 0 else cur_ref[0, before, lanes]
            return jnp.concatenate([prev, cur_ref[0, rows, lanes]], axis=0)

        lse_acc = jnp.zeros((SLOTS, LANES), F32)
        for p in range(len(head_cfg) // 2):
            halves = []
            for hh in range(2):
                h = 2 * p + hh
                q_src, q_half, k_tile, v_src, v_tile = head_cfg[h]
                q = q_refs[q_src][0, rows, p * LANES:(p + 1) * LANES]
                q = jnp.where(low if q_half == 0 else jnp.logical_not(low), q, jnp.zeros_like(q))
                kcat = keys(kc_ref, kp_ref, slice(k_tile * LANES, (k_tile + 1) * LANES))
                s = _dot_nt(q, kcat) + tab_ref[h]
                if t == 0:
                    s = s + prev_pen
                m = jnp.max(s, axis=-1, keepdims=True)
                if has_sink:
                    m = jnp.maximum(m, sink_ref[h])
                e = jnp.exp(s - m)
                l = jnp.sum(e, axis=-1, keepdims=True)
                if has_sink:
                    l = l + jnp.exp(sink_ref[h] - m)
                vc_ref, vp_ref = v_refs[v_src]
                vcat = keys(vc_ref, vp_ref, slice(v_tile * LANES, (v_tile + 1) * LANES))
                halves.append(_dot(e.astype(MXU_DTYPE), vcat) / l)
                if want_lse:
                    lse_acc = jnp.where(lane == h, m + jnp.log(l), lse_acc)
            o_ref[0, rows, p * LANES:(p + 1) * LANES] = jnp.where(low, halves[0], halves[1]).astype(o_ref.dtype)
        if want_lse:
            lse_ref[0, rows] = lse_acc


def _swa(tab, q_list, k, v_list, head_cfg, sinks=None, want_lse=True):
    r, s, _ = q_list[0].shape
    ck = k.shape[-1]
    tiles = math.gcd(SWA_TILES, s // SLOTS)
    rows = tiles * SLOTS
    assert s % rows == 0
    nb = s // rows
    cur = lambda i, j: (i, j, 0)
    prev = lambda i, j: (i, jnp.maximum(tiles * j - 1, 0), 0)
    in_specs = [pl.BlockSpec(tab.shape, lambda i, j: (0, 0, 0))]
    args = [tab]
    if sinks is not None:
        in_specs.append(pl.BlockSpec(memory_space=pltpu.SMEM))
        args.append(sinks)
    for q in q_list:
        in_specs.append(pl.BlockSpec((1, rows, q.shape[-1]), cur))
        args.append(q)
    in_specs += [pl.BlockSpec((1, rows, ck), cur), pl.BlockSpec((1, SLOTS, ck), prev)]
    args += [k, k]
    for v in v_list:
        in_specs += [pl.BlockSpec((1, rows, ck), cur), pl.BlockSpec((1, SLOTS, ck), prev)]
        args += [v, v]
    out_shape = [jax.ShapeDtypeStruct((r, s, W_A), MXU_DTYPE)]
    out_specs = [pl.BlockSpec((1, rows, W_A), cur)]
    if want_lse:
        out_shape.append(jax.ShapeDtypeStruct((r, s, LANES), F32))
        out_specs.append(pl.BlockSpec((1, rows, LANES), cur))
    out = pl.pallas_call(
        functools.partial(_swa_kernel, tuple(head_cfg), sinks is not None, want_lse),
        grid=(r, nb),
        in_specs=in_specs,
        out_shape=out_shape,
        out_specs=out_specs,
        compiler_params=_cparams(("parallel", "arbitrary")),
        name="swa",
    )(*args)
    return out if want_lse else out[0]


HEAD_CFG_A = tuple((0, h % 2, h // 2, 0, h // 2) for h in range(H_A))


def _head_cfg_b():
    cfg = []
    for h in range(H_B):
        c = h // G_B
        src = 0 if h % 2 == c else 1
        cfg.append((src, c, 0, src, 0))
    return tuple(cfg)


HEAD_CFG_B = _head_cfg_b()


def _shift_in(x, new_col):
    length = x.shape[-1]
    rolled = pltpu.roll(x, length - 1, axis=1)
    lane = lax.broadcasted_iota(jnp.int32, x.shape, 1)
    return jnp.where(lane == length - 1, new_col, rolled)


def _col_attention(q, kmat, vmat, k_new, v_new, bias_row, mult_row, bias_new, mult_new, sink):
    s = jnp.sum(q * kmat, axis=0, keepdims=True) + bias_row
    s_new = jnp.sum(q * k_new, axis=0, keepdims=True) + bias_new
    if mult_row is not None:
        s = jnp.where(mult_row > 0.0, s, NEG_INF)
    m = jnp.maximum(jnp.max(s, axis=-1, keepdims=True), s_new)
    if sink is not None:
        m = jnp.maximum(m, sink)
    e = jnp.exp(s - m)
    if mult_row is not None:
        e = e * mult_row
    e_new = mult_new * jnp.exp(s_new - m)
    l = jnp.sum(e, axis=-1, keepdims=True) + e_new
    if sink is not None:
        l = l + jnp.exp(sink - m)
    o = jnp.sum(vmat * e, axis=-1, keepdims=True) + v_new * e_new
    return o / l


SAMPLE_HEADS = H_A // KV_B
assert SAMPLE_HEADS == G_B


def _column(ht_ref, row0, nrows, pick):
    return jnp.sum(jnp.where(pick, ht_ref[pl.ds(row0, nrows), :], 0.0), axis=1, keepdims=True)


def _sample_kernel(bias0a_ref, bias0b_ref, sink_ref, taba_ref, mult_ref, tabb_ref, ht_ref,
                   ka_ref, va_ref, kb_ref, vb_ref, ot_ref, kao_ref, vao_ref, kbo_ref, vbo_ref):
    n, hb = pl.program_id(0), pl.program_id(1)
    lane = n % LANES
    pick = lax.broadcasted_iota(jnp.int32, (1, LANES), 1) == lane
    rows = SAMPLE_HEADS * HEAD_DIM
    blk = pl.multiple_of(hb * rows, rows)
    kvr = pl.multiple_of(hb * HEAD_DIM, HEAD_DIM)

    @pl.when((lane == 0) & (hb == 0))
    def _():
        ot_ref[...] = jnp.zeros_like(ot_ref)

    def put(row0, col):
        ot_ref[pl.ds(row0, rows), :] = jnp.where(pick, col, ot_ref[pl.ds(row0, rows), :])

    q, k_new, v_new = (_column(ht_ref, c0 + blk, rows, pick) for c0 in (0, W_A, 2 * W_A))
    mult = mult_ref[...]
    outs = []
    for hl in range(SAMPLE_HEADS):
        r = slice(hl * HEAD_DIM, (hl + 1) * HEAD_DIM)
        kmat, vmat = ka_ref[0, hl], va_ref[0, hl]
        outs.append(_col_attention(q[r], kmat, vmat, k_new[r], v_new[r], taba_ref[hl], mult,
                                   bias0a_ref[hb * SAMPLE_HEADS + hl], float(len(DILATIONS)), None))
        kao_ref[0, hl] = _shift_in(kmat, k_new[r])
        vao_ref[0, hl] = _shift_in(vmat, v_new[r])
    put(blk, jnp.concatenate(outs, axis=0))

    c3 = 3 * W_A
    q = _column(ht_ref, c3 + blk, rows, pick)
    k_new = _column(ht_ref, c3 + W_BQ + kvr, HEAD_DIM, pick)
    v_new = _column(ht_ref, c3 + W_BQ + W_BKV + kvr, HEAD_DIM, pick)
    kmat, vmat = kb_ref[0, 0], vb_ref[0, 0]
    outs = []
    for g in range(G_B):
        h = hb * G_B + g
        outs.append(_col_attention(q[g * HEAD_DIM:(g + 1) * HEAD_DIM], kmat, vmat, k_new, v_new, tabb_ref[g], None,
                                   bias0b_ref[h], 1.0, sink_ref[h]))
    put(W_A + blk, jnp.concatenate(outs, axis=0))
    kbo_ref[0, 0] = _shift_in(kmat, k_new)
    vbo_ref[0, 0] = _shift_in(vmat, v_new)


def _sample_specs(bias0_a, bias0_b, sinks, tab_a, mult_a, tab_b, h_t, ka_t, va_t, kb_t, vb_t):
    ns, nh, hd, la = ka_t.shape
    _, nkv, _, lb = kb_t.shape
    assert nh // SAMPLE_HEADS == nkv and ns % LANES == 0
    smem = pl.BlockSpec(memory_space=pltpu.SMEM)
    lanes_of = lambda rows: pl.BlockSpec((rows, LANES), lambda i, j: (0, i // LANES))
    a_spec = pl.BlockSpec((1, SAMPLE_HEADS, hd, la), lambda i, j: (i, j, 0, 0))
    b_spec = pl.BlockSpec((1, 1, hd, lb), lambda i, j: (i, j, 0, 0))
    in_specs = [smem, smem, smem,
                pl.BlockSpec((SAMPLE_HEADS, 1, la), lambda i, j: (j, 0, 0)),
                pl.BlockSpec((1, la), lambda i, j: (0, 0)),
                pl.BlockSpec((G_B, 1, lb), lambda i, j: (j, 0, 0)),
                lanes_of(h_t.shape[0]), a_spec, a_spec, b_spec, b_spec]
    out_specs = [lanes_of(W_A + W_BQ), a_spec, a_spec, b_spec, b_spec]
    shapes = [jax.ShapeDtypeStruct((W_A + W_BQ, ns), F32)] + [jax.ShapeDtypeStruct(c.shape, F32)
                                                              for c in (ka_t, va_t, kb_t, vb_t)]
    return in_specs, out_specs, shapes


TAIL_TB = 512


def _unpermute(pt_ref, blocks_ref, exact):
    d = blocks_ref.shape[1]
    chunk = pt_ref.shape[0]
    per = chunk // d
    out = []
    for c in range(d * blocks_ref.shape[2] // chunk):
        x = jnp.concatenate([blocks_ref[0, r, c * per:(c + 1) * per] for r in range(d)], axis=0)
        if exact:
            out.append(sum(_dot(pt_ref[...], part) for part in _split3(x)))
        else:
            out.append(_dot(pt_ref[...], x))
    return jnp.concatenate(out, axis=0)


def _tail_prompt_kernel(x_ref, o1_ref, l1_ref, o4_ref, l4_ref, o16_ref, l16_ref, ob_ref, pt4_ref, pt16_ref,
                        spread_ref, w_ref, g_ref, h_ref, xn_ref):
    o_g = [o1_ref[0].astype(F32), _unpermute(pt4_ref, o4_ref, False), _unpermute(pt16_ref, o16_ref, False)]
    l_g = [l1_ref[0], _unpermute(pt4_ref, l4_ref, True), _unpermute(pt16_ref, l16_ref, True)]
    m = jnp.maximum(jnp.maximum(l_g[0], l_g[1]), l_g[2])
    e_g = [jnp.exp(l - m) for l in l_g]
    den = e_g[0] + e_g[1] + e_g[2]
    oa = None
    for e, o in zip(e_g, o_g):
        w = e / den
        w_wide = sum(_dot(part, spread_ref[...]) for part in _split2(w))
        oa = w_wide * o if oa is None else oa + w_wide * o
    o = jnp.concatenate([oa.astype(MXU_DTYPE), ob_ref[0]], axis=-1)
    h = x_ref[0] + _dot(o, w_ref[...])
    h_ref[0] = h
    xn_ref[0] = _rms(h, g_ref[...])


def _tail_prompt(x, o1, l1, o4, l4, o16, l16, ob, w_out, g):
    n, s, dm = x.shape
    tb = TAIL_TB
    pt4 = jnp.asarray(_residue_perm(PERM_CHUNK[4], 4).T, MXU_DTYPE)
    pt16 = jnp.asarray(_residue_perm(PERM_CHUNK[16], 16).T, MXU_DTYPE)
    spread = np.zeros((LANES, W_A), np.float32)
    spread[np.arange(W_A) // HEAD_DIM, np.arange(W_A)] = 1.0
    spread = jnp.asarray(spread, MXU_DTYPE)
    nat = lambda w: pl.BlockSpec((1, tb, w), lambda i, j: (i, j, 0))
    res = lambda d, w: pl.BlockSpec((1, d, tb // d, w), lambda i, j: (i, 0, j, 0))
    const = lambda a: pl.BlockSpec(a.shape, lambda i, j: (0,) * a.ndim)
    return pl.pallas_call(
        _tail_prompt_kernel,
        grid=(n, s // tb),
        in_specs=[nat(dm), nat(W_A), nat(LANES), res(4, W_A), res(4, LANES), res(16, W_A), res(16, LANES),
                  nat(W_BQ), const(pt4), const(pt16), const(spread), const(w_out), const(g)],
        out_shape=[jax.ShapeDtypeStruct((n, s, dm), F32), jax.ShapeDtypeStruct((n, s, dm), F32)],
        out_specs=[nat(dm), nat(dm)],
        compiler_params=_cparams(("parallel", "parallel")),
        name="tail_prompt",
    )(x, o1, l1, o4, l4, o16, l16, ob, pt4, pt16, spread, w_out, g)


def _tail_sample_kernel(x_ref, ot_ref, w_ref, g_ref, h_ref, xn_ref):
    h = x_ref[...] + _dot(ot_ref[...].T.astype(MXU_DTYPE), w_ref[...])
    h_ref[...] = h
    xn_ref[...] = _rms(h, g_ref[...])


def _tail_sample(x, o_t, w_out, g):
    return pl.pallas_call(
        _tail_sample_kernel,
        out_shape=[jax.ShapeDtypeStruct(x.shape, F32), jax.ShapeDtypeStruct(x.shape, F32)],
        compiler_params=_cparams(None),
        name="tail_sample",
    )(x, o_t, w_out, g)


def _sorting_network(n):
    size = 1 << (n - 1).bit_length()

    def merge(lo, hi, r):
        step = r * 2
        if step < hi - lo:
            yield from merge(lo, hi, step)
            yield from merge(lo + r, hi, step)
            yield from ((i, i + r) for i in range(lo + r, hi - r, step))
        else:
            yield (lo, lo + r)

    def sort(lo, hi):
        if hi - lo >= 1:
            mid = lo + (hi - lo) // 2
            yield from sort(lo, mid)
            yield from sort(mid + 1, hi)
            yield from merge(lo, hi, 1)

    return [(i, j) for i, j in sort(0, size - 1) if j < n]


def _top_rows(s, count):
    tiles = [s[v * SUBLANES:(v + 1) * SUBLANES] for v in range(s.shape[0] // SUBLANES)]
    for i, j in _sorting_network(len(tiles)):
        tiles[i], tiles[j] = jnp.maximum(tiles[i], tiles[j]), jnp.minimum(tiles[i], tiles[j])
    rows = []
    for t in range(count):
        m = jnp.max(tiles[0], axis=0, keepdims=True)
        rows.append(m)
        if t + 1 < count:
            popped = tiles[0] == m
            depth = min(len(tiles), count - t - 1)
            for k in range(depth):
                below = tiles[k + 1] if k + 1 < len(tiles) else NEG_INF
                tiles[k] = jnp.where(popped, below, tiles[k])
    return rows


def _pad_rows(rows, count):
    pad = [jnp.full_like(rows[0], NEG_INF)] * (count - len(rows))
    return jnp.concatenate(list(rows) + pad, axis=0)


def _candidate_sums(top1, top2):
    k = PEER_TOPK + 1
    wide = -(-k // SUBLANES) * SUBLANES
    narrow = -(-(k // 2) // SUBLANES) * SUBLANES
    assert k // (narrow + 1) <= 1
    v2_wide = _pad_rows(top2, wide)
    v2_narrow = v2_wide[:narrow]
    parts = [top1[0] + v2_wide] + [top1[a] + v2_narrow for a in range(1, narrow)]
    parts.append(_pad_rows(top1[narrow:], -(-(k - narrow) // SUBLANES) * SUBLANES) + top2[0])
    return jnp.concatenate(parts, axis=0)


def _route_kernel(x_ref, wqh_ref, wql_ref, skh_ref, skl_ref, thr_ref, p1_ref, p2_ref):
    xh, xl = _split2(x_ref[...])
    wqh = wqh_ref[...]
    q_t = _dot_nt(wqh, xh) + _dot_nt(wqh, xl) + _dot_nt(wql_ref[...], xh)
    half = D_KEY // 2
    for h in range(PEER_HEADS):
        scores = []
        for c in range(2):
            qh, ql = _split2(q_t[h * D_KEY + c * half:h * D_KEY + (c + 1) * half, :])
            scores.append(_dot(skh_ref[c], qh) + _dot(skh_ref[c], ql) + _dot(skl_ref[c], qh))
        for g in range(q_t.shape[1] // LANES):
            s1, s2 = (s[:, g * LANES:(g + 1) * LANES] for s in scores)
            top1, top2 = _top_rows(s1, PEER_TOPK + 1), _top_rows(s2, PEER_TOPK + 1)
            best = _top_rows(_candidate_sums(top1, top2), PEER_TOPK + 1)
            tau = 0.5 * (best[PEER_TOPK - 1] + best[PEER_TOPK])
            z = sum(jnp.exp(b - best[0]) for b in best[:PEER_TOPK])
            log_norm = best[0] + jnp.log(z)
            m2 = top2[0]
            thr_ref[h, g] = jnp.exp((tau - m2) - s1)
            p1_ref[h, g] = jnp.exp(s1 + (m2 - log_norm))
            p2_ref[h, g] = jnp.exp(s2 - m2)


def _route(xn, wq_t_hi, wq_t_lo, sk_hi, sk_lo, tb):
    t, dm = xn.shape
    const = lambda a: pl.BlockSpec(a.shape, lambda i: (0,) * a.ndim)
    gspec = pl.BlockSpec((PEER_HEADS, tb // LANES, N_KEYS, LANES), lambda i: (0, i, 0, 0))
    gshape = jax.ShapeDtypeStruct((PEER_HEADS, t // LANES, N_KEYS, LANES), F32)
    return pl.pallas_call(
        _route_kernel,
        grid=(t // tb,),
        in_specs=[pl.BlockSpec((tb, dm), lambda i: (i, 0)), const(wq_t_hi), const(wq_t_lo), const(sk_hi), const(sk_lo)],
        out_shape=[gshape] * 3,
        out_specs=[gspec] * 3,
        compiler_params=_cparams(("parallel",)),
        name="peer_route",
    )(xn, wq_t_hi, wq_t_lo, sk_hi, sk_lo)


EXPERT_CHUNK = SUBLANES * N_KEYS
GATE_JBLOCK = 2 * SUBLANES
INV_SQRT2 = 0.7071067811865476


def _gate_rows(thr_ref, p1_ref, p2_ref, w_ref):
    nv = GATE_JBLOCK // SUBLANES

    def row_group(i8, carry):
        base = pl.multiple_of(i8 * SUBLANES, SUBLANES)
        thr_t = [thr_ref[h, 0, pl.ds(base, SUBLANES), :] for h in range(PEER_HEADS)]
        p1_t = [p1_ref[h, 0, pl.ds(base, SUBLANES), :] for h in range(PEER_HEADS)]
        for jb in range(N_KEYS // GATE_JBLOCK):
            w = [[None] * nv for _ in range(SUBLANES)]
            for h in range(PEER_HEADS):
                p2 = [p2_ref[h, 0, pl.ds(jb * GATE_JBLOCK + v * SUBLANES, SUBLANES), :] for v in range(nv)]
                for r in range(SUBLANES):
                    thr = jnp.broadcast_to(thr_t[h][r:r + 1, :], (SUBLANES, LANES))
                    p1 = jnp.broadcast_to(p1_t[h][r:r + 1, :], (SUBLANES, LANES))
                    for v in range(nv):
                        term = jnp.where(p2[v] >= thr, p1 * p2[v], 0.0)
                        w[r][v] = term if w[r][v] is None else w[r][v] + term
            for r in range(SUBLANES):
                row0 = pl.multiple_of((base + r) * N_KEYS + jb * GATE_JBLOCK, GATE_JBLOCK)
                w_ref[0, pl.ds(row0, GATE_JBLOCK), :] = jnp.concatenate(w[r], axis=0).astype(w_ref.dtype)
        return carry

    lax.fori_loop(0, thr_ref.shape[2] // SUBLANES, row_group, 0)


GATE_ROWS = N_KEYS // 2


def _gate_specs(t):
    rows = pl.BlockSpec((PEER_HEADS, 1, GATE_ROWS, LANES), lambda i, j: (0, i, j, 0))
    full = pl.BlockSpec((PEER_HEADS, 1, N_KEYS, LANES), lambda i, j: (0, i, 0, 0))
    out = pl.BlockSpec((1, GATE_ROWS * N_KEYS, LANES), lambda i, j: (i, j, 0))
    shape = jax.ShapeDtypeStruct((t // LANES, N_KEYS * N_KEYS, LANES), MXU_DTYPE)
    return [rows, rows, full], out, shape


def _gates(thr, p1, p2):
    t = thr.shape[1] * LANES
    in_specs, out_spec, shape = _gate_specs(t)
    return pl.pallas_call(
        _gate_rows,
        grid=(t // LANES, N_KEYS // GATE_ROWS),
        in_specs=in_specs, out_specs=out_spec, out_shape=shape,
        compiler_params=_cparams(("parallel", "parallel")),
        name="peer_gates",
    )(thr, p1, p2)


def _sample_and_gates_kernel(*refs):
    n_in, n_gate = 11, 3
    sample_in, gate_in = refs[:n_in], refs[n_in:n_in + n_gate]
    sample_out, gate_out = refs[n_in + n_gate:-1], refs[-1]
    _sample_kernel(*sample_in, *sample_out)
    _gate_rows(*gate_in, gate_out)


def _sample_and_gates(sample_args, thr, p1, p2):
    t = thr.shape[1] * LANES
    ns, nkv = sample_args[7].shape[0], sample_args[9].shape[1]
    s_in, s_out, s_shapes = _sample_specs(*sample_args)
    if t // LANES != ns or N_KEYS // GATE_ROWS != nkv:
        outs = pl.pallas_call(
            _sample_kernel, grid=(ns, nkv), in_specs=s_in, out_specs=s_out, out_shape=s_shapes,
            compiler_params=_cparams(("arbitrary", "arbitrary")), name="sample_attn",
        )(*sample_args)
        return list(outs) + [_gates(thr, p1, p2)]
    g_in, g_out, g_shape = _gate_specs(t)
    return pl.pallas_call(
        _sample_and_gates_kernel,
        grid=(ns, nkv),
        in_specs=s_in + g_in, out_specs=s_out + [g_out], out_shape=s_shapes + [g_shape],
        compiler_params=_cparams(("arbitrary", "arbitrary")),
        name="sample_attn_and_gates",
    )(*sample_args, thr, p1, p2)


def _activate(g, a_ref, p_ref, w_ref):
    blk = 8 * SUBLANES
    for b in range(a_ref.shape[1] // blk):
        rows = pl.ds(b * blk, blk)
        a = a_ref[g, rows, :]
        half = 0.5 * a
        act = half + half * lax.erf(a * INV_SQRT2)
        p_ref[g, rows, :] = w_ref[g, rows, :] * act.astype(p_ref.dtype)


def _experts_kernel(xn_ref, res_ref, g_ref, down_ref, upt_ref, w_a_ref, w_b_ref,
                    y_ref, xb_ref, a0_ref, a1_ref, pb0_ref, pb1_ref, acc_ref):
    i, j = pl.program_id(0), pl.program_id(1)
    ch = EXPERT_CHUNK
    ng = a0_ref.shape[0]

    @pl.when((i == 0) & (j == 0))
    def _():
        for ref in (a0_ref, a1_ref, pb0_ref, pb1_ref):
            ref[...] = jnp.zeros_like(ref)

    @pl.when(j == 0)
    def _():
        xb_ref[...] = xn_ref[...].astype(MXU_DTYPE)

    keep = j >= 1

    def tick(half, a_in_ref, a_out_ref, p_out_ref, p_in_ref, w_ref):
        p_in = jnp.concatenate([p_in_ref[gg] for gg in range(ng)], axis=1)
        upd = _dot(upt_ref[:, half * ch:(half + 1) * ch], p_in)
        nxt = _dot_nt(down_ref[pl.ds(half * ch, ch), :], xb_ref[...])
        for g in range(ng):
            a_out_ref[g] = nxt[:, g * LANES:(g + 1) * LANES]
            _activate(g, a_in_ref, p_out_ref, w_ref)
        return upd

    upd = tick(0, a1_ref, a0_ref, pb1_ref, pb0_ref, w_a_ref)
    upd = upd + tick(1, a0_ref, a1_ref, pb0_ref, pb1_ref, w_b_ref)
    acc_ref[...] = jnp.where(keep, acc_ref[...] + upd, 0.0)

    @pl.when(j == pl.num_programs(1) - 1)
    def _():
        y = res_ref[...] + acc_ref[...].T
        y_ref[...] = _rms(y, g_ref[...])


def _experts(xn, res, g, down, up_t, w, tb):
    t, dm = xn.shape
    ne = down.shape[0]
    step = 2 * EXPERT_CHUNK
    nj = ne // step
    ng = tb // LANES
    tok = pl.BlockSpec((tb, dm), lambda i, j: (i, 0))
    w_a = pl.BlockSpec((ng, EXPERT_CHUNK, LANES), lambda i, j: (i, jnp.maximum(2 * j - 1, 0), 0))
    w_b = pl.BlockSpec((ng, EXPERT_CHUNK, LANES), lambda i, j: (i, jnp.minimum(2 * j, 2 * nj - 1), 0))
    return pl.pallas_call(
        _experts_kernel,
        grid=(t // tb, nj + 1),
        in_specs=[tok, tok, pl.BlockSpec((1, dm), lambda i, j: (0, 0)),
                  pl.BlockSpec((step, dm), lambda i, j: (jnp.minimum(j, nj - 1), 0)),
                  pl.BlockSpec((dm, step), lambda i, j: (0, jnp.maximum(j - 1, 0))),
                  w_a, w_b],
        out_shape=jax.ShapeDtypeStruct((t, dm), F32),
        out_specs=tok,
        scratch_shapes=[pltpu.VMEM((tb, dm), MXU_DTYPE),
                        pltpu.VMEM((ng, EXPERT_CHUNK, LANES), F32), pltpu.VMEM((ng, EXPERT_CHUNK, LANES), F32),
                        pltpu.VMEM((ng, EXPERT_CHUNK, LANES), MXU_DTYPE),
                        pltpu.VMEM((ng, EXPERT_CHUNK, LANES), MXU_DTYPE),
                        pltpu.VMEM((dm, tb), F32)],
        compiler_params=_cparams(("arbitrary", "arbitrary")),
        name="peer_experts",
    )(xn, res, g, down, up_t, w, w)


PEER_TB = 512


def _cache_to_feature_major(c):
    return jnp.transpose(c, (0, 2, 3, 1))


def _cache_from_feature_major(c):
    return jnp.transpose(c, (0, 3, 1, 2))


def _layer(xp, xs, cache_a_k, cache_a_v, cache_b_k, cache_b_v, norm_attn, w_in, rel_bias, sinks, w_out, norm_ffn,
           w_peer_q, peer_sub_keys, peer_down, peer_up, g_final):
    n, s, dm = xp.shape
    ns = xs.shape[0]
    la, lb = cache_a_k.shape[1], cache_b_k.shape[1]

    c3 = 3 * W_A
    w_q_scaled = jnp.concatenate([w_in[:, :W_A] * SCALE, w_in[:, W_A:c3], w_in[:, c3:c3 + W_BQ] * SCALE,
                                  w_in[:, c3 + W_BQ:]], axis=1)
    assert W_BKV == LANES
    w_ext = w_q_scaled.astype(MXU_DTYPE)
    w_nat_t = w_q_scaled.T.astype(MXU_DTYPE)
    w_out_b = w_out.astype(MXU_DTYPE)
    g_attn, g_ffn = norm_attn[None, :], norm_ffn[None, :]
    wq_t = w_peer_q.T
    wq_t_hi = wq_t.astype(MXU_DTYPE)
    wq_t_lo = (wq_t - wq_t_hi.astype(F32)).astype(MXU_DTYPE)
    sk_hi = peer_sub_keys.astype(MXU_DTYPE)
    sk_lo = (peer_sub_keys - sk_hi.astype(F32)).astype(MXU_DTYPE)
    peer_w = (wq_t_hi, wq_t_lo, sk_hi, sk_lo, peer_down.astype(MXU_DTYPE), peer_up.T.astype(MXU_DTYPE))

    bidx_sa, mult_sa = _sample_tables(la)
    bidx_sb, _ = _sample_tables(lb)
    assert H_A == H_B
    *tabs_a, tab_b, tab_sa, tab_sb = _bias_tables(
        rel_bias, H_A, [(_window_bucket_matrix(d), 0) for d in DILATIONS]
        + [(_window_bucket_matrix(1), H_A), (bidx_sa, 0), (bidx_sb, H_A)])
    bias0_a, bias0_b = rel_bias[0, :H_A], rel_bias[0, H_A:]

    (q1, k1, v1, kf, vf, q4, k4, v4, q16, k16, v16, qb, qbs, kb, vb, vbs, kbf, vbf) = _qkv_prompt(xp, g_attn, w_ext)
    o1, l1 = _swa(tabs_a[0], [q1], k1, [v1], HEAD_CFG_A)
    flat = lambda a: a.reshape((a.shape[0] * a.shape[1],) + a.shape[2:])
    o4, l4 = _swa(tabs_a[1], [flat(q4)], flat(k4), [flat(v4)], HEAD_CFG_A)
    o16, l16 = _swa(tabs_a[2], [flat(q16)], flat(k16), [flat(v16)], HEAD_CFG_A)
    ob = _swa(tab_b, [qb, qbs], kb, [vb, vbs], HEAD_CFG_B, sinks=sinks, want_lse=False)
    unflat = lambda a, d: a.reshape((n, d) + a.shape[1:])
    hp, xnp = _tail_prompt(xp, o1, l1, unflat(o4, 4), unflat(l4, 4), unflat(o16, 16), unflat(l16, 16), ob,
                           w_out_b, g_ffn)
    prompt_caches = (kf[:, s - min(la, s):].reshape(n, -1, H_A, HEAD_DIM),
                     vf[:, s - min(la, s):].reshape(n, -1, H_A, HEAD_DIM),
                     kbf[:, s - min(lb, s):].reshape(n, -1, KV_B, HEAD_DIM),
                     vbf[:, s - min(lb, s):].reshape(n, -1, KV_B, HEAD_DIM))

    hp, xnp = hp.reshape(n * s, dm), xnp.reshape(n * s, dm)
    route_w, (down_b, up_t_b) = peer_w[:4], peer_w[4:]
    routing_p = _route(xnp, *route_w, PEER_TB)

    hs_t = _qkv_sample(xs, g_attn, w_nat_t)
    sample_args = (bias0_a, bias0_b, sinks, tab_sa, jnp.asarray(mult_sa), tab_sb, hs_t,
                   _cache_to_feature_major(cache_a_k), _cache_to_feature_major(cache_a_v),
                   _cache_to_feature_major(cache_b_k), _cache_to_feature_major(cache_b_v))
    o_t, *shifted, gates_p = _sample_and_gates(sample_args, *routing_p)
    h_s, xn_s = _tail_sample(xs, o_t, w_out_b, g_ffn)
    sample_caches = tuple(_cache_from_feature_major(c) for c in shifted)

    y_p = _experts(xnp, hp, g_final, down_b, up_t_b, gates_p, PEER_TB)
    gates_s = _gates(*_route(xn_s, *route_w, LANES))
    y_s = _experts(xn_s, h_s, g_final, down_b, up_t_b, gates_s, LANES)
    return y_p.reshape(n, s, dm), y_s, prompt_caches, sample_caches


def kernel(x_prompt, x_sample, cache_a_k, cache_a_v, cache_b_k, cache_b_v, norm_attn, w_in, rel_bias, sinks, w_out,
           norm_ffn, w_peer_q, peer_sub_keys, peer_down, peer_up, norm_final):
    depth = w_in.shape[0]
    assert depth == 1, "single-layer trunk"
    n, s, dm = x_prompt.shape
    ns = x_sample.shape[0]
    assert x_sample.shape[1] == 1 and s % QKV_TB == 0 and (n * s) % PEER_TB == 0 and ns % LANES == 0
    l = 0
    y_prompt, y_sample, prompt_caches, sample_caches = _layer(
        x_prompt, x_sample[:, 0], cache_a_k[l], cache_a_v[l], cache_b_k[l], cache_b_v[l], norm_attn[l], w_in[l],
        rel_bias, sinks[l], w_out[l], norm_ffn[l], w_peer_q[l], peer_sub_keys[l], peer_down[l], peer_up[l],
        norm_final[None, :])
    return ((y_prompt, y_sample.reshape(ns, 1, dm)) + tuple(c[None] for c in prompt_caches)
            + tuple(c[None] for c in sample_caches))
```

```python
import functools
import math

import numpy as np
import jax
import jax.numpy as jnp
from jax import lax
from jax.experimental import pallas as pl
from jax.experimental.pallas import tpu as pltpu

HEAD_DIM = 64
H_A = 8
H_B = 8
KV_B = 2
G_B = H_B // KV_B
DILATIONS = (1, 4, 16)
SLOTS = 128
N_BUCKETS = 32
MAX_DISTANCE = 2048
N_KEYS = 128
PEER_HEADS = 8
PEER_TOPK = 16
D_KEY = 128
RMS_EPS = 1e-6
SCALE = HEAD_DIM ** -0.5
W_A = H_A * HEAD_DIM
W_BQ = H_B * HEAD_DIM
W_BKV = KV_B * HEAD_DIM

LANES = 128
SUBLANES = 8
VMEM_LIMIT_BYTES = 56 * 1024 * 1024

MXU_DTYPE = jnp.bfloat16
F32 = jnp.float32
NEG_INF = float("-inf")


def _cparams(sem):
    return pltpu.CompilerParams(dimension_semantics=sem, vmem_limit_bytes=VMEM_LIMIT_BYTES)


def _t5_bucket_np(dist):
    exact = N_BUCKETS // 2
    d = np.maximum(dist, 0)
    logd = np.log(np.maximum(d, 1).astype(np.float32) / np.float32(exact)) / np.float32(math.log(MAX_DISTANCE / exact))
    large = np.minimum(exact + (logd * np.float32(N_BUCKETS - exact)).astype(np.int32), N_BUCKETS - 1)
    return np.where(d < exact, d, large).astype(np.int32)


def _window_bucket_matrix(dilation):
    i = np.arange(SLOTS)[:, None]
    j = np.arange(2 * SLOTS)[None, :]
    dist = i - j + SLOTS
    ok = (dist >= 0) & (dist <= SLOTS)
    return np.where(ok, _t5_bucket_np(dist * dilation), -1).astype(np.int32)


def _sample_tables(cache_len):
    delta = cache_len - np.arange(cache_len)
    mult = np.zeros(cache_len, np.float32)
    for d in DILATIONS:
        mult += ((delta % d == 0) & (delta <= SLOTS * d)).astype(np.float32)
    return _t5_bucket_np(delta)[None, :], mult[None, :]


def _residue_perm(tb, d):
    p = np.zeros((tb, tb), np.float32)
    rows = np.arange(tb)
    p[rows, (rows % (tb // d)) * d + rows // (tb // d)] = 1.0
    return p


def _rms(x, g):
    return x * lax.rsqrt(jnp.mean(x * x, axis=-1, keepdims=True) + RMS_EPS) * g


def _dot(a, b):
    return jnp.dot(a, b, preferred_element_type=F32)


def _dot_nt(a, b):
    return lax.dot_general(a, b, (((1,), (1,)), ((), ())), preferred_element_type=F32)


def _split2(x):
    hi = x.astype(MXU_DTYPE)
    lo = (x - hi.astype(F32)).astype(MXU_DTYPE)
    return hi, lo


def _split3(x):
    hi = x.astype(MXU_DTYPE)
    r1 = x - hi.astype(F32)
    mid = r1.astype(MXU_DTYPE)
    lo = (r1 - mid.astype(F32)).astype(MXU_DTYPE)
    return hi, mid, lo


def _bias_kernel(col0s, nheads, rb_ref, *refs):
    n = len(col0s)
    for col0, bidx_ref, o_ref in zip(col0s, refs[:n], refs[n:]):
        b = bidx_ref[...]
        for h in range(nheads):
            val = jnp.full(b.shape, NEG_INF, F32)
            for k in range(N_BUCKETS):
                val = jnp.where(b == k, rb_ref[k, col0 + h], val)
            o_ref[h] = val


def _bias_tables(rel_bias, nheads, tables):
    bidx = [jnp.asarray(b) for b, _ in tables]
    return pl.pallas_call(
        functools.partial(_bias_kernel, tuple(c for _, c in tables), nheads),
        out_shape=[jax.ShapeDtypeStruct((nheads,) + b.shape, F32) for b in bidx],
        in_specs=[pl.BlockSpec(memory_space=pltpu.SMEM)] + [pl.BlockSpec(b.shape, lambda: (0, 0)) for b in bidx],
        out_specs=[pl.BlockSpec((nheads,) + b.shape, lambda: (0, 0, 0)) for b in bidx],
        name="bias_tables",
    )(rel_bias, *bidx)


QKV_TB = 512
PERM_CHUNK = {4: 128, 16: 256}


def _qkv_prompt_kernel(rows_a, rows_b, x_ref, g_ref, w_ref, p4_ref, p16_ref,
                       q1_ref, k1_ref, v1_ref, kf_ref, vf_ref,
                       q4_ref, k4_ref, v4_ref, q16_ref, k16_ref, v16_ref,
                       qb_ref, qbs_ref, kb_ref, vb_ref, vbs_ref, kbf_ref, vbf_ref):
    xn = _rms(x_ref[0], g_ref[...])
    h = _dot(xn.astype(MXU_DTYPE), w_ref[...])
    c0, c1, c2, c3 = 0, W_A, 2 * W_A, 3 * W_A
    c4, c5, c6 = c3 + W_BQ, c3 + W_BQ + W_BKV, c3 + W_BQ + 2 * W_BKV
    swap_halves = lambda t: pltpu.roll(t, HEAD_DIM, axis=1)
    ha = h[:, :c3].astype(MXU_DTYPE)
    q1_ref[0] = ha[:, c0:c1]
    k1_ref[0] = ha[:, c1:c2]
    v1_ref[0] = ha[:, c2:c3]
    kf_ref[0] = h[rows_a, c1:c2]
    vf_ref[0] = h[rows_a, c2:c3]
    qb_ref[0] = h[:, c3:c4].astype(MXU_DTYPE)
    kb_ref[0] = h[:, c4:c5].astype(MXU_DTYPE)
    vb_ref[0] = h[:, c5:c6].astype(MXU_DTYPE)
    kbf_ref[0] = h[rows_b, c4:c5]
    vbf_ref[0] = h[rows_b, c5:c6]
    qbs_ref[0] = jnp.concatenate([swap_halves(h[:, c:c + LANES]) for c in range(c3, c4, LANES)],
                                 axis=1).astype(MXU_DTYPE)
    vbs_ref[0] = swap_halves(h[:, c5:c6]).astype(MXU_DTYPE)
    tb = ha.shape[0]
    for d, p_ref, outs in ((4, p4_ref, (q4_ref, k4_ref, v4_ref)), (16, p16_ref, (q16_ref, k16_ref, v16_ref))):
        chunk = p_ref.shape[0]
        per = chunk // d
        for c in range(tb // chunk):
            perm = _dot(p_ref[...], ha[c * chunk:(c + 1) * chunk]).astype(MXU_DTYPE)
            for r in range(d):
                for t, o_ref in enumerate(outs):
                    o_ref[0, r, c * per:(c + 1) * per] = perm[r * per:(r + 1) * per, t * W_A:(t + 1) * W_A]


def _qkv_prompt(x, g, w_ext, keep_a, keep_b):
    n, s, dm = x.shape
    tb = QKV_TB
    nb = s // tb

    def tail(width, keep):
        if keep >= tb:
            assert keep % tb == 0
            first = nb - keep // tb
            spec = pl.BlockSpec((1, tb, width), lambda i, j: (i, jnp.maximum(j - first, 0), 0))
            return (jax.ShapeDtypeStruct((n, keep, width), F32), spec), slice(None)
        spec = pl.BlockSpec((1, keep, width), lambda i, j: (i, 0, 0))
        return (jax.ShapeDtypeStruct((n, keep, width), F32), spec), slice(tb - keep, tb)

    tail_a, rows_a = tail(W_A, keep_a)
    tail_b, rows_b = tail(W_BKV, keep_b)
    p4 = jnp.asarray(_residue_perm(PERM_CHUNK[4], 4), MXU_DTYPE)
    p16 = jnp.asarray(_residue_perm(PERM_CHUNK[16], 16), MXU_DTYPE)
    bf = MXU_DTYPE

    def nat(width, dt):
        return jax.ShapeDtypeStruct((n, s, width), dt), pl.BlockSpec((1, tb, width), lambda i, j: (i, j, 0))

    def res(d):
        return (jax.ShapeDtypeStruct((n, d, s // d, W_A), bf),
                pl.BlockSpec((1, d, tb // d, W_A), lambda i, j: (i, 0, j, 0)))

    outs = [nat(W_A, bf), nat(W_A, bf), nat(W_A, bf), tail_a, tail_a,
            res(4), res(4), res(4), res(16), res(16), res(16),
            nat(W_BQ, bf), nat(W_BQ, bf), nat(W_BKV, bf), nat(W_BKV, bf), nat(W_BKV, bf),
            tail_b, tail_b]
    return pl.pallas_call(
        functools.partial(_qkv_prompt_kernel, rows_a, rows_b),
        grid=(n, nb),
        in_specs=[pl.BlockSpec((1, tb, dm), lambda i, j: (i, j, 0)),
                  pl.BlockSpec((1, dm), lambda i, j: (0, 0)),
                  pl.BlockSpec(w_ext.shape, lambda i, j: (0, 0)),
                  pl.BlockSpec(p4.shape, lambda i, j: (0, 0)),
                  pl.BlockSpec(p16.shape, lambda i, j: (0, 0))],
        out_shape=[o[0] for o in outs],
        out_specs=[o[1] for o in outs],
        compiler_params=_cparams(("arbitrary", "arbitrary")),
        name="qkv_prompt",
    )(x, g, w_ext, p4, p16)


def _qkv_sample_kernel(x_ref, g_ref, wt_ref, o_ref):
    xn = _rms(x_ref[...], g_ref[...])
    o_ref[...] = _dot_nt(wt_ref[...], xn.astype(MXU_DTYPE))


def _qkv_sample(x, g, w_t):
    return pl.pallas_call(
        _qkv_sample_kernel,
        out_shape=jax.ShapeDtypeStruct((w_t.shape[0], x.shape[0]), F32),
        compiler_params=_cparams(None),
        name="qkv_sample",
    )(x, g, w_t)


SWA_TILES = 4


def _swa_kernel(head_cfg, has_sink, want_lse, *refs):
    it = iter(refs)
    tab_ref = next(it)
    sink_ref = next(it) if has_sink else None
    q_refs = [next(it)]
    if any(c[0] == 1 for c in head_cfg):
        q_refs.append(next(it))
    kc_ref, kp_ref = next(it), next(it)
    v_refs = [(next(it), next(it))]
    if any(c[3] == 1 for c in head_cfg):
        v_refs.append((next(it), next(it)))
    o_ref = next(it)
    lse_ref = next(it) if want_lse else None

    first = pl.program_id(1) == 0
    col = lax.broadcasted_iota(jnp.int32, (SLOTS, 2 * SLOTS), 1)
    prev_pen = jnp.where((col < SLOTS) & first, NEG_INF, 0.0)
    lane = lax.broadcasted_iota(jnp.int32, (SLOTS, LANES), 1)
    low = lane < HEAD_DIM
    for t in range(kc_ref.shape[1] // SLOTS):
        rows = slice(t * SLOTS, (t + 1) * SLOTS)
        before = slice((t - 1) * SLOTS, t * SLOTS)

        def keys(cur_ref, prev_ref, lanes):
            prev = prev_ref[0, :, lanes] if t == 0 else cur_ref[0, before, lanes]
            return jnp.concatenate([prev, cur_ref[0, rows, lanes]], axis=0)

        lse_acc = jnp.zeros((SLOTS, LANES), F32)
        for p in range(len(head_cfg) // 2):
            halves = []
            for hh in range(2):
                h = 2 * p + hh
                q_src, q_half, k_tile, v_src, v_tile = head_cfg[h]
                q = q_refs[q_src][0, rows, p * LANES:(p + 1) * LANES]
                q = jnp.where(low if q_half == 0 else jnp.logical_not(low), q, jnp.zeros_like(q))
                kcat = keys(kc_ref, kp_ref, slice(k_tile * LANES, (k_tile + 1) * LANES))
                s = _dot_nt(q, kcat) + tab_ref[h]
                if t == 0:
                    s = s + prev_pen
                m = jnp.max(s, axis=-1, keepdims=True)
                if has_sink:
                    m = jnp.maximum(m, sink_ref[h])
                e = jnp.exp(s - m)
                l = jnp.sum(e, axis=-1, keepdims=True)
                if has_sink:
                    l = l + jnp.exp(sink_ref[h] - m)
                vc_ref, vp_ref = v_refs[v_src]
                vcat = keys(vc_ref, vp_ref, slice(v_tile * LANES, (v_tile + 1) * LANES))
                halves.append(_dot(e.astype(MXU_DTYPE), vcat) / l)
                if want_lse:
                    lse_acc = jnp.where(lane == h, m + jnp.log(l), lse_acc)
            o_ref[0, rows, p * LANES:(p + 1) * LANES] = jnp.where(low, halves[0], halves[1]).astype(o_ref.dtype)
        if want_lse:
            lse_ref[0, rows] = lse_acc


def _swa(tab, q_list, k, v_list, head_cfg, sinks=None, want_lse=True):
    r, s, _ = q_list[0].shape
    ck = k.shape[-1]
    tiles = math.gcd(SWA_TILES, s // SLOTS)
    rows = tiles * SLOTS
    assert s % rows == 0
    nb = s // rows
    cur = lambda i, j: (i, j, 0)
    prev = lambda i, j: (i, jnp.maximum(tiles * j - 1, 0), 0)
    in_specs = [pl.BlockSpec(tab.shape, lambda i, j: (0, 0, 0))]
    args = [tab]
    if sinks is not None:
        in_specs.append(pl.BlockSpec(memory_space=pltpu.SMEM))
        args.append(sinks)
    for q in q_list:
        in_specs.append(pl.BlockSpec((1, rows, q.shape[-1]), cur))
        args.append(q)
    in_specs += [pl.BlockSpec((1, rows, ck), cur), pl.BlockSpec((1, SLOTS, ck), prev)]
    args += [k, k]
    for v in v_list:
        in_specs += [pl.BlockSpec((1, rows, ck), cur), pl.BlockSpec((1, SLOTS, ck), prev)]
        args += [v, v]
    out_shape = [jax.ShapeDtypeStruct((r, s, W_A), MXU_DTYPE)]
    out_specs = [pl.BlockSpec((1, rows, W_A), cur)]
    if want_lse:
        out_shape.append(jax.ShapeDtypeStruct((r, s, LANES), F32))
        out_specs.append(pl.BlockSpec((1, rows, LANES), cur))
    out = pl.pallas_call(
        functools.partial(_swa_kernel, tuple(head_cfg), sinks is not None, want_lse),
        grid=(r, nb),
        in_specs=in_specs,
        out_shape=out_shape,
        out_specs=out_specs,
        compiler_params=_cparams(("parallel", "arbitrary")),
        name="swa",
    )(*args)
    return out if want_lse else out[0]


HEAD_CFG_A = tuple((0, h % 2, h // 2, 0, h // 2) for h in range(H_A))


def _head_cfg_b():
    cfg = []
    for h in range(H_B):
        c = h // G_B
        src = 0 if h % 2 == c else 1
        cfg.append((src, c, 0, src, 0))
    return tuple(cfg)


HEAD_CFG_B = _head_cfg_b()


def _shift_in(x, new_col):
    length = x.shape[-1]
    rolled = pltpu.roll(x, length - 1, axis=1)
    lane = lax.broadcasted_iota(jnp.int32, x.shape, 1)
    return jnp.where(lane == length - 1, new_col, rolled)


def _col_attention(q, kmat, vmat, k_new, v_new, bias_row, mult_row, bias_new, mult_new, sink):
    s = jnp.sum(q * kmat, axis=0, keepdims=True) + bias_row
    s_new = jnp.sum(q * k_new, axis=0, keepdims=True) + bias_new
    if mult_row is not None:
        s = jnp.where(mult_row > 0.0, s, NEG_INF)
    m = jnp.maximum(jnp.max(s, axis=-1, keepdims=True), s_new)
    if sink is not None:
        m = jnp.maximum(m, sink)
    e = jnp.exp(s - m)
    if mult_row is not None:
        e = e * mult_row
    e_new = mult_new * jnp.exp(s_new - m)
    l = jnp.sum(e, axis=-1, keepdims=True) + e_new
    if sink is not None:
        l = l + jnp.exp(sink - m)
    o = jnp.sum(vmat * e, axis=-1, keepdims=True) + v_new * e_new
    return o / l


SAMPLE_HEADS = H_A // KV_B
assert SAMPLE_HEADS == G_B


def _column(ht_ref, row0, nrows, pick):
    return jnp.sum(jnp.where(pick, ht_ref[pl.ds(row0, nrows), :], 0.0), axis=1, keepdims=True)


def _sample_kernel(bias0a_ref, bias0b_ref, sink_ref, taba_ref, mult_ref, tabb_ref, ht_ref,
                   ka_ref, va_ref, kb_ref, vb_ref, ot_ref, kao_ref, vao_ref, kbo_ref, vbo_ref):
    n, hb = pl.program_id(0), pl.program_id(1)
    lane = n % LANES
    pick = lax.broadcasted_iota(jnp.int32, (1, LANES), 1) == lane
    rows = SAMPLE_HEADS * HEAD_DIM
    blk = pl.multiple_of(hb * rows, rows)
    kvr = pl.multiple_of(hb * HEAD_DIM, HEAD_DIM)

    @pl.when((lane == 0) & (hb == 0))
    def _():
        ot_ref[...] = jnp.zeros_like(ot_ref)

    def put(row0, col):
        ot_ref[pl.ds(row0, rows), :] = jnp.where(pick, col, ot_ref[pl.ds(row0, rows), :])

    q, k_new, v_new = (_column(ht_ref, c0 + blk, rows, pick) for c0 in (0, W_A, 2 * W_A))
    mult = mult_ref[...]
    outs = []
    for hl in range(SAMPLE_HEADS):
        r = slice(hl * HEAD_DIM, (hl + 1) * HEAD_DIM)
        kmat, vmat = ka_ref[0, hl], va_ref[0, hl]
        outs.append(_col_attention(q[r], kmat, vmat, k_new[r], v_new[r], taba_ref[hl], mult,
                                   bias0a_ref[hb * SAMPLE_HEADS + hl], float(len(DILATIONS)), None))
        kao_ref[0, hl] = _shift_in(kmat, k_new[r])
        vao_ref[0, hl] = _shift_in(vmat, v_new[r])
    put(blk, jnp.concatenate(outs, axis=0))

    c3 = 3 * W_A
    q = _column(ht_ref, c3 + blk, rows, pick)
    k_new = _column(ht_ref, c3 + W_BQ + kvr, HEAD_DIM, pick)
    v_new = _column(ht_ref, c3 + W_BQ + W_BKV + kvr, HEAD_DIM, pick)
    kmat, vmat = kb_ref[0, 0], vb_ref[0, 0]
    outs = []
    for g in range(G_B):
        h = hb * G_B + g
        outs.append(_col_attention(q[g * HEAD_DIM:(g + 1) * HEAD_DIM], kmat, vmat, k_new, v_new, tabb_ref[g], None,
                                   bias0b_ref[h], 1.0, sink_ref[h]))
    put(W_A + blk, jnp.concatenate(outs, axis=0))
    kbo_ref[0, 0] = _shift_in(kmat, k_new)
    vbo_ref[0, 0] = _shift_in(vmat, v_new)


def _sample_specs(bias0_a, bias0_b, sinks, tab_a, mult_a, tab_b, h_t, ka_t, va_t, kb_t, vb_t):
    ns, nh, hd, la = ka_t.shape
    _, nkv, _, lb = kb_t.shape
    assert nh // SAMPLE_HEADS == nkv and ns % LANES == 0
    smem = pl.BlockSpec(memory_space=pltpu.SMEM)
    lanes_of = lambda rows: pl.BlockSpec((rows, LANES), lambda i, j: (0, i // LANES))
    a_spec = pl.BlockSpec((1, SAMPLE_HEADS, hd, la), lambda i, j: (i, j, 0, 0))
    b_spec = pl.BlockSpec((1, 1, hd, lb), lambda i, j: (i, j, 0, 0))
    in_specs = [smem, smem, smem,
                pl.BlockSpec((SAMPLE_HEADS, 1, la), lambda i, j: (j, 0, 0)),
                pl.BlockSpec((1, la), lambda i, j: (0, 0)),
                pl.BlockSpec((G_B, 1, lb), lambda i, j: (j, 0, 0)),
                lanes_of(h_t.shape[0]), a_spec, a_spec, b_spec, b_spec]
    out_specs = [lanes_of(W_A + W_BQ), a_spec, a_spec, b_spec, b_spec]
    shapes = [jax.ShapeDtypeStruct((W_A + W_BQ, ns), F32)] + [jax.ShapeDtypeStruct(c.shape, F32)
                                                              for c in (ka_t, va_t, kb_t, vb_t)]
    return in_specs, out_specs, shapes


TAIL_TB = 512


def _unpermute(pt_ref, blocks_ref, exact):
    d = blocks_ref.shape[1]
    chunk = pt_ref.shape[0]
    per = chunk // d
    out = []
    for c in range(d * blocks_ref.shape[2] // chunk):
        x = jnp.concatenate([blocks_ref[0, r, c * per:(c + 1) * per] for r in range(d)], axis=0)
        if exact:
            out.append(sum(_dot(pt_ref[...], part) for part in _split3(x)))
        else:
            out.append(_dot(pt_ref[...], x))
    return jnp.concatenate(out, axis=0)


def _tail_prompt_kernel(x_ref, o1_ref, l1_ref, o4_ref, l4_ref, o16_ref, l16_ref, ob_ref, pt4_ref, pt16_ref,
                        spread_ref, w_ref, g_ref, h_ref, xn_ref):
    o_g = [o1_ref[0].astype(F32), _unpermute(pt4_ref, o4_ref, False), _unpermute(pt16_ref, o16_ref, False)]
    l_g = [l1_ref[0], _unpermute(pt4_ref, l4_ref, True), _unpermute(pt16_ref, l16_ref, True)]
    m = jnp.maximum(jnp.maximum(l_g[0], l_g[1]), l_g[2])
    e_g = [jnp.exp(l - m) for l in l_g]
    den = e_g[0] + e_g[1] + e_g[2]
    oa = None
    for e, o in zip(e_g, o_g):
        w = e / den
        w_wide = sum(_dot(part, spread_ref[...]) for part in _split2(w))
        oa = w_wide * o if oa is None else oa + w_wide * o
    o = jnp.concatenate([oa.astype(MXU_DTYPE), ob_ref[0]], axis=-1)
    h = x_ref[0] + _dot(o, w_ref[...])
    h_ref[0] = h
    xn_ref[0] = _rms(h, g_ref[...])


def _tail_prompt(x, o1, l1, o4, l4, o16, l16, ob, w_out, g):
    n, s, dm = x.shape
    tb = TAIL_TB
    pt4 = jnp.asarray(_residue_perm(PERM_CHUNK[4], 4).T, MXU_DTYPE)
    pt16 = jnp.asarray(_residue_perm(PERM_CHUNK[16], 16).T, MXU_DTYPE)
    spread = np.zeros((LANES, W_A), np.float32)
    spread[np.arange(W_A) // HEAD_DIM, np.arange(W_A)] = 1.0
    spread = jnp.asarray(spread, MXU_DTYPE)
    nat = lambda w: pl.BlockSpec((1, tb, w), lambda i, j: (i, j, 0))
    res = lambda d, w: pl.BlockSpec((1, d, tb // d, w), lambda i, j: (i, 0, j, 0))
    const = lambda a: pl.BlockSpec(a.shape, lambda i, j: (0,) * a.ndim)
    return pl.pallas_call(
        _tail_prompt_kernel,
        grid=(n, s // tb),
        in_specs=[nat(dm), nat(W_A), nat(LANES), res(4, W_A), res(4, LANES), res(16, W_A), res(16, LANES),
                  nat(W_BQ), const(pt4), const(pt16), const(spread), const(w_out), const(g)],
        out_shape=[jax.ShapeDtypeStruct((n, s, dm), F32), jax.ShapeDtypeStruct((n, s, dm), F32)],
        out_specs=[nat(dm), nat(dm)],
        compiler_params=_cparams(("parallel", "parallel")),
        name="tail_prompt",
    )(x, o1, l1, o4, l4, o16, l16, ob, pt4, pt16, spread, w_out, g)


def _tail_sample_kernel(x_ref, ot_ref, w_ref, g_ref, h_ref, xn_ref):
    h = x_ref[...] + _dot(ot_ref[...].T.astype(MXU_DTYPE), w_ref[...])
    h_ref[...] = h
    xn_ref[...] = _rms(h, g_ref[...])


def _tail_sample(x, o_t, w_out, g):
    return pl.pallas_call(
        _tail_sample_kernel,
        out_shape=[jax.ShapeDtypeStruct(x.shape, F32), jax.ShapeDtypeStruct(x.shape, F32)],
        compiler_params=_cparams(None),
        name="tail_sample",
    )(x, o_t, w_out, g)


def _sorting_network(n):
    size = 1 << (n - 1).bit_length()

    def merge(lo, hi, r):
        step = r * 2
        if step < hi - lo:
            yield from merge(lo, hi, step)
            yield from merge(lo + r, hi, step)
            yield from ((i, i + r) for i in range(lo + r, hi - r, step))
        else:
            yield (lo, lo + r)

    def sort(lo, hi):
        if hi - lo >= 1:
            mid = lo + (hi - lo) // 2
            yield from sort(lo, mid)
            yield from sort(mid + 1, hi)
            yield from merge(lo, hi, 1)

    return [(i, j) for i, j in sort(0, size - 1) if j < n]


def _top_rows(s, count):
    tiles = [s[v * SUBLANES:(v + 1) * SUBLANES] for v in range(s.shape[0] // SUBLANES)]
    for i, j in _sorting_network(len(tiles)):
        tiles[i], tiles[j] = jnp.maximum(tiles[i], tiles[j]), jnp.minimum(tiles[i], tiles[j])
    rows = []
    for t in range(count):
        m = jnp.max(tiles[0], axis=0, keepdims=True)
        rows.append(m)
        if t + 1 < count:
            popped = tiles[0] == m
            depth = min(len(tiles), count - t - 1)
            for k in range(depth):
                below = tiles[k + 1] if k + 1 < len(tiles) else NEG_INF
                tiles[k] = jnp.where(popped, below, tiles[k])
    return rows


def _pad_rows(rows, count):
    pad = [jnp.full_like(rows[0], NEG_INF)] * (count - len(rows))
    return jnp.concatenate(list(rows) + pad, axis=0)


def _candidate_sums(top1, top2):
    k = PEER_TOPK + 1
    wide = -(-k // SUBLANES) * SUBLANES
    narrow = -(-(k // 2) // SUBLANES) * SUBLANES
    assert k // (narrow + 1) <= 1
    v2_wide = _pad_rows(top2, wide)
    v2_narrow = v2_wide[:narrow]
    parts = [top1[0] + v2_wide] + [top1[a] + v2_narrow for a in range(1, narrow)]
    parts.append(_pad_rows(top1[narrow:], -(-(k - narrow) // SUBLANES) * SUBLANES) + top2[0])
    return jnp.concatenate(parts, axis=0)


def _route_kernel(x_ref, wqh_ref, wql_ref, skh_ref, skl_ref, thr_ref, p1_ref, p2_ref):
    xh, xl = _split2(x_ref[...])
    wqh = wqh_ref[...]
    q_t = _dot_nt(wqh, xh) + _dot_nt(wqh, xl) + _dot_nt(wql_ref[...], xh)
    half = D_KEY // 2
    for h in range(PEER_HEADS):
        scores = []
        for c in range(2):
            qh, ql = _split2(q_t[h * D_KEY + c * half:h * D_KEY + (c + 1) * half, :])
            scores.append(_dot(skh_ref[c], qh) + _dot(skh_ref[c], ql) + _dot(skl_ref[c], qh))
        for g in range(q_t.shape[1] // LANES):
            s1, s2 = (s[:, g * LANES:(g + 1) * LANES] for s in scores)
            top1, top2 = _top_rows(s1, PEER_TOPK + 1), _top_rows(s2, PEER_TOPK + 1)
            best = _top_rows(_candidate_sums(top1, top2), PEER_TOPK + 1)
            tau = 0.5 * (best[PEER_TOPK - 1] + best[PEER_TOPK])
            z = sum(jnp.exp(b - best[0]) for b in best[:PEER_TOPK])
            log_norm = best[0] + jnp.log(z)
            m2 = top2[0]
            thr_ref[h, g] = jnp.exp((tau - m2) - s1)
            p1_ref[h, g] = jnp.exp(s1 + (m2 - log_norm))
            p2_ref[h, g] = jnp.exp(s2 - m2)


def _route(xn, wq_t_hi, wq_t_lo, sk_hi, sk_lo, tb):
    t, dm = xn.shape
    const = lambda a: pl.BlockSpec(a.shape, lambda i: (0,) * a.ndim)
    gspec = pl.BlockSpec((PEER_HEADS, tb // LANES, N_KEYS, LANES), lambda i: (0, i, 0, 0))
    gshape = jax.ShapeDtypeStruct((PEER_HEADS, t // LANES, N_KEYS, LANES), F32)
    return pl.pallas_call(
        _route_kernel,
        grid=(t // tb,),
        in_specs=[pl.BlockSpec((tb, dm), lambda i: (i, 0)), const(wq_t_hi), const(wq_t_lo), const(sk_hi), const(sk_lo)],
        out_shape=[gshape] * 3,
        out_specs=[gspec] * 3,
        compiler_params=_cparams(("parallel",)),
        name="peer_route",
    )(xn, wq_t_hi, wq_t_lo, sk_hi, sk_lo)


EXPERT_CHUNK = SUBLANES * N_KEYS
GATE_JBLOCK = 2 * SUBLANES
INV_SQRT2 = 0.7071067811865476


def _gate_rows(thr_ref, p1_ref, p2_ref, w_ref):
    nv = GATE_JBLOCK // SUBLANES

    def row_group(i8, carry):
        base = pl.multiple_of(i8 * SUBLANES, SUBLANES)
        thr_t = [thr_ref[h, 0, pl.ds(base, SUBLANES), :] for h in range(PEER_HEADS)]
        p1_t = [p1_ref[h, 0, pl.ds(base, SUBLANES), :] for h in range(PEER_HEADS)]
        for jb in range(N_KEYS // GATE_JBLOCK):
            w = [[None] * nv for _ in range(SUBLANES)]
            for h in range(PEER_HEADS):
                p2 = [p2_ref[h, 0, pl.ds(jb * GATE_JBLOCK + v * SUBLANES, SUBLANES), :] for v in range(nv)]
                for r in range(SUBLANES):
                    thr = jnp.broadcast_to(thr_t[h][r:r + 1, :], (SUBLANES, LANES))
                    p1 = jnp.broadcast_to(p1_t[h][r:r + 1, :], (SUBLANES, LANES))
                    for v in range(nv):
                        term = jnp.where(p2[v] >= thr, p1 * p2[v], 0.0)
                        w[r][v] = term if w[r][v] is None else w[r][v] + term
            for r in range(SUBLANES):
                row0 = pl.multiple_of((base + r) * N_KEYS + jb * GATE_JBLOCK, GATE_JBLOCK)
                w_ref[0, pl.ds(row0, GATE_JBLOCK), :] = jnp.concatenate(w[r], axis=0).astype(w_ref.dtype)
        return carry

    lax.fori_loop(0, thr_ref.shape[2] // SUBLANES, row_group, 0)


GATE_ROWS = N_KEYS // 2


def _gate_specs(t):
    rows = pl.BlockSpec((PEER_HEADS, 1, GATE_ROWS, LANES), lambda i, j: (0, i, j, 0))
    full = pl.BlockSpec((PEER_HEADS, 1, N_KEYS, LANES), lambda i, j: (0, i, 0, 0))
    out = pl.BlockSpec((1, GATE_ROWS * N_KEYS, LANES), lambda i, j: (i, j, 0))
    shape = jax.ShapeDtypeStruct((t // LANES, N_KEYS * N_KEYS, LANES), MXU_DTYPE)
    return [rows, rows, full], out, shape


def _gates(thr, p1, p2):
    t = thr.shape[1] * LANES
    in_specs, out_spec, shape = _gate_specs(t)
    return pl.pallas_call(
        _gate_rows,
        grid=(t // LANES, N_KEYS // GATE_ROWS),
        in_specs=in_specs, out_specs=out_spec, out_shape=shape,
        compiler_params=_cparams(("parallel", "parallel")),
        name="peer_gates",
    )(thr, p1, p2)


def _sample_and_gates_kernel(*refs):
    n_in, n_gate = 11, 3
    sample_in, gate_in = refs[:n_in], refs[n_in:n_in + n_gate]
    sample_out, gate_out = refs[n_in + n_gate:-1], refs[-1]
    _sample_kernel(*sample_in, *sample_out)
    _gate_rows(*gate_in, gate_out)


def _sample_and_gates(sample_args, thr, p1, p2):
    t = thr.shape[1] * LANES
    ns, nkv = sample_args[7].shape[0], sample_args[9].shape[1]
    s_in, s_out, s_shapes = _sample_specs(*sample_args)
    if t // LANES != ns or N_KEYS // GATE_ROWS != nkv:
        outs = pl.pallas_call(
            _sample_kernel, grid=(ns, nkv), in_specs=s_in, out_specs=s_out, out_shape=s_shapes,
            compiler_params=_cparams(("arbitrary", "arbitrary")), name="sample_attn",
        )(*sample_args)
        return list(outs) + [_gates(thr, p1, p2)]
    g_in, g_out, g_shape = _gate_specs(t)
    return pl.pallas_call(
        _sample_and_gates_kernel,
        grid=(ns, nkv),
        in_specs=s_in + g_in, out_specs=s_out + [g_out], out_shape=s_shapes + [g_shape],
        compiler_params=_cparams(("arbitrary", "arbitrary")),
        name="sample_attn_and_gates",
    )(*sample_args, thr, p1, p2)


def _activate(g, a_ref, p_ref, w_ref):
    blk = 8 * SUBLANES
    for b in range(a_ref.shape[1] // blk):
        rows = pl.ds(b * blk, blk)
        a = a_ref[g, rows, :]
        half = 0.5 * a
        act = half + half * lax.erf(a * INV_SQRT2)
        p_ref[g, rows, :] = w_ref[g, rows, :] * act.astype(p_ref.dtype)


def _experts_kernel(xn_ref, res_ref, g_ref, down_ref, upt_ref, w_a_ref, w_b_ref,
                    y_ref, xb_ref, a0_ref, a1_ref, pb0_ref, pb1_ref, acc_ref):
    i, j = pl.program_id(0), pl.program_id(1)
    ch = EXPERT_CHUNK
    ng = a0_ref.shape[0]

    @pl.when((i == 0) & (j == 0))
    def _():
        for ref in (a0_ref, a1_ref, pb0_ref, pb1_ref):
            ref[...] = jnp.zeros_like(ref)

    @pl.when(j == 0)
    def _():
        xb_ref[...] = xn_ref[...].astype(MXU_DTYPE)

    keep = j >= 1

    def tick(half, a_in_ref, a_out_ref, p_out_ref, p_in_ref, w_ref):
        p_in = jnp.concatenate([p_in_ref[gg] for gg in range(ng)], axis=1)
        upd = _dot(upt_ref[:, half * ch:(half + 1) * ch], p_in)
        nxt = _dot_nt(down_ref[pl.ds(half * ch, ch), :], xb_ref[...])
        for g in range(ng):
            a_out_ref[g] = nxt[:, g * LANES:(g + 1) * LANES]
            _activate(g, a_in_ref, p_out_ref, w_ref)
        return upd

    upd = tick(0, a1_ref, a0_ref, pb1_ref, pb0_ref, w_a_ref)
    upd = upd + tick(1, a0_ref, a1_ref, pb0_ref, pb1_ref, w_b_ref)
    acc_ref[...] = jnp.where(keep, acc_ref[...] + upd, 0.0)

    @pl.when(j == pl.num_programs(1) - 1)
    def _():
        y = res_ref[...] + acc_ref[...].T
        y_ref[...] = _rms(y, g_ref[...])


def _experts(xn, res, g, down, up_t, w, tb):
    t, dm = xn.shape
    ne = down.shape[0]
    step = 2 * EXPERT_CHUNK
    nj = ne // step
    ng = tb // LANES
    tok = pl.BlockSpec((tb, dm), lambda i, j: (i, 0))
    w_a = pl.BlockSpec((ng, EXPERT_CHUNK, LANES), lambda i, j: (i, jnp.maximum(2 * j - 1, 0), 0))
    w_b = pl.BlockSpec((ng, EXPERT_CHUNK, LANES), lambda i, j: (i, jnp.minimum(2 * j, 2 * nj - 1), 0))
    return pl.pallas_call(
        _experts_kernel,
        grid=(t // tb, nj + 1),
        in_specs=[tok, tok, pl.BlockSpec((1, dm), lambda i, j: (0, 0)),
                  pl.BlockSpec((step, dm), lambda i, j: (jnp.minimum(j, nj - 1), 0)),
                  pl.BlockSpec((dm, step), lambda i, j: (0, jnp.maximum(j - 1, 0))),
                  w_a, w_b],
        out_shape=jax.ShapeDtypeStruct((t, dm), F32),
        out_specs=tok,
        scratch_shapes=[pltpu.VMEM((tb, dm), MXU_DTYPE),
                        pltpu.VMEM((ng, EXPERT_CHUNK, LANES), F32), pltpu.VMEM((ng, EXPERT_CHUNK, LANES), F32),
                        pltpu.VMEM((ng, EXPERT_CHUNK, LANES), MXU_DTYPE),
                        pltpu.VMEM((ng, EXPERT_CHUNK, LANES), MXU_DTYPE),
                        pltpu.VMEM((dm, tb), F32)],
        compiler_params=_cparams(("arbitrary", "arbitrary")),
        name="peer_experts",
    )(xn, res, g, down, up_t, w, w)


PEER_TB = 512


def _cache_to_feature_major(c):
    return jnp.transpose(c, (0, 2, 3, 1))


def _cache_from_feature_major(c):
    return jnp.transpose(c, (0, 3, 1, 2))


def _layer(xp, xs, cache_a_k, cache_a_v, cache_b_k, cache_b_v, norm_attn, w_in, rel_bias, sinks, w_out, norm_ffn,
           w_peer_q, peer_sub_keys, peer_down, peer_up, g_final):
    n, s, dm = xp.shape
    ns = xs.shape[0]
    la, lb = cache_a_k.shape[1], cache_b_k.shape[1]

    c3 = 3 * W_A
    w_q_scaled = jnp.concatenate([w_in[:, :W_A] * SCALE, w_in[:, W_A:c3], w_in[:, c3:c3 + W_BQ] * SCALE,
                                  w_in[:, c3 + W_BQ:]], axis=1)
    assert W_BKV == LANES
    w_ext = w_q_scaled.astype(MXU_DTYPE)
    w_nat_t = w_q_scaled.T.astype(MXU_DTYPE)
    w_out_b = w_out.astype(MXU_DTYPE)
    g_attn, g_ffn = norm_attn[None, :], norm_ffn[None, :]
    wq_t = w_peer_q.T
    wq_t_hi = wq_t.astype(MXU_DTYPE)
    wq_t_lo = (wq_t - wq_t_hi.astype(F32)).astype(MXU_DTYPE)
    sk_hi = peer_sub_keys.astype(MXU_DTYPE)
    sk_lo = (peer_sub_keys - sk_hi.astype(F32)).astype(MXU_DTYPE)
    peer_w = (wq_t_hi, wq_t_lo, sk_hi, sk_lo, peer_down.astype(MXU_DTYPE), peer_up.T.astype(MXU_DTYPE))

    bidx_sa, mult_sa = _sample_tables(la)
    bidx_sb, _ = _sample_tables(lb)
    assert H_A == H_B
    *tabs_a, tab_b, tab_sa, tab_sb = _bias_tables(
        rel_bias, H_A, [(_window_bucket_matrix(d), 0) for d in DILATIONS]
        + [(_window_bucket_matrix(1), H_A), (bidx_sa, 0), (bidx_sb, H_A)])
    bias0_a, bias0_b = rel_bias[0, :H_A], rel_bias[0, H_A:]

    (q1, k1, v1, kf, vf, q4, k4, v4, q16, k16, v16, qb, qbs, kb, vb, vbs, kbf, vbf) = _qkv_prompt(
        xp, g_attn, w_ext, min(la, s), min(lb, s))
    o1, l1 = _swa(tabs_a[0], [q1], k1, [v1], HEAD_CFG_A)
    flat = lambda a: a.reshape((a.shape[0] * a.shape[1],) + a.shape[2:])
    o4, l4 = _swa(tabs_a[1], [flat(q4)], flat(k4), [flat(v4)], HEAD_CFG_A)
    o16, l16 = _swa(tabs_a[2], [flat(q16)], flat(k16), [flat(v16)], HEAD_CFG_A)
    ob = _swa(tab_b, [qb, qbs], kb, [vb, vbs], HEAD_CFG_B, sinks=sinks, want_lse=False)
    unflat = lambda a, d: a.reshape((n, d) + a.shape[1:])
    hp, xnp = _tail_prompt(xp, o1, l1, unflat(o4, 4), unflat(l4, 4), unflat(o16, 16), unflat(l16, 16), ob,
                           w_out_b, g_ffn)
    prompt_caches = (kf.reshape(n, -1, H_A, HEAD_DIM), vf.reshape(n, -1, H_A, HEAD_DIM),
                     kbf.reshape(n, -1, KV_B, HEAD_DIM), vbf.reshape(n, -1, KV_B, HEAD_DIM))

    hp, xnp = hp.reshape(n * s, dm), xnp.reshape(n * s, dm)
    route_w, (down_b, up_t_b) = peer_w[:4], peer_w[4:]
    routing_p = _route(xnp, *route_w, PEER_TB)

    hs_t = _qkv_sample(xs, g_attn, w_nat_t)
    sample_args = (bias0_a, bias0_b, sinks, tab_sa, jnp.asarray(mult_sa), tab_sb, hs_t,
                   _cache_to_feature_major(cache_a_k), _cache_to_feature_major(cache_a_v),
                   _cache_to_feature_major(cache_b_k), _cache_to_feature_major(cache_b_v))
    o_t, *shifted, gates_p = _sample_and_gates(sample_args, *routing_p)
    h_s, xn_s = _tail_sample(xs, o_t, w_out_b, g_ffn)
    sample_caches = tuple(_cache_from_feature_major(c) for c in shifted)

    y_p = _experts(xnp, hp, g_final, down_b, up_t_b, gates_p, PEER_TB)
    gates_s = _gates(*_route(xn_s, *route_w, LANES))
    y_s = _experts(xn_s, h_s, g_final, down_b, up_t_b, gates_s, LANES)
    return y_p.reshape(n, s, dm), y_s, prompt_caches, sample_caches


def kernel(x_prompt, x_sample, cache_a_k, cache_a_v, cache_b_k, cache_b_v, norm_attn, w_in, rel_bias, sinks, w_out,
           norm_ffn, w_peer_q, peer_sub_keys, peer_down, peer_up, norm_final):
    depth = w_in.shape[0]
    assert depth == 1, "single-layer trunk"
    n, s, dm = x_prompt.shape
    ns = x_sample.shape[0]
    assert x_sample.shape[1] == 1 and s % QKV_TB == 0 and (n * s) % PEER_TB == 0 and ns % LANES == 0
    l = 0
    y_prompt, y_sample, prompt_caches, sample_caches = _layer(
        x_prompt, x_sample[:, 0], cache_a_k[l], cache_a_v[l], cache_b_k[l], cache_b_v[l], norm_attn[l], w_in[l],
        rel_bias, sinks[l], w_out[l], norm_ffn[l], w_peer_q[l], peer_sub_keys[l], peer_down[l], peer_up[l],
        norm_final[None, :])
    return ((y_prompt, y_sample.reshape(ns, 1, dm)) + tuple(c[None] for c in prompt_caches)
            + tuple(c[None] for c in sample_caches))
```

```python
import functools
import math

import numpy as np
import jax
import jax.numpy as jnp
from jax import lax
from jax.experimental import pallas as pl
from jax.experimental.pallas import tpu as pltpu

HEAD_DIM = 64
H_A = 8
H_B = 8
KV_B = 2
G_B = H_B // KV_B
DILATIONS = (1, 4, 16)
SLOTS = 128
N_BUCKETS = 32
MAX_DISTANCE = 2048
N_KEYS = 128
PEER_HEADS = 8
PEER_TOPK = 16
D_KEY = 128
RMS_EPS = 1e-6
SCALE = HEAD_DIM ** -0.5
W_A = H_A * HEAD_DIM
W_BQ = H_B * HEAD_DIM
W_BKV = KV_B * HEAD_DIM

LANES = 128
SUBLANES = 8
VMEM_LIMIT_BYTES = 56 * 1024 * 1024

MXU_DTYPE = jnp.bfloat16
F32 = jnp.float32
NEG_INF = float("-inf")


def _cparams(sem):
    return pltpu.CompilerParams(dimension_semantics=sem, vmem_limit_bytes=VMEM_LIMIT_BYTES)


def _t5_bucket_np(dist):
    exact = N_BUCKETS // 2
    d = np.maximum(dist, 0)
    logd = np.log(np.maximum(d, 1).astype(np.float32) / np.float32(exact)) / np.float32(math.log(MAX_DISTANCE / exact))
    large = np.minimum(exact + (logd * np.float32(N_BUCKETS - exact)).astype(np.int32), N_BUCKETS - 1)
    return np.where(d < exact, d, large).astype(np.int32)


def _window_bucket_matrix(dilation):
    i = np.arange(SLOTS)[:, None]
    j = np.arange(2 * SLOTS)[None, :]
    dist = i - j + SLOTS
    ok = (dist >= 0) & (dist <= SLOTS)
    return np.where(ok, _t5_bucket_np(dist * dilation), -1).astype(np.int32)


def _sample_tables(cache_len):
    delta = cache_len - np.arange(cache_len)
    mult = np.zeros(cache_len, np.float32)
    for d in DILATIONS:
        mult += ((delta % d == 0) & (delta <= SLOTS * d)).astype(np.float32)
    return _t5_bucket_np(delta)[None, :], mult[None, :]


def _residue_perm(tb, d):
    p = np.zeros((tb, tb), np.float32)
    rows = np.arange(tb)
    p[rows, (rows % (tb // d)) * d + rows // (tb // d)] = 1.0
    return p


def _rms(x, g):
    return x * lax.rsqrt(jnp.mean(x * x, axis=-1, keepdims=True) + RMS_EPS) * g


def _dot(a, b):
    return jnp.dot(a, b, preferred_element_type=F32)


def _dot_nt(a, b):
    return lax.dot_general(a, b, (((1,), (1,)), ((), ())), preferred_element_type=F32)


def _split2(x):
    hi = x.astype(MXU_DTYPE)
    lo = (x - hi.astype(F32)).astype(MXU_DTYPE)
    return hi, lo


def _split3(x):
    hi = x.astype(MXU_DTYPE)
    r1 = x - hi.astype(F32)
    mid = r1.astype(MXU_DTYPE)
    lo = (r1 - mid.astype(F32)).astype(MXU_DTYPE)
    return hi, mid, lo


def _bias_kernel(col0s, nheads, rb_ref, *refs):
    n = len(col0s)
    for col0, bidx_ref, o_ref in zip(col0s, refs[:n], refs[n:]):
        b = bidx_ref[...]
        for h in range(nheads):
            val = jnp.full(b.shape, NEG_INF, F32)
            for k in range(N_BUCKETS):
                val = jnp.where(b == k, rb_ref[k, col0 + h], val)
            o_ref[h] = val


def _bias_tables(rel_bias, nheads, tables):
    bidx = [jnp.asarray(b) for b, _ in tables]
    return pl.pallas_call(
        functools.partial(_bias_kernel, tuple(c for _, c in tables), nheads),
        out_shape=[jax.ShapeDtypeStruct((nheads,) + b.shape, F32) for b in bidx],
        in_specs=[pl.BlockSpec(memory_space=pltpu.SMEM)] + [pl.BlockSpec(b.shape, lambda: (0, 0)) for b in bidx],
        out_specs=[pl.BlockSpec((nheads,) + b.shape, lambda: (0, 0, 0)) for b in bidx],
        name="bias_tables",
    )(rel_bias, *bidx)


QKV_TB = 512
PERM_CHUNK = {4: 128, 16: 256}


def _qkv_prompt_kernel(rows_a, rows_b, x_ref, g_ref, w_ref, p4_ref, p16_ref,
                       q1_ref, k1_ref, v1_ref, kf_ref, vf_ref,
                       q4_ref, k4_ref, v4_ref, q16_ref, k16_ref, v16_ref,
                       qb_ref, qbs_ref, kb_ref, vb_ref, vbs_ref, kbf_ref, vbf_ref):
    xn = _rms(x_ref[0], g_ref[...])
    h = _dot(xn.astype(MXU_DTYPE), w_ref[...])
    c0, c1, c2, c3 = 0, W_A, 2 * W_A, 3 * W_A
    c4, c5, c6 = c3 + W_BQ, c3 + W_BQ + W_BKV, c3 + W_BQ + 2 * W_BKV
    swap_halves = lambda t: pltpu.roll(t, HEAD_DIM, axis=1)
    ha = h[:, :c3].astype(MXU_DTYPE)
    q1_ref[0] = ha[:, c0:c1]
    k1_ref[0] = ha[:, c1:c2]
    v1_ref[0] = ha[:, c2:c3]
    kf_ref[0] = h[rows_a, c1:c2]
    vf_ref[0] = h[rows_a, c2:c3]
    qb_ref[0] = h[:, c3:c4].astype(MXU_DTYPE)
    kb_ref[0] = h[:, c4:c5].astype(MXU_DTYPE)
    vb_ref[0] = h[:, c5:c6].astype(MXU_DTYPE)
    kbf_ref[0] = h[rows_b, c4:c5]
    vbf_ref[0] = h[rows_b, c5:c6]
    qbs_ref[0] = jnp.concatenate([swap_halves(h[:, c:c + LANES]) for c in range(c3, c4, LANES)],
                                 axis=1).astype(MXU_DTYPE)
    vbs_ref[0] = swap_halves(h[:, c5:c6]).astype(MXU_DTYPE)
    tb = ha.shape[0]
    for d, p_ref, outs in ((4, p4_ref, (q4_ref, k4_ref, v4_ref)), (16, p16_ref, (q16_ref, k16_ref, v16_ref))):
        chunk = p_ref.shape[0]
        per = chunk // d
        for c in range(tb // chunk):
            perm = _dot(p_ref[...], ha[c * chunk:(c + 1) * chunk]).astype(MXU_DTYPE)
            for r in range(d):
                for t, o_ref in enumerate(outs):
                    o_ref[0, r, c * per:(c + 1) * per] = perm[r * per:(r + 1) * per, t * W_A:(t + 1) * W_A]


def _qkv_prompt(x, g, w_ext, keep_a, keep_b):
    n, s, dm = x.shape
    tb = QKV_TB
    nb = s // tb

    def tail(width, keep):
        if keep >= tb:
            assert keep % tb == 0
            first = nb - keep // tb
            spec = pl.BlockSpec((1, tb, width), lambda i, j: (i, jnp.maximum(j - first, 0), 0))
            return (jax.ShapeDtypeStruct((n, keep, width), F32), spec), slice(None)
        spec = pl.BlockSpec((1, keep, width), lambda i, j: (i, 0, 0))
        return (jax.ShapeDtypeStruct((n, keep, width), F32), spec), slice(tb - keep, tb)

    tail_a, rows_a = tail(W_A, keep_a)
    tail_b, rows_b = tail(W_BKV, keep_b)
    p4 = jnp.asarray(_residue_perm(PERM_CHUNK[4], 4), MXU_DTYPE)
    p16 = jnp.asarray(_residue_perm(PERM_CHUNK[16], 16), MXU_DTYPE)
    bf = MXU_DTYPE

    def nat(width, dt):
        return jax.ShapeDtypeStruct((n, s, width), dt), pl.BlockSpec((1, tb, width), lambda i, j: (i, j, 0))

    def res(d):
        return (jax.ShapeDtypeStruct((n, d, s // d, W_A), bf),
                pl.BlockSpec((1, d, tb // d, W_A), lambda i, j: (i, 0, j, 0)))

    outs = [nat(W_A, bf), nat(W_A, bf), nat(W_A, bf), tail_a, tail_a,
            res(4), res(4), res(4), res(16), res(16), res(16),
            nat(W_BQ, bf), nat(W_BQ, bf), nat(W_BKV, bf), nat(W_BKV, bf), nat(W_BKV, bf),
            tail_b, tail_b]
    return pl.pallas_call(
        functools.partial(_qkv_prompt_kernel, rows_a, rows_b),
        grid=(n, nb),
        in_specs=[pl.BlockSpec((1, tb, dm), lambda i, j: (i, j, 0)),
                  pl.BlockSpec((1, dm), lambda i, j: (0, 0)),
                  pl.BlockSpec(w_ext.shape, lambda i, j: (0, 0)),
                  pl.BlockSpec(p4.shape, lambda i, j: (0, 0)),
                  pl.BlockSpec(p16.shape, lambda i, j: (0, 0))],
        out_shape=[o[0] for o in outs],
        out_specs=[o[1] for o in outs],
        compiler_params=_cparams(("arbitrary", "arbitrary")),
        name="qkv_prompt",
    )(x, g, w_ext, p4, p16)


def _qkv_sample_kernel(x_ref, g_ref, wt_ref, o_ref):
    xn = _rms(x_ref[...], g_ref[...])
    o_ref[...] = _dot_nt(wt_ref[...], xn.astype(MXU_DTYPE))


def _qkv_sample(x, g, w_t):
    return pl.pallas_call(
        _qkv_sample_kernel,
        out_shape=jax.ShapeDtypeStruct((w_t.shape[0], x.shape[0]), F32),
        compiler_params=_cparams(None),
        name="qkv_sample",
    )(x, g, w_t)


SWA_TILES = 8


def _swa_kernel(head_cfg, has_sink, want_lse, *refs):
    it = iter(refs)
    tab_ref = next(it)
    sink_ref = next(it) if has_sink else None
    q_refs = [next(it)]
    if any(c[0] == 1 for c in head_cfg):
        q_refs.append(next(it))
    kc_ref, kp_ref = next(it), next(it)
    v_refs = [(next(it), next(it))]
    if any(c[3] == 1 for c in head_cfg):
        v_refs.append((next(it), next(it)))
    o_ref = next(it)
    lse_ref = next(it) if want_lse else None

    first = pl.program_id(1) == 0
    col = lax.broadcasted_iota(jnp.int32, (SLOTS, 2 * SLOTS), 1)
    prev_pen = jnp.where((col < SLOTS) & first, NEG_INF, 0.0)
    lane = lax.broadcasted_iota(jnp.int32, (SLOTS, LANES), 1)
    low = lane < HEAD_DIM
    for t in range(kc_ref.shape[1] // SLOTS):
        rows = slice(t * SLOTS, (t + 1) * SLOTS)
        before = slice((t - 1) * SLOTS, t * SLOTS)

        def keys(cur_ref, prev_ref, lanes):
            prev = prev_ref[0, :, lanes] if t == 0 else cur_ref[0, before, lanes]
            return jnp.concatenate([prev, cur_ref[0, rows, lanes]], axis=0)

        lse_acc = jnp.zeros((SLOTS, LANES), F32)
        for p in range(len(head_cfg) // 2):
            halves = []
            for hh in range(2):
                h = 2 * p + hh
                q_src, q_half, k_tile, v_src, v_tile = head_cfg[h]
                q = q_refs[q_src][0, rows, p * LANES:(p + 1) * LANES]
                q = jnp.where(low if q_half == 0 else jnp.logical_not(low), q, jnp.zeros_like(q))
                kcat = keys(kc_ref, kp_ref, slice(k_tile * LANES, (k_tile + 1) * LANES))
                s = _dot_nt(q, kcat) + tab_ref[h]
                if t == 0:
                    s = s + prev_pen
                m = jnp.max(s, axis=-1, keepdims=True)
                if has_sink:
                    m = jnp.maximum(m, sink_ref[h])
                e = jnp.exp(s - m)
                l = jnp.sum(e, axis=-1, keepdims=True)
                if has_sink:
                    l = l + jnp.exp(sink_ref[h] - m)
                vc_ref, vp_ref = v_refs[v_src]
                vcat = keys(vc_ref, vp_ref, slice(v_tile * LANES, (v_tile + 1) * LANES))
                halves.append(_dot(e.astype(MXU_DTYPE), vcat) / l)
                if want_lse:
                    lse_acc = jnp.where(lane == h, m + jnp.log(l), lse_acc)
            o_ref[0, rows, p * LANES:(p + 1) * LANES] = jnp.where(low, halves[0], halves[1]).astype(o_ref.dtype)
        if want_lse:
            lse_ref[0, rows] = lse_acc


def _swa(tab, q_list, k, v_list, head_cfg, sinks=None, want_lse=True):
    r, s, _ = q_list[0].shape
    ck = k.shape[-1]
    tiles = math.gcd(SWA_TILES, s // SLOTS)
    rows = tiles * SLOTS
    assert s % rows == 0
    nb = s // rows
    cur = lambda i, j: (i, j, 0)
    prev = lambda i, j: (i, jnp.maximum(tiles * j - 1, 0), 0)
    in_specs = [pl.BlockSpec(tab.shape, lambda i, j: (0, 0, 0))]
    args = [tab]
    if sinks is not None:
        in_specs.append(pl.BlockSpec(memory_space=pltpu.SMEM))
        args.append(sinks)
    for q in q_list:
        in_specs.append(pl.BlockSpec((1, rows, q.shape[-1]), cur))
        args.append(q)
    in_specs += [pl.BlockSpec((1, rows, ck), cur), pl.BlockSpec((1, SLOTS, ck), prev)]
    args += [k, k]
    for v in v_list:
        in_specs += [pl.BlockSpec((1, rows, ck), cur), pl.BlockSpec((1, SLOTS, ck), prev)]
        args += [v, v]
    out_shape = [jax.ShapeDtypeStruct((r, s, W_A), MXU_DTYPE)]
    out_specs = [pl.BlockSpec((1, rows, W_A), cur)]
    if want_lse:
        out_shape.append(jax.ShapeDtypeStruct((r, s, LANES), F32))
        out_specs.append(pl.BlockSpec((1, rows, LANES), cur))
    out = pl.pallas_call(
        functools.partial(_swa_kernel, tuple(head_cfg), sinks is not None, want_lse),
        grid=(r, nb),
        in_specs=in_specs,
        out_shape=out_shape,
        out_specs=out_specs,
        compiler_params=_cparams(("parallel", "arbitrary")),
        name="swa",
    )(*args)
    return out if want_lse else out[0]


HEAD_CFG_A = tuple((0, h % 2, h // 2, 0, h // 2) for h in range(H_A))


def _head_cfg_b():
    cfg = []
    for h in range(H_B):
        c = h // G_B
        src = 0 if h % 2 == c else 1
        cfg.append((src, c, 0, src, 0))
    return tuple(cfg)


HEAD_CFG_B = _head_cfg_b()


def _shift_in(x, new_col):
    length = x.shape[-1]
    rolled = pltpu.roll(x, length - 1, axis=1)
    lane = lax.broadcasted_iota(jnp.int32, x.shape, 1)
    return jnp.where(lane == length - 1, new_col, rolled)


def _col_attention(q, kmat, vmat, k_new, v_new, bias_row, mult_row, bias_new, mult_new, sink):
    s = jnp.sum(q * kmat, axis=0, keepdims=True) + bias_row
    s_new = jnp.sum(q * k_new, axis=0, keepdims=True) + bias_new
    if mult_row is not None:
        s = jnp.where(mult_row > 0.0, s, NEG_INF)
    m = jnp.maximum(jnp.max(s, axis=-1, keepdims=True), s_new)
    if sink is not None:
        m = jnp.maximum(m, sink)
    e = jnp.exp(s - m)
    if mult_row is not None:
        e = e * mult_row
    e_new = mult_new * jnp.exp(s_new - m)
    l = jnp.sum(e, axis=-1, keepdims=True) + e_new
    if sink is not None:
        l = l + jnp.exp(sink - m)
    o = jnp.sum(vmat * e, axis=-1, keepdims=True) + v_new * e_new
    return o / l


SAMPLE_HEADS = H_A // KV_B
assert SAMPLE_HEADS == G_B


def _column(ht_ref, row0, nrows, pick):
    return jnp.sum(jnp.where(pick, ht_ref[pl.ds(row0, nrows), :], 0.0), axis=1, keepdims=True)


def _sample_kernel(bias0a_ref, bias0b_ref, sink_ref, taba_ref, mult_ref, tabb_ref, ht_ref,
                   ka_ref, va_ref, kb_ref, vb_ref, ot_ref, kao_ref, vao_ref, kbo_ref, vbo_ref):
    n, hb = pl.program_id(0), pl.program_id(1)
    lane = n % LANES
    pick = lax.broadcasted_iota(jnp.int32, (1, LANES), 1) == lane
    rows = SAMPLE_HEADS * HEAD_DIM
    blk = pl.multiple_of(hb * rows, rows)
    kvr = pl.multiple_of(hb * HEAD_DIM, HEAD_DIM)

    @pl.when((lane == 0) & (hb == 0))
    def _():
        ot_ref[...] = jnp.zeros_like(ot_ref)

    def put(row0, col):
        ot_ref[pl.ds(row0, rows), :] = jnp.where(pick, col, ot_ref[pl.ds(row0, rows), :])

    q, k_new, v_new = (_column(ht_ref, c0 + blk, rows, pick) for c0 in (0, W_A, 2 * W_A))
    mult = mult_ref[...]
    outs = []
    for hl in range(SAMPLE_HEADS):
        r = slice(hl * HEAD_DIM, (hl + 1) * HEAD_DIM)
        kmat, vmat = ka_ref[0, hl], va_ref[0, hl]
        outs.append(_col_attention(q[r], kmat, vmat, k_new[r], v_new[r], taba_ref[hl], mult,
                                   bias0a_ref[hb * SAMPLE_HEADS + hl], float(len(DILATIONS)), None))
        kao_ref[0, hl] = _shift_in(kmat, k_new[r])
        vao_ref[0, hl] = _shift_in(vmat, v_new[r])
    put(blk, jnp.concatenate(outs, axis=0))

    c3 = 3 * W_A
    q = _column(ht_ref, c3 + blk, rows, pick)
    k_new = _column(ht_ref, c3 + W_BQ + kvr, HEAD_DIM, pick)
    v_new = _column(ht_ref, c3 + W_BQ + W_BKV + kvr, HEAD_DIM, pick)
    kmat, vmat = kb_ref[0, 0], vb_ref[0, 0]
    outs = []
    for g in range(G_B):
        h = hb * G_B + g
        outs.append(_col_attention(q[g * HEAD_DIM:(g + 1) * HEAD_DIM], kmat, vmat, k_new, v_new, tabb_ref[g], None,
                                   bias0b_ref[h], 1.0, sink_ref[h]))
    put(W_A + blk, jnp.concatenate(outs, axis=0))
    kbo_ref[0, 0] = _shift_in(kmat, k_new)
    vbo_ref[0, 0] = _shift_in(vmat, v_new)


def _sample_specs(bias0_a, bias0_b, sinks, tab_a, mult_a, tab_b, h_t, ka_t, va_t, kb_t, vb_t):
    ns, nh, hd, la = ka_t.shape
    _, nkv, _, lb = kb_t.shape
    assert nh // SAMPLE_HEADS == nkv and ns % LANES == 0
    smem = pl.BlockSpec(memory_space=pltpu.SMEM)
    lanes_of = lambda rows: pl.BlockSpec((rows, LANES), lambda i, j: (0, i // LANES))
    a_spec = pl.BlockSpec((1, SAMPLE_HEADS, hd, la), lambda i, j: (i, j, 0, 0))
    b_spec = pl.BlockSpec((1, 1, hd, lb), lambda i, j: (i, j, 0, 0))
    in_specs = [smem, smem, smem,
                pl.BlockSpec((SAMPLE_HEADS, 1, la), lambda i, j: (j, 0, 0)),
                pl.BlockSpec((1, la), lambda i, j: (0, 0)),
                pl.BlockSpec((G_B, 1, lb), lambda i, j: (j, 0, 0)),
                lanes_of(h_t.shape[0]), a_spec, a_spec, b_spec, b_spec]
    out_specs = [lanes_of(W_A + W_BQ), a_spec, a_spec, b_spec, b_spec]
    shapes = [jax.ShapeDtypeStruct((W_A + W_BQ, ns), F32)] + [jax.ShapeDtypeStruct(c.shape, F32)
                                                              for c in (ka_t, va_t, kb_t, vb_t)]
    return in_specs, out_specs, shapes


TAIL_TB = 512


def _unpermute(pt_ref, blocks_ref, exact):
    d = blocks_ref.shape[1]
    chunk = pt_ref.shape[0]
    per = chunk // d
    out = []
    for c in range(d * blocks_ref.shape[2] // chunk):
        x = jnp.concatenate([blocks_ref[0, r, c * per:(c + 1) * per] for r in range(d)], axis=0)
        if exact:
            out.append(sum(_dot(pt_ref[...], part) for part in _split3(x)))
        else:
            out.append(_dot(pt_ref[...], x))
    return jnp.concatenate(out, axis=0)


def _tail_prompt_kernel(x_ref, o1_ref, l1_ref, o4_ref, l4_ref, o16_ref, l16_ref, ob_ref, pt4_ref, pt16_ref,
                        spread_ref, w_ref, g_ref, h_ref, xn_ref):
    o_g = [o1_ref[0].astype(F32), _unpermute(pt4_ref, o4_ref, False), _unpermute(pt16_ref, o16_ref, False)]
    l_g = [l1_ref[0], _unpermute(pt4_ref, l4_ref, True), _unpermute(pt16_ref, l16_ref, True)]
    m = jnp.maximum(jnp.maximum(l_g[0], l_g[1]), l_g[2])
    e_g = [jnp.exp(l - m) for l in l_g]
    den = e_g[0] + e_g[1] + e_g[2]
    oa = None
    for e, o in zip(e_g, o_g):
        w = e / den
        w_wide = sum(_dot(part, spread_ref[...]) for part in _split2(w))
        oa = w_wide * o if oa is None else oa + w_wide * o
    o = jnp.concatenate([oa.astype(MXU_DTYPE), ob_ref[0]], axis=-1)
    h = x_ref[0] + _dot(o, w_ref[...])
    h_ref[0] = h
    xn_ref[0] = _rms(h, g_ref[...])


def _tail_prompt(x, o1, l1, o4, l4, o16, l16, ob, w_out, g):
    n, s, dm = x.shape
    tb = TAIL_TB
    pt4 = jnp.asarray(_residue_perm(PERM_CHUNK[4], 4).T, MXU_DTYPE)
    pt16 = jnp.asarray(_residue_perm(PERM_CHUNK[16], 16).T, MXU_DTYPE)
    spread = np.zeros((LANES, W_A), np.float32)
    spread[np.arange(W_A) // HEAD_DIM, np.arange(W_A)] = 1.0
    spread = jnp.asarray(spread, MXU_DTYPE)
    nat = lambda w: pl.BlockSpec((1, tb, w), lambda i, j: (i, j, 0))
    res = lambda d, w: pl.BlockSpec((1, d, tb // d, w), lambda i, j: (i, 0, j, 0))
    const = lambda a: pl.BlockSpec(a.shape, lambda i, j: (0,) * a.ndim)
    return pl.pallas_call(
        _tail_prompt_kernel,
        grid=(n, s // tb),
        in_specs=[nat(dm), nat(W_A), nat(LANES), res(4, W_A), res(4, LANES), res(16, W_A), res(16, LANES),
                  nat(W_BQ), const(pt4), const(pt16), const(spread), const(w_out), const(g)],
        out_shape=[jax.ShapeDtypeStruct((n, s, dm), F32), jax.ShapeDtypeStruct((n, s, dm), F32)],
        out_specs=[nat(dm), nat(dm)],
        compiler_params=_cparams(("parallel", "parallel")),
        name="tail_prompt",
    )(x, o1, l1, o4, l4, o16, l16, ob, pt4, pt16, spread, w_out, g)


def _tail_sample_kernel(x_ref, ot_ref, w_ref, g_ref, h_ref, xn_ref):
    h = x_ref[...] + _dot(ot_ref[...].T.astype(MXU_DTYPE), w_ref[...])
    h_ref[...] = h
    xn_ref[...] = _rms(h, g_ref[...])


def _tail_sample(x, o_t, w_out, g):
    return pl.pallas_call(
        _tail_sample_kernel,
        out_shape=[jax.ShapeDtypeStruct(x.shape, F32), jax.ShapeDtypeStruct(x.shape, F32)],
        compiler_params=_cparams(None),
        name="tail_sample",
    )(x, o_t, w_out, g)


def _sorting_network(n):
    size = 1 << (n - 1).bit_length()

    def merge(lo, hi, r):
        step = r * 2
        if step < hi - lo:
            yield from merge(lo, hi, step)
            yield from merge(lo + r, hi, step)
            yield from ((i, i + r) for i in range(lo + r, hi - r, step))
        else:
            yield (lo, lo + r)

    def sort(lo, hi):
        if hi - lo >= 1:
            mid = lo + (hi - lo) // 2
            yield from sort(lo, mid)
            yield from sort(mid + 1, hi)
            yield from merge(lo, hi, 1)

    return [(i, j) for i, j in sort(0, size - 1) if j < n]


def _top_rows(s, count):
    tiles = [s[v * SUBLANES:(v + 1) * SUBLANES] for v in range(s.shape[0] // SUBLANES)]
    for i, j in _sorting_network(len(tiles)):
        tiles[i], tiles[j] = jnp.maximum(tiles[i], tiles[j]), jnp.minimum(tiles[i], tiles[j])
    rows = []
    for t in range(count):
        m = jnp.max(tiles[0], axis=0, keepdims=True)
        rows.append(m)
        if t + 1 < count:
            popped = tiles[0] == m
            depth = min(len(tiles), count - t - 1)
            for k in range(depth):
                below = tiles[k + 1] if k + 1 < len(tiles) else NEG_INF
                tiles[k] = jnp.where(popped, below, tiles[k])
    return rows


def _pad_rows(rows, count):
    pad = [jnp.full_like(rows[0], NEG_INF)] * (count - len(rows))
    return jnp.concatenate(list(rows) + pad, axis=0)


def _candidate_sums(top1, top2):
    k = PEER_TOPK + 1
    wide = -(-k // SUBLANES) * SUBLANES
    narrow = -(-(k // 2) // SUBLANES) * SUBLANES
    assert k // (narrow + 1) <= 1
    v2_wide = _pad_rows(top2, wide)
    v2_narrow = v2_wide[:narrow]
    parts = [top1[0] + v2_wide] + [top1[a] + v2_narrow for a in range(1, narrow)]
    parts.append(_pad_rows(top1[narrow:], -(-(k - narrow) // SUBLANES) * SUBLANES) + top2[0])
    return jnp.concatenate(parts, axis=0)


def _route_kernel(x_ref, wqh_ref, wql_ref, skh_ref, skl_ref, thr_ref, p1_ref, p2_ref):
    xh, xl = _split2(x_ref[...])
    wqh = wqh_ref[...]
    q_t = _dot_nt(wqh, xh) + _dot_nt(wqh, xl) + _dot_nt(wql_ref[...], xh)
    half = D_KEY // 2
    for h in range(PEER_HEADS):
        scores = []
        for c in range(2):
            qh, ql = _split2(q_t[h * D_KEY + c * half:h * D_KEY + (c + 1) * half, :])
            scores.append(_dot(skh_ref[c], qh) + _dot(skh_ref[c], ql) + _dot(skl_ref[c], qh))
        for g in range(q_t.shape[1] // LANES):
            s1, s2 = (s[:, g * LANES:(g + 1) * LANES] for s in scores)
            top1, top2 = _top_rows(s1, PEER_TOPK + 1), _top_rows(s2, PEER_TOPK + 1)
            best = _top_rows(_candidate_sums(top1, top2), PEER_TOPK + 1)
            tau = 0.5 * (best[PEER_TOPK - 1] + best[PEER_TOPK])
            z = sum(jnp.exp(b - best[0]) for b in best[:PEER_TOPK])
            log_norm = best[0] + jnp.log(z)
            m2 = top2[0]
            thr_ref[h, g] = jnp.exp((tau - m2) - s1)
            p1_ref[h, g] = jnp.exp(s1 + (m2 - log_norm))
            p2_ref[h, g] = jnp.exp(s2 - m2)


def _route(xn, wq_t_hi, wq_t_lo, sk_hi, sk_lo, tb):
    t, dm = xn.shape
    const = lambda a: pl.BlockSpec(a.shape, lambda i: (0,) * a.ndim)
    gspec = pl.BlockSpec((PEER_HEADS, tb // LANES, N_KEYS, LANES), lambda i: (0, i, 0, 0))
    gshape = jax.ShapeDtypeStruct((PEER_HEADS, t // LANES, N_KEYS, LANES), F32)
    return pl.pallas_call(
        _route_kernel,
        grid=(t // tb,),
        in_specs=[pl.BlockSpec((tb, dm), lambda i: (i, 0)), const(wq_t_hi), const(wq_t_lo), const(sk_hi), const(sk_lo)],
        out_shape=[gshape] * 3,
        out_specs=[gspec] * 3,
        compiler_params=_cparams(("parallel",)),
        name="peer_route",
    )(xn, wq_t_hi, wq_t_lo, sk_hi, sk_lo)


EXPERT_CHUNK = SUBLANES * N_KEYS
GATE_JBLOCK = 2 * SUBLANES
INV_SQRT2 = 0.7071067811865476


def _gate_rows(thr_ref, p1_ref, p2_ref, w_ref):
    nv = GATE_JBLOCK // SUBLANES

    def row_group(i8, carry):
        base = pl.multiple_of(i8 * SUBLANES, SUBLANES)
        thr_t = [thr_ref[h, 0, pl.ds(base, SUBLANES), :] for h in range(PEER_HEADS)]
        p1_t = [p1_ref[h, 0, pl.ds(base, SUBLANES), :] for h in range(PEER_HEADS)]
        for jb in range(N_KEYS // GATE_JBLOCK):
            w = [[None] * nv for _ in range(SUBLANES)]
            for h in range(PEER_HEADS):
                p2 = [p2_ref[h, 0, pl.ds(jb * GATE_JBLOCK + v * SUBLANES, SUBLANES), :] for v in range(nv)]
                for r in range(SUBLANES):
                    thr = jnp.broadcast_to(thr_t[h][r:r + 1, :], (SUBLANES, LANES))
                    p1 = jnp.broadcast_to(p1_t[h][r:r + 1, :], (SUBLANES, LANES))
                    for v in range(nv):
                        term = jnp.where(p2[v] >= thr, p1 * p2[v], 0.0)
                        w[r][v] = term if w[r][v] is None else w[r][v] + term
            for r in range(SUBLANES):
                row0 = pl.multiple_of((base + r) * N_KEYS + jb * GATE_JBLOCK, GATE_JBLOCK)
                w_ref[0, pl.ds(row0, GATE_JBLOCK), :] = jnp.concatenate(w[r], axis=0).astype(w_ref.dtype)
        return carry

    lax.fori_loop(0, thr_ref.shape[2] // SUBLANES, row_group, 0)


GATE_ROWS = N_KEYS // 2


def _gate_specs(t):
    rows = pl.BlockSpec((PEER_HEADS, 1, GATE_ROWS, LANES), lambda i, j: (0, i, j, 0))
    full = pl.BlockSpec((PEER_HEADS, 1, N_KEYS, LANES), lambda i, j: (0, i, 0, 0))
    out = pl.BlockSpec((1, GATE_ROWS * N_KEYS, LANES), lambda i, j: (i, j, 0))
    shape = jax.ShapeDtypeStruct((t // LANES, N_KEYS * N_KEYS, LANES), MXU_DTYPE)
    return [rows, rows, full], out, shape


def _gates(thr, p1, p2):
    t = thr.shape[1] * LANES
    in_specs, out_spec, shape = _gate_specs(t)
    return pl.pallas_call(
        _gate_rows,
        grid=(t // LANES, N_KEYS // GATE_ROWS),
        in_specs=in_specs, out_specs=out_spec, out_shape=shape,
        compiler_params=_cparams(("parallel", "parallel")),
        name="peer_gates",
    )(thr, p1, p2)


def _sample_and_gates_kernel(*refs):
    n_in, n_gate = 11, 3
    sample_in, gate_in = refs[:n_in], refs[n_in:n_in + n_gate]
    sample_out, gate_out = refs[n_in + n_gate:-1], refs[-1]
    _sample_kernel(*sample_in, *sample_out)
    _gate_rows(*gate_in, gate_out)


def _sample_and_gates(sample_args, thr, p1, p2):
    t = thr.shape[1] * LANES
    ns, nkv = sample_args[7].shape[0], sample_args[9].shape[1]
    s_in, s_out, s_shapes = _sample_specs(*sample_args)
    if t // LANES != ns or N_KEYS // GATE_ROWS != nkv:
        outs = pl.pallas_call(
            _sample_kernel, grid=(ns, nkv), in_specs=s_in, out_specs=s_out, out_shape=s_shapes,
            compiler_params=_cparams(("arbitrary", "arbitrary")), name="sample_attn",
        )(*sample_args)
        return list(outs) + [_gates(thr, p1, p2)]
    g_in, g_out, g_shape = _gate_specs(t)
    return pl.pallas_call(
        _sample_and_gates_kernel,
        grid=(ns, nkv),
        in_specs=s_in + g_in, out_specs=s_out + [g_out], out_shape=s_shapes + [g_shape],
        compiler_params=_cparams(("arbitrary", "arbitrary")),
        name="sample_attn_and_gates",
    )(*sample_args, thr, p1, p2)


def _activate(g, a_ref, p_ref, w_ref):
    blk = 8 * SUBLANES
    for b in range(a_ref.shape[1] // blk):
        rows = pl.ds(b * blk, blk)
        a = a_ref[g, rows, :]
        half = 0.5 * a
        act = half + half * lax.erf(a * INV_SQRT2)
        p_ref[g, rows, :] = w_ref[g, rows, :] * act.astype(p_ref.dtype)


def _experts_kernel(xn_ref, res_ref, g_ref, down_ref, upt_ref, w_a_ref, w_b_ref,
                    y_ref, xb_ref, a0_ref, a1_ref, pb0_ref, pb1_ref, acc_ref):
    i, j = pl.program_id(0), pl.program_id(1)
    ch = EXPERT_CHUNK
    ng = a0_ref.shape[0]

    @pl.when((i == 0) & (j == 0))
    def _():
        for ref in (a0_ref, a1_ref, pb0_ref, pb1_ref):
            ref[...] = jnp.zeros_like(ref)

    @pl.when(j == 0)
    def _():
        xb_ref[...] = xn_ref[...].astype(MXU_DTYPE)

    keep = j >= 1

    def tick(half, a_in_ref, a_out_ref, p_out_ref, p_in_ref, w_ref):
        p_in = jnp.concatenate([p_in_ref[gg] for gg in range(ng)], axis=1)
        upd = _dot(upt_ref[:, half * ch:(half + 1) * ch], p_in)
        nxt = _dot_nt(down_ref[pl.ds(half * ch, ch), :], xb_ref[...])
        for g in range(ng):
            a_out_ref[g] = nxt[:, g * LANES:(g + 1) * LANES]
            _activate(g, a_in_ref, p_out_ref, w_ref)
        return upd

    upd = tick(0, a1_ref, a0_ref, pb1_ref, pb0_ref, w_a_ref)
    upd = upd + tick(1, a0_ref, a1_ref, pb0_ref, pb1_ref, w_b_ref)
    acc_ref[...] = jnp.where(keep, acc_ref[...] + upd, 0.0)

    @pl.when(j == pl.num_programs(1) - 1)
    def _():
        y = res_ref[...] + acc_ref[...].T
        y_ref[...] = _rms(y, g_ref[...])


def _experts(xn, res, g, down, up_t, w, tb):
    t, dm = xn.shape
    ne = down.shape[0]
    step = 2 * EXPERT_CHUNK
    nj = ne // step
    ng = tb // LANES
    tok = pl.BlockSpec((tb, dm), lambda i, j: (i, 0))
    w_a = pl.BlockSpec((ng, EXPERT_CHUNK, LANES), lambda i, j: (i, jnp.maximum(2 * j - 1, 0), 0))
    w_b = pl.BlockSpec((ng, EXPERT_CHUNK, LANES), lambda i, j: (i, jnp.minimum(2 * j, 2 * nj - 1), 0))
    return pl.pallas_call(
        _experts_kernel,
        grid=(t // tb, nj + 1),
        in_specs=[tok, tok, pl.BlockSpec((1, dm), lambda i, j: (0, 0)),
                  pl.BlockSpec((step, dm), lambda i, j: (jnp.minimum(j, nj - 1), 0)),
                  pl.BlockSpec((dm, step), lambda i, j: (0, jnp.maximum(j - 1, 0))),
                  w_a, w_b],
        out_shape=jax.ShapeDtypeStruct((t, dm), F32),
        out_specs=tok,
        scratch_shapes=[pltpu.VMEM((tb, dm), MXU_DTYPE),
                        pltpu.VMEM((ng, EXPERT_CHUNK, LANES), F32), pltpu.VMEM((ng, EXPERT_CHUNK, LANES), F32),
                        pltpu.VMEM((ng, EXPERT_CHUNK, LANES), MXU_DTYPE),
                        pltpu.VMEM((ng, EXPERT_CHUNK, LANES), MXU_DTYPE),
                        pltpu.VMEM((dm, tb), F32)],
        compiler_params=_cparams(("arbitrary", "arbitrary")),
        name="peer_experts",
    )(xn, res, g, down, up_t, w, w)


PEER_TB = 512


def _cache_to_feature_major(c):
    return jnp.transpose(c, (0, 2, 3, 1))


def _cache_from_feature_major(c):
    return jnp.transpose(c, (0, 3, 1, 2))


def _layer(xp, xs, cache_a_k, cache_a_v, cache_b_k, cache_b_v, norm_attn, w_in, rel_bias, sinks, w_out, norm_ffn,
           w_peer_q, peer_sub_keys, peer_down, peer_up, g_final):
    n, s, dm = xp.shape
    ns = xs.shape[0]
    la, lb = cache_a_k.shape[1], cache_b_k.shape[1]

    c3 = 3 * W_A
    w_q_scaled = jnp.concatenate([w_in[:, :W_A] * SCALE, w_in[:, W_A:c3], w_in[:, c3:c3 + W_BQ] * SCALE,
                                  w_in[:, c3 + W_BQ:]], axis=1)
    assert W_BKV == LANES
    w_ext = w_q_scaled.astype(MXU_DTYPE)
    w_nat_t = w_q_scaled.T.astype(MXU_DTYPE)
    w_out_b = w_out.astype(MXU_DTYPE)
    g_attn, g_ffn = norm_attn[None, :], norm_ffn[None, :]
    wq_t = w_peer_q.T
    wq_t_hi = wq_t.astype(MXU_DTYPE)
    wq_t_lo = (wq_t - wq_t_hi.astype(F32)).astype(MXU_DTYPE)
    sk_hi = peer_sub_keys.astype(MXU_DTYPE)
    sk_lo = (peer_sub_keys - sk_hi.astype(F32)).astype(MXU_DTYPE)
    peer_w = (wq_t_hi, wq_t_lo, sk_hi, sk_lo, peer_down.astype(MXU_DTYPE), peer_up.T.astype(MXU_DTYPE))

    bidx_sa, mult_sa = _sample_tables(la)
    bidx_sb, _ = _sample_tables(lb)
    assert H_A == H_B
    *tabs_a, tab_b, tab_sa, tab_sb = _bias_tables(
        rel_bias, H_A, [(_window_bucket_matrix(d), 0) for d in DILATIONS]
        + [(_window_bucket_matrix(1), H_A), (bidx_sa, 0), (bidx_sb, H_A)])
    bias0_a, bias0_b = rel_bias[0, :H_A], rel_bias[0, H_A:]

    (q1, k1, v1, kf, vf, q4, k4, v4, q16, k16, v16, qb, qbs, kb, vb, vbs, kbf, vbf) = _qkv_prompt(
        xp, g_attn, w_ext, min(la, s), min(lb, s))
    o1, l1 = _swa(tabs_a[0], [q1], k1, [v1], HEAD_CFG_A)
    flat = lambda a: a.reshape((a.shape[0] * a.shape[1],) + a.shape[2:])
    o4, l4 = _swa(tabs_a[1], [flat(q4)], flat(k4), [flat(v4)], HEAD_CFG_A)
    o16, l16 = _swa(tabs_a[2], [flat(q16)], flat(k16), [flat(v16)], HEAD_CFG_A)
    ob = _swa(tab_b, [qb, qbs], kb, [vb, vbs], HEAD_CFG_B, sinks=sinks, want_lse=False)
    unflat = lambda a, d: a.reshape((n, d) + a.shape[1:])
    hp, xnp = _tail_prompt(xp, o1, l1, unflat(o4, 4), unflat(l4, 4), unflat(o16, 16), unflat(l16, 16), ob,
                           w_out_b, g_ffn)
    prompt_caches = (kf.reshape(n, -1, H_A, HEAD_DIM), vf.reshape(n, -1, H_A, HEAD_DIM),
                     kbf.reshape(n, -1, KV_B, HEAD_DIM), vbf.reshape(n, -1, KV_B, HEAD_DIM))

    hp, xnp = hp.reshape(n * s, dm), xnp.reshape(n * s, dm)
    route_w, (down_b, up_t_b) = peer_w[:4], peer_w[4:]
    routing_p = _route(xnp, *route_w, PEER_TB)

    hs_t = _qkv_sample(xs, g_attn, w_nat_t)
    sample_args = (bias0_a, bias0_b, sinks, tab_sa, jnp.asarray(mult_sa), tab_sb, hs_t,
                   _cache_to_feature_major(cache_a_k), _cache_to_feature_major(cache_a_v),
                   _cache_to_feature_major(cache_b_k), _cache_to_feature_major(cache_b_v))
    o_t, *shifted, gates_p = _sample_and_gates(sample_args, *routing_p)
    h_s, xn_s = _tail_sample(xs, o_t, w_out_b, g_ffn)
    sample_caches = tuple(_cache_from_feature_major(c) for c in shifted)

    y_p = _experts(xnp, hp, g_final, down_b, up_t_b, gates_p, PEER_TB)
    gates_s = _gates(*_route(xn_s, *route_w, LANES))
    y_s = _experts(xn_s, h_s, g_final, down_b, up_t_b, gates_s, LANES)
    return y_p.reshape(n, s, dm), y_s, prompt_caches, sample_caches


def kernel(x_prompt, x_sample, cache_a_k, cache_a_v, cache_b_k, cache_b_v, norm_attn, w_in, rel_bias, sinks, w_out,
           norm_ffn, w_peer_q, peer_sub_keys, peer_down, peer_up, norm_final):
    depth = w_in.shape[0]
    assert depth == 1, "single-layer trunk"
    n, s, dm = x_prompt.shape
    ns = x_sample.shape[0]
    assert x_sample.shape[1] == 1 and s % QKV_TB == 0 and (n * s) % PEER_TB == 0 and ns % LANES == 0
    l = 0
    y_prompt, y_sample, prompt_caches, sample_caches = _layer(
        x_prompt, x_sample[:, 0], cache_a_k[l], cache_a_v[l], cache_b_k[l], cache_b_v[l], norm_attn[l], w_in[l],
        rel_bias, sinks[l], w_out[l], norm_ffn[l], w_peer_q[l], peer_sub_keys[l], peer_down[l], peer_up[l],
        norm_final[None, :])
    return ((y_prompt, y_sample.reshape(ns, 1, dm)) + tuple(c[None] for c in prompt_caches)
            + tuple(c[None] for c in sample_caches))
```
